```python
import math
import jax
import jax.numpy as jnp
from jax import lax
import numpy as np

D_MODEL = 2048
BATCH = 1
SEQ = 16384
DEPTH = 2

CTX_LEN = 256
GRID_W = 64
EPS = 1e-6

SSD_HEADS = 16
SSD_HEAD_DIM = 64
SSD_INNER = SSD_HEADS * SSD_HEAD_DIM
SSD_GROUPS = 2
SSD_HPG = SSD_HEADS // SSD_GROUPS
SSD_STATE = 128
SSD_CONV = 3
SSD_CHUNK = 128
SSD_XBC = SSD_INNER + 2 * SSD_GROUPS * SSD_STATE
SSD_DT = 2 * SSD_HEADS
MLA_HEADS = 8
MLA_NOPE = 128
MLA_ROPE = 64
MLA_V = 128
MLA_Q_RANK = 512
MLA_KV_RANK = 512
ROPE_THETA = 10000.0
ATTN_BLOCK = 128
Q_SIDE = SSD_INNER + MLA_Q_RANK
KV_SIDE = SSD_XBC + SSD_DT + MLA_KV_RANK + MLA_ROPE
IN_COLS = Q_SIDE + KV_SIDE
MIX_WIDTH = MLA_HEADS * MLA_V + SSD_INNER
HY_ORDER = 2
HY_SHORT = 3
HY_EMB = 33
HY_HID = 64
HY_INNER = 2
HY_FAST_DECAY = 0.3
HY_SLOW_DECAY = 1.5
HY_TARGET = 1e-2
D_FF = -(-8 * D_MODEL // (3 * 256)) * 256
N_EVEN = (DEPTH + 1) // 2
N_ODD = DEPTH // 2

kernel_name = 'hybrid_ssd_mla_hyena_prefix_dit'


def rms_norm(x, w):
    xf = x.astype(jnp.float32)
    y = xf * lax.rsqrt(jnp.mean(xf * xf, axis=-1, keepdims=True) + EPS)
    return (y * w.astype(jnp.float32)).astype(x.dtype)


def modulate(h, shift, scale):
    return h * (1.0 + scale) + shift


def swiglu(h, w1, w3, w2):
    return (jax.nn.silu(h @ w1) * (h @ w3)) @ w2


def dwconv_centred(x, w, b):
    k = w.shape[0]
    y = lax.conv_general_dilated(x, w[:, None, :].astype(x.dtype), window_strides=(1,),
                                 padding=[(k // 2, k // 2)], dimension_numbers=('NWC', 'WIO', 'NWC'),
                                 feature_group_count=x.shape[-1])
    return y + b


def axial_rope(rows):
    n_freq = MLA_ROPE // 4
    row = jnp.repeat(jnp.arange(rows, dtype=jnp.float32), GRID_W)
    col = jnp.tile(jnp.arange(GRID_W, dtype=jnp.float32), rows)
    inv = ROPE_THETA ** (-jnp.arange(n_freq, dtype=jnp.float32) / n_freq)
    ang = jnp.stack([row[:, None] * inv, col[:, None] * inv], axis=1)
    return jnp.cos(ang), jnp.sin(ang)


def apply_rope(x, cos, sin):
    shp = x.shape
    xr = x.reshape(shp[:-1] + (2, 2, MLA_ROPE // 4))
    x1, x2 = xr[..., 0, :], xr[..., 1, :]
    cos = cos.astype(x.dtype)
    sin = sin.astype(x.dtype)
    out = jnp.stack([x1 * cos - x2 * sin, x1 * sin + x2 * cos], axis=-2)
    return out.reshape(shp)


def block_attention(q, k, v):
    nb, lq, nh, dk = q.shape
    scale = dk ** -0.5
    qb = q.reshape(nb, lq // ATTN_BLOCK, ATTN_BLOCK, nh, dk).transpose(1, 0, 2, 3, 4)

    def one(qi):
        s = jnp.einsum('bqhd,bkhd->bhqk', qi, k).astype(jnp.float32) * scale
        p = jax.nn.softmax(s, axis=-1).astype(v.dtype)
        return jnp.einsum('bhqk,bkhv->bqhv', p, v)

    o = lax.map(one, qb)
    return o.transpose(1, 0, 2, 3, 4).reshape(nb, lq, nh, v.shape[-1])


def ssd_inputs(xbc_raw, dt_raw, conv_w, conv_b, dt_bias):
    nb, seq_len, _ = xbc_raw.shape
    xbc = jax.nn.silu(dwconv_centred(xbc_raw, conv_w, conv_b))
    gn = SSD_GROUPS * SSD_STATE
    xs = xbc[..., :SSD_INNER].reshape(nb, seq_len, SSD_GROUPS, SSD_HPG, SSD_HEAD_DIM)
    bm = xbc[..., SSD_INNER:SSD_INNER + gn].reshape(nb, seq_len, SSD_GROUPS, SSD_STATE)
    cm = xbc[..., SSD_INNER + gn:].reshape(nb, seq_len, SSD_GROUPS, SSD_STATE)
    dt = jax.nn.softplus((dt_raw + dt_bias).astype(jnp.float32)).reshape(nb, seq_len, 2, SSD_GROUPS, SSD_HPG)
    return xs, bm, cm, dt[:, :, 0], dt[:, :, 1]


def ssd_chunked(xs, dt, a, bm, cm, h0):
    nb, seq_len, g, r, p = xs.shape
    n = bm.shape[-1]
    q = SSD_CHUNK
    nc = seq_len // q
    xdt = (xs.astype(jnp.float32) * dt[..., None]).reshape(nb, nc, q, g, r, p)
    bc = bm.astype(jnp.float32).reshape(nb, nc, q, g, n)
    cc = cm.astype(jnp.float32).reshape(nb, nc, q, g, n)
    a_cum = jnp.cumsum((dt * a).reshape(nb, nc, q, g, r), axis=2)
    lower = jnp.tril(jnp.ones((q, q), dtype=bool))[:, :, None, None]
    seg = a_cum[:, :, :, None] - a_cum[:, :, None, :]
    decay = jnp.exp(jnp.where(lower, seg, -jnp.inf))
    cb = jnp.einsum('bcign,bcjgn->bcijg', cc, bc)
    y_diag = jnp.einsum('bcijgr,bcjgrp->bcigrp', cb[..., None] * decay, xdt)
    to_end = jnp.exp(a_cum[:, :, -1:] - a_cum)
    states = jnp.einsum('bcqgn,bcqgrp->bcgrpn', bc, xdt * to_end[..., None])
    chunk_decay = jnp.exp(a_cum[:, :, -1])

    def step(h, inp):
        s_c, d_c = inp
        return h * d_c[..., None, None] + s_c, h

    _, h_prev = lax.scan(step, h0.astype(jnp.float32),
                         (jnp.moveaxis(states, 1, 0), jnp.moveaxis(chunk_decay, 1, 0)))
    h_prev = jnp.moveaxis(h_prev, 0, 1)
    y_off = jnp.einsum('bcign,bcgrpn->bcigrp', cc, h_prev) * jnp.exp(a_cum)[..., None]
    return (y_diag + y_off).reshape(nb, seq_len, g, r, p).astype(xs.dtype)


def ssd_final_state(xs, dt, a, bm):
    a_cum = jnp.cumsum(dt * a, axis=1)
    w = jnp.exp(a_cum[:, -1:] - a_cum) * dt
    return jnp.einsum('blgn,blgrp->bgrpn', bm.astype(jnp.float32), xs.astype(jnp.float32) * w[..., None])


def bi_ssd(xs, dt_f, dt_b, bm, cm, a, d_skip, h0_f, h0_b):
    rev = lambda t: jnp.flip(t, axis=1)
    y_f = ssd_chunked(xs, dt_f, a[0], bm, cm, h0_f)
    y_b = rev(ssd_chunked(rev(xs), rev(dt_b), a[1], rev(bm), rev(cm), h0_b))
    return y_f + y_b + (d_skip[0] + d_skip[1])[..., None] * xs


def ssd_gated_norm(y, z, w):
    nb, seq_len = z.shape[:2]
    gy = y.reshape(nb, seq_len, SSD_GROUPS, -1) * jax.nn.silu(z.reshape(nb, seq_len, SSD_GROUPS, -1))
    return rms_norm(gy, w.reshape(SSD_GROUPS, -1)).reshape(nb, seq_len, SSD_INNER)


def mla_q(cq, q_norm_w, w_uq, rope):
    nb, seq_len, _ = cq.shape
    q = (rms_norm(cq, q_norm_w) @ w_uq).reshape(nb, seq_len, MLA_HEADS, MLA_NOPE + MLA_ROPE)
    if rope is None:
        return q
    cos, sin = rope
    return jnp.concatenate([q[..., :MLA_NOPE], apply_rope(q[..., MLA_NOPE:], cos[:, None], sin[:, None])], axis=-1)


def mla_kv(ckv, kr, kv_norm_w, w_ukv, rope):
    nb, seq_len, _ = ckv.shape
    kv = (rms_norm(ckv, kv_norm_w) @ w_ukv).reshape(nb, seq_len, MLA_HEADS, MLA_NOPE + MLA_V)
    k_nope, v = kv[..., :MLA_NOPE], kv[..., MLA_NOPE:]
    if rope is not None:
        kr = apply_rope(kr, rope[0], rope[1])
    k_rope = jnp.broadcast_to(kr[:, :, None, :], (nb, seq_len, MLA_HEADS, MLA_ROPE))
    return jnp.concatenate([k_nope, k_rope], axis=-1), v


def even_mixer(n_lat, n_ctx, rope, ctx_out, w_in, conv_w, conv_b, dt_bias, a_log, d_skip, ssd_norm_w,
               q_norm_w, w_uq, kv_norm_w, w_ukv, w_o):
    a = -jnp.exp(a_log.astype(jnp.float32)).reshape(2, SSD_GROUPS, SSD_HPG)
    d_sk = d_skip.reshape(2, SSD_GROUPS, SSD_HPG)
    o1 = SSD_XBC
    o2 = o1 + SSD_DT
    o3 = o2 + MLA_KV_RANK

    def kv_side(pk, rope_):
        xs, bm, cm, dt_f, dt_b = ssd_inputs(pk[..., :o1], pk[..., o1:o2], conv_w, conv_b, dt_bias)
        k, v = mla_kv(pk[..., o2:o3], pk[..., o3:], kv_norm_w, w_ukv, rope_)
        return xs, bm, cm, dt_f, dt_b, k, v

    def q_side(pq, rope_):
        return pq[..., :SSD_INNER], mla_q(pq[..., SSD_INNER:], q_norm_w, w_uq, rope_)

    def merge(z, y_ssd, o_att):
        nb, seq_len = z.shape[:2]
        heads = jnp.concatenate([o_att.reshape(nb, seq_len, -1), ssd_gated_norm(y_ssd, z, ssd_norm_w)], axis=-1)
        return heads @ w_o

    rev = lambda t: jnp.flip(t, axis=1)
    if ctx_out:
        p_ctx = n_ctx @ w_in
        pq_c, pk_c = p_ctx[..., :Q_SIDE], p_ctx[..., Q_SIDE:]
    else:
        pk_c = n_ctx @ w_in[:, Q_SIDE:]
    xs_c, b_c, c_c, dtf_c, dtb_c, k_c, v_c = kv_side(pk_c, None)
    h0_f = ssd_final_state(xs_c, dtf_c, a[0], b_c)
    h0_b = ssd_final_state(rev(xs_c), rev(dtb_c), a[1], rev(b_c))
    p_lat = n_lat @ w_in
    z, q = q_side(p_lat[..., :Q_SIDE], rope)
    xs, bm, cm, dt_f, dt_b, k, v = kv_side(p_lat[..., Q_SIDE:], rope)
    y = bi_ssd(xs, dt_f, dt_b, bm, cm, a, d_sk, h0_f, h0_b)
    o = block_attention(q, jnp.concatenate([k_c, k], axis=1), jnp.concatenate([v_c, v], axis=1))
    out_lat = merge(z, y, o)
    if not ctx_out:
        return out_lat, None
    z_c, q_c = q_side(pq_c, None)
    zero = jnp.zeros_like(h0_f)
    y_c = bi_ssd(xs_c, dtf_c, dtb_c, b_c, c_c, a, d_sk, zero, zero)
    o_c = block_attention(q_c, k_c, v_c)
    return out_lat, merge(z_c, y_c, o_c)


def hyena_filter_hidden(seq_len, fw1, fb1, fw_mid, fb_mid, freq):
    t = jnp.linspace(0.0, 1.0, seq_len, dtype=jnp.float32)[:, None]
    bands = (HY_EMB - 1) // 2
    w = 2.0 * math.pi * jnp.arange(seq_len, dtype=jnp.float32)[:, None] / seq_len
    f = jnp.linspace(1e-4, bands - 1, bands, dtype=jnp.float32)[None, :]
    emb = jnp.concatenate([t, jnp.cos(f * w), -jnp.sin(f * w)], axis=-1)
    fr = freq.astype(jnp.float32)
    hid = jnp.sin(fr * (emb @ fw1.astype(jnp.float32) + fb1.astype(jnp.float32)))
    for j in range(HY_INNER):
        hid = jnp.sin(fr * (hid @ fw_mid[j].astype(jnp.float32) + fb_mid[j].astype(jnp.float32)))
    return hid, t


def hyena_decay(t):
    lo = math.log(HY_SLOW_DECAY) / HY_TARGET
    hi = math.log(HY_FAST_DECAY) / HY_TARGET
    deltas = jnp.abs(jnp.linspace(lo, hi, D_MODEL, dtype=jnp.float32))
    return jnp.exp(-t * deltas)


def long_conv_bidir(u, h_two):
    seq_len, d = u.shape[1], u.shape[2]
    k = jnp.concatenate([h_two[:, 0], jnp.zeros((1, d), h_two.dtype), jnp.flip(h_two[1:, 1], axis=0)], axis=0)
    k = k / jnp.sum(jnp.abs(k), axis=0, keepdims=True)
    kf = jnp.fft.rfft(k, axis=0)
    uf = jnp.fft.rfft(u.astype(jnp.float32), n=2 * seq_len, axis=1)
    return jnp.fft.irfft(uf * kf, n=2 * seq_len, axis=1)[:, :seq_len].astype(u.dtype)


def hyena(u, w_in, b_in, short_w, short_b, fw1, fb1, fw_mid, fb_mid, freq, fw_out, fbias, w_out, b_out):
    nb, seq_len, d = u.shape
    proj = dwconv_centred(u @ w_in + b_in, short_w, short_b)
    x1, x2, y = proj[..., :d], proj[..., d:2 * d], proj[..., 2 * d:]
    hid, t = hyena_filter_hidden(seq_len, fw1, fb1, fw_mid, fb_mid, freq)
    window = hyena_decay(t)
    for i, gate in enumerate((x1, x2)):
        h_two = jnp.einsum('lf,fkd->lkd', hid, fw_out[:, i].astype(jnp.float32)) * window[:, None, :]
        y = gate * (long_conv_bidir(y, h_two) + fbias[i] * y)
    return y @ w_out + b_out


def setup_inputs(seed: int = 0) -> dict:
    key = jax.random.key(seed)
    ks = iter(jax.random.split(key, 48))
    f32 = jnp.float32
    D = D_MODEL

    def nrm(shape, scale):
        return jax.random.normal(next(ks), shape, f32) * scale

    def gain(shape):
        return 1.0 + nrm(shape, 0.01)

    x = nrm((BATCH, SEQ, D), 1.0)
    c = nrm((BATCH, D), 1.0)
    ctx = nrm((BATCH, CTX_LEN, D), 1.0)
    c_ctx = nrm((D,), 1.0)
    dt0 = jnp.exp(jax.random.uniform(next(ks), (N_EVEN, SSD_DT), f32, math.log(1e-3), math.log(1e-1)))
    dt_bias = dt0 + jnp.log(-jnp.expm1(-dt0))
    a_log = jnp.log(jax.random.uniform(next(ks), (N_EVEN, 2, SSD_HEADS), f32, 1.0, 16.0))
    return {
        'x': x, 'c': c, 'ctx': ctx, 'c_ctx': c_ctx,
        'mod_w': nrm((DEPTH, D, 6 * D), 0.5 * D ** -0.5),
        'mod_b': nrm((DEPTH, 6 * D), 0.02),
        'norm_mix_w': gain((DEPTH, D)),
        'norm_ffn_w': gain((DEPTH, D)),
        'ffn_w1': nrm((DEPTH, D, D_FF), D ** -0.5),
        'ffn_w3': nrm((DEPTH, D, D_FF), D ** -0.5),
        'ffn_w2': nrm((DEPTH, D_FF, D), D_FF ** -0.5),
        'ev_w_in': nrm((N_EVEN, D, IN_COLS), D ** -0.5),
        'ev_conv_w': nrm((N_EVEN, SSD_CONV, SSD_XBC), SSD_CONV ** -0.5),
        'ev_conv_b': nrm((N_EVEN, SSD_XBC), 0.02),
        'ev_dt_bias': dt_bias,
        'ev_a_log': a_log,
        'ev_d_skip': gain((N_EVEN, 2, SSD_HEADS)),
        'ev_ssd_norm_w': gain((N_EVEN, SSD_INNER)),
        'ev_q_norm_w': gain((N_EVEN, MLA_Q_RANK)),
        'ev_w_uq': nrm((N_EVEN, MLA_Q_RANK, MLA_HEADS * (MLA_NOPE + MLA_ROPE)), MLA_Q_RANK ** -0.5),
        'ev_kv_norm_w': gain((N_EVEN, MLA_KV_RANK)),
        'ev_w_ukv': nrm((N_EVEN, MLA_KV_RANK, MLA_HEADS * (MLA_NOPE + MLA_V)), MLA_KV_RANK ** -0.5),
        'ev_w_o': nrm((N_EVEN, MIX_WIDTH, D), MIX_WIDTH ** -0.5),
        'hy_w_in': nrm((N_ODD, D, 3 * D), D ** -0.5),
        'hy_b_in': nrm((N_ODD, 3 * D), 0.02),
        'hy_short_w': nrm((N_ODD, HY_SHORT, 3 * D), HY_SHORT ** -0.5),
        'hy_short_b': nrm((N_ODD, 3 * D), 0.02),
        'hy_fw1': nrm((N_ODD, HY_EMB, HY_HID), HY_EMB ** -0.5),
        'hy_fb1': nrm((N_ODD, HY_HID), 0.02),
        'hy_fw_mid': nrm((N_ODD, HY_INNER, HY_HID, HY_HID), HY_HID ** -0.5),
        'hy_fb_mid': nrm((N_ODD, HY_INNER, HY_HID), 0.02),
        'hy_freq': gain((N_ODD, HY_HID)),
        'hy_fw_out': nrm((N_ODD, HY_HID, HY_ORDER, 2, D), HY_HID ** -0.5),
        'hy_fbias': nrm((N_ODD, HY_ORDER, D), 0.5),
        'hy_w_out': nrm((N_ODD, D, D), D ** -0.5),
        'hy_b_out': nrm((N_ODD, D), 0.02),
        'final_norm_w': gain((D,)),
    }


def reference(x, c, ctx, c_ctx, mod_w, mod_b, norm_mix_w, norm_ffn_w, ffn_w1, ffn_w3, ffn_w2,
              ev_w_in, ev_conv_w, ev_conv_b, ev_dt_bias, ev_a_log, ev_d_skip, ev_ssd_norm_w,
              ev_q_norm_w, ev_w_uq, ev_kv_norm_w, ev_w_ukv, ev_w_o,
              hy_w_in, hy_b_in, hy_short_w, hy_short_b, hy_fw1, hy_fb1, hy_fw_mid, hy_fb_mid,
              hy_freq, hy_fw_out, hy_fbias, hy_w_out, hy_b_out, final_norm_w):
    seq_len = x.shape[1]
    rows = seq_len // GRID_W
    rope = axial_rope(rows)
    c_act = jax.nn.silu(c)
    cc_act = jax.nn.silu(c_ctx)
    xc = ctx
    for i in range(DEPTH):
        even = i % 2 == 0
        ctx_later = any(j % 2 == 0 for j in range(i + 1, DEPTH))
        mod = (c_act @ mod_w[i] + mod_b[i])[:, None, :]
        sh1, sc1, g1, sh2, sc2, g2 = jnp.split(mod, 6, axis=-1)
        n_lat = modulate(rms_norm(x, norm_mix_w[i]), sh1, sc1)
        if ctx_later:
            csh1, csc1, cg1, csh2, csc2, cg2 = jnp.split(cc_act @ mod_w[i] + mod_b[i], 6)
        elif even:
            csh1, csc1 = jnp.split(cc_act @ mod_w[i][:, :2 * D_MODEL] + mod_b[i][:2 * D_MODEL], 2)
        if even or ctx_later:
            n_ctx = modulate(rms_norm(xc, norm_mix_w[i]), csh1, csc1)
        if even:
            e = i // 2
            o_lat, o_ctx = even_mixer(n_lat, n_ctx, rope, ctx_later, ev_w_in[e], ev_conv_w[e], ev_conv_b[e],
                                      ev_dt_bias[e], ev_a_log[e], ev_d_skip[e], ev_ssd_norm_w[e],
                                      ev_q_norm_w[e], ev_w_uq[e], ev_kv_norm_w[e], ev_w_ukv[e], ev_w_o[e])
        else:
            o = i // 2
            hy_args = (hy_w_in[o], hy_b_in[o], hy_short_w[o], hy_short_b[o], hy_fw1[o], hy_fb1[o],
                       hy_fw_mid[o], hy_fb_mid[o], hy_freq[o], hy_fw_out[o], hy_fbias[o], hy_w_out[o], hy_b_out[o])
            o_lat = hyena(n_lat, *hy_args)
            o_ctx = hyena(n_ctx, *hy_args) if ctx_later else None
        x = x + g1 * o_lat
        x = x + g2 * swiglu(modulate(rms_norm(x, norm_ffn_w[i]), sh2, sc2), ffn_w1[i], ffn_w3[i], ffn_w2[i])
        if ctx_later:
            xc = xc + cg1 * o_ctx
            xc = xc + cg2 * swiglu(modulate(rms_norm(xc, norm_ffn_w[i]), csh2, csc2), ffn_w1[i], ffn_w3[i], ffn_w2[i])
    return rms_norm(x, final_norm_w)
```

```python
import functools
import math

import jax
import jax.numpy as jnp
import numpy as np
from jax import lax
from jax.experimental import pallas as pl
from jax.experimental.pallas import tpu as pltpu

F32 = jnp.float32
BF16 = jnp.bfloat16

EPS = 1e-6
GRID_W = 64
CTX_LEN = 256
SSD_HEADS = 16
SSD_HEAD_DIM = 64
SSD_INNER = 1024
SSD_GROUPS = 2
SSD_HPG = 8
SSD_STATE = 128
SSD_CHUNK = 128
SSD_XBC = 1536
SSD_DT = 32
MLA_HEADS = 8
MLA_NOPE = 128
MLA_ROPE = 64
MLA_V = 128
MLA_Q_RANK = 512
MLA_KV_RANK = 512
ROPE_THETA = 10000.0
MLA_HEAD_PAD = 256
Q_SIDE = SSD_INNER + MLA_Q_RANK
HY_EMB = 33
HY_HID = 64
HY_FAST_DECAY = 0.3
HY_SLOW_DECAY = 1.5
HY_TARGET = 1e-2
FFT_N2 = 128
SLAB_TILE = 8

P_Z, P_CQ, P_XBC, P_CKV, P_KR, P_DT, P_COLS = 0, 1024, 1536, 3072, 3584, 3712, 3840

VMEM_LIMIT = 56 * 1024 * 1024


def _cparams(*sem):
    return pltpu.CompilerParams(dimension_semantics=sem, vmem_limit_bytes=VMEM_LIMIT)


def _pick(n, cands):
    for c in cands:
        if n % c == 0:
            return c
    raise ValueError(f"no tile for {n}")


def _split(a):
    hi = a.astype(BF16)
    lo = (a - hi.astype(F32)).astype(BF16)
    return hi, lo


def _dot(a, b):
    return jnp.dot(a, b, preferred_element_type=F32)


def _dot3(a_hi, a_lo, b_hi, b_lo):
    return _dot(a_hi, b_hi) + (_dot(a_lo, b_hi) + _dot(a_hi, b_lo))


def _silu(x):
    return x * (1.0 / (1.0 + jnp.exp(-x)))


def _modnorm(x, nw, sh, sc):
    ms = jnp.mean(x * x, axis=-1, keepdims=True)
    return (x * lax.rsqrt(ms + EPS) * nw) * (1.0 + sc) + sh


def _matvec_kernel(x_ref, w_ref, b_ref, o_ref):
    x = x_ref[...]
    o_ref[...] = _dot(_silu(x).astype(BF16), w_ref[...].astype(BF16)) + b_ref[...]


def adaln_vectors(vecs, w, b):
    k, n = w.shape
    tn = 1024
    return pl.pallas_call(
        _matvec_kernel,
        grid=(n // tn,),
        in_specs=[pl.BlockSpec((8, k), lambda j: (0, 0)),
                  pl.BlockSpec((k, tn), lambda j: (0, j)),
                  pl.BlockSpec((1, tn), lambda j: (0, j))],
        out_specs=pl.BlockSpec((8, tn), lambda j: (0, j)),
        out_shape=jax.ShapeDtypeStruct((8, n), F32),
        compiler_params=_cparams("arbitrary"),
        name="adaln_vectors",
    )(vecs, w, b.reshape(1, n))


def _normmm_kernel(x_ref, nw_ref, sh_ref, sc_ref, w_ref, b_ref, o_ref, xn_ref):
    @pl.when(pl.program_id(1) == 0)
    def _():
        xn_ref[...] = _modnorm(x_ref[...].astype(F32), nw_ref[...], sh_ref[...], sc_ref[...]).astype(BF16)

    o_ref[...] = (_dot(xn_ref[...], w_ref[...]) + b_ref[...]).astype(o_ref.dtype)


def normmm(x, nw, sh, sc, w, b, *, tm, tn, out_dtype, x_col=0):
    m = x.shape[0]
    k, n = w.shape
    return pl.pallas_call(
        _normmm_kernel,
        grid=(m // tm, n // tn),
        in_specs=[pl.BlockSpec((tm, k), lambda i, j: (i, x_col)),
                  pl.BlockSpec((1, k), lambda i, j: (0, 0)),
                  pl.BlockSpec((1, k), lambda i, j: (0, 0)),
                  pl.BlockSpec((1, k), lambda i, j: (0, 0)),
                  pl.BlockSpec((k, tn), lambda i, j: (0, j)),
                  pl.BlockSpec((1, tn), lambda i, j: (0, j))],
        out_specs=pl.BlockSpec((tm, tn), lambda i, j: (i, j)),
        out_shape=jax.ShapeDtypeStruct((m, n), out_dtype),
        scratch_shapes=[pltpu.VMEM((tm, k), BF16)],
        compiler_params=_cparams("parallel", "arbitrary"),
        name="normmm",
    )(x, nw.reshape(1, k), sh.reshape(1, k), sc.reshape(1, k), w, b.reshape(1, n))


def _qproj_kernel(x_ref, nw_ref, wa_ref, wb_ref, ta_ref, tb_ref, o_ref, xn_ref):
    @pl.when(pl.program_id(1) == 0)
    def _():
        x = x_ref[...]
        ms = jnp.mean(x * x, axis=-1, keepdims=True)
        xn_ref[...] = (x * lax.rsqrt(ms + EPS) * nw_ref[...]).astype(BF16)

    xn = xn_ref[...]
    o_ref[...] = (_dot(xn, wa_ref[...]) * ta_ref[...] + _dot(xn, wb_ref[...]) * tb_ref[...]).astype(o_ref.dtype)


def qproj(p, nw, wa, wb, ta, tb, *, tm):
    m = ta.shape[0]
    k = MLA_Q_RANK
    hp = MLA_HEAD_PAD
    return pl.pallas_call(
        _qproj_kernel,
        grid=(m // tm, MLA_HEADS),
        in_specs=[pl.BlockSpec((tm, k), lambda i, j: (i, P_CQ // MLA_Q_RANK)),
                  pl.BlockSpec((1, k), lambda i, j: (0, 0)),
                  pl.BlockSpec((k, hp), lambda i, j: (0, j)),
                  pl.BlockSpec((k, hp), lambda i, j: (0, j)),
                  pl.BlockSpec((tm, hp), lambda i, j: (i, 0)),
                  pl.BlockSpec((tm, hp), lambda i, j: (i, 0))],
        out_specs=pl.BlockSpec((tm, hp), lambda i, j: (i, j)),
        out_shape=jax.ShapeDtypeStruct((m, MLA_HEADS * hp), BF16),
        scratch_shapes=[pltpu.VMEM((tm, k), BF16)],
        compiler_params=_cparams("parallel", "arbitrary"),
        name="mla_qproj",
    )(p, nw.reshape(1, k), wa, wb, ta, tb)


def _kv_assemble_kernel(kv_ref, kr_ref, tc_ref, ts_ref, k_ref, v_ref, *, rope):
    krr = kr_ref[...]
    kr = krr[:, :MLA_ROPE]
    if rope:
        kr = kr * tc_ref[...][:, :MLA_ROPE] + krr[:, MLA_ROPE:] * ts_ref[...][:, :MLA_ROPE]
    tail = jnp.concatenate([kr, jnp.zeros_like(kr)], axis=-1).astype(BF16)
    for h in range(MLA_HEADS):
        base = h * (MLA_NOPE + MLA_V)
        k_ref[:, h * MLA_HEAD_PAD:h * MLA_HEAD_PAD + MLA_NOPE] = kv_ref[:, base:base + MLA_NOPE]
        k_ref[:, h * MLA_HEAD_PAD + MLA_NOPE:(h + 1) * MLA_HEAD_PAD] = tail
        v_ref[:, h * MLA_V:(h + 1) * MLA_V] = kv_ref[:, base + MLA_NOPE:base + MLA_NOPE + MLA_V]


def kv_assemble(kvp, p, tc, ts, *, tm, rope):
    m = kvp.shape[0]
    return pl.pallas_call(
        functools.partial(_kv_assemble_kernel, rope=rope),
        grid=(m // tm,),
        in_specs=[pl.BlockSpec((tm, MLA_HEADS * (MLA_NOPE + MLA_V)), lambda i: (i, 0)),
                  pl.BlockSpec((tm, 128), lambda i: (i, P_KR // 128)),
                  pl.BlockSpec((tm, 128), lambda i: (i, 0)),
                  pl.BlockSpec((tm, 128), lambda i: (i, 0))],
        out_specs=[pl.BlockSpec((tm, MLA_HEADS * MLA_HEAD_PAD), lambda i: (i, 0)),
                   pl.BlockSpec((tm, MLA_HEADS * MLA_V), lambda i: (i, 0))],
        out_shape=[jax.ShapeDtypeStruct((m, MLA_HEADS * MLA_HEAD_PAD), BF16),
                   jax.ShapeDtypeStruct((m, MLA_HEADS * MLA_V), BF16)],
        compiler_params=_cparams("parallel"),
        name="mla_kv_assemble",
    )(kvp, p, tc, ts)


def _attn_kernel(q_ref, k_ref, v_ref, o_ref, m_ref, l_ref, acc_ref):
    j = pl.program_id(2)

    @pl.when(j == 0)
    def _():
        m_ref[...] = jnp.full(m_ref.shape, -jnp.inf, F32)
        l_ref[...] = jnp.zeros(l_ref.shape, F32)
        acc_ref[...] = jnp.zeros(acc_ref.shape, F32)

    s = lax.dot_general(q_ref[...], k_ref[...], (((1,), (1,)), ((), ())), preferred_element_type=F32)
    m_prev = m_ref[...]
    m_new = jnp.maximum(m_prev, jnp.max(s, axis=-1, keepdims=True))
    alpha = jnp.exp(m_prev - m_new)
    p = jnp.exp(s - m_new)
    l_ref[...] = alpha * l_ref[...] + jnp.sum(p, axis=-1, keepdims=True)
    acc_ref[...] = alpha * acc_ref[...] + _dot(p.astype(BF16), v_ref[...])
    m_ref[...] = m_new

    @pl.when(j == pl.num_programs(2) - 1)
    def _():
        o_ref[...] = (acc_ref[...] / l_ref[...]).astype(o_ref.dtype)


def attention(q, k, v, *, tq, tk):
    lq = q.shape[0]
    s = k.shape[0]
    return pl.pallas_call(
        _attn_kernel,
        grid=(MLA_HEADS, lq // tq, s // tk),
        in_specs=[pl.BlockSpec((tq, MLA_HEAD_PAD), lambda h, i, j: (i, h)),
                  pl.BlockSpec((tk, MLA_HEAD_PAD), lambda h, i, j: (j, h)),
                  pl.BlockSpec((tk, MLA_V), lambda h, i, j: (j, h))],
        out_specs=pl.BlockSpec((tq, MLA_V), lambda h, i, j: (i, h)),
        out_shape=jax.ShapeDtypeStruct((lq, MLA_HEADS * MLA_V), BF16),
        scratch_shapes=[pltpu.VMEM((tq, 1), F32), pltpu.VMEM((tq, 1), F32), pltpu.VMEM((tq, MLA_V), F32)],
        compiler_params=_cparams("parallel", "parallel", "arbitrary"),
        name="mla_attention",
    )(q, k, v)


def _shift_rows(x, prev_row, next_row):
    tm = x.shape[0]
    rows = lax.broadcasted_iota(jnp.int32, x.shape, 0)
    xp = jnp.where(rows == 0, prev_row, pltpu.roll(x, 1, 0))
    xn = jnp.where(rows == tm - 1, next_row, pltpu.roll(x, tm - 1, 0))
    return xp, xn


def _dwconv_kernel(x_ref, xprev_ref, xnext_ref, w_ref, b_ref, o_ref, *, act):
    i = pl.program_id(0)
    x = x_ref[...]
    prev_row = jnp.where(i == 0, 0.0, xprev_ref[7:8, :])
    next_row = jnp.where(i == pl.num_programs(0) - 1, 0.0, xnext_ref[0:1, :])
    xp, xn = _shift_rows(x, prev_row, next_row)
    y = xp * w_ref[0:1, :] + x * w_ref[1:2, :] + xn * w_ref[2:3, :] + b_ref[...]
    if act:
        y = _silu(y)
    o_ref[...] = y


def dwconv3(x, w, b, *, tm, tc, col0, width, act):
    m = x.shape[0]
    cb = col0 // tc
    nrb = m // 8
    tb = tm // 8
    return pl.pallas_call(
        functools.partial(_dwconv_kernel, act=act),
        grid=(m // tm, width // tc),
        in_specs=[pl.BlockSpec((tm, tc), lambda i, j: (i, cb + j)),
                  pl.BlockSpec((8, tc), lambda i, j: (jnp.maximum(i * tb - 1, 0), cb + j)),
                  pl.BlockSpec((8, tc), lambda i, j: (jnp.minimum((i + 1) * tb, nrb - 1), cb + j)),
                  pl.BlockSpec((3, tc), lambda i, j: (0, j)),
                  pl.BlockSpec((1, tc), lambda i, j: (0, j))],
        out_specs=pl.BlockSpec((tm, tc), lambda i, j: (i, j)),
        out_shape=jax.ShapeDtypeStruct((m, width), F32),
        compiler_params=_cparams("arbitrary", "arbitrary"),
        name="dwconv3",
    )(x, x, x, w, b.reshape(1, width))


def _softplus(x):
    return jnp.maximum(x, 0.0) + jnp.log(1.0 + jnp.exp(-jnp.abs(x)))


def _ssd_kernel(xbc_ref, dt_ref, dtt_ref, bias_ref, biast_ref, a_ref, at_ref, tri_ref, trit_ref,
                o_ref, h_ref):
    d = pl.program_id(0)

    @pl.when(pl.program_id(1) == 0)
    def _():
        h_ref[...] = jnp.zeros(h_ref.shape, F32)

    q = SSD_CHUNK
    xbc = xbc_ref[...]
    dt = _softplus(dt_ref[0] + bias_ref[0])
    dtt = _softplus(dtt_ref[0, 0] + biast_ref[0])
    dta_hi, dta_lo = _split(dt * a_ref[0])
    dtat_hi, dtat_lo = _split(dtt * at_ref[0])
    tri = tri_ref[0]
    trit = trit_ref[0]
    acum = _dot(tri, dta_hi) + _dot(tri, dta_lo)
    acumt = _dot(dtat_hi, trit) + _dot(dtat_lo, trit)
    total = jnp.where(d == 0, acum[q - 1:q, :], acum[0:1, :])
    to_end = jnp.exp(total - acum)
    from_start = jnp.exp(acum)
    chunk_decay = jnp.exp(total)
    mask = tri > 0.5

    for g in range(SSD_GROUPS):
        bm = xbc[:, SSD_INNER + g * SSD_STATE:SSD_INNER + (g + 1) * SSD_STATE]
        cm = xbc[:, SSD_INNER + (SSD_GROUPS + g) * SSD_STATE:SSD_INNER + (SSD_GROUPS + g + 1) * SSD_STATE]
        bm_b = bm.astype(BF16)
        cm_b = cm.astype(BF16)
        cb = lax.dot_general(cm_b, bm_b, (((1,), (1,)), ((), ())), preferred_element_type=F32)
        bt_b = bm.T.astype(BF16)
        for r in range(SSD_HPG):
            h = g * SSD_HPG + r
            xs = xbc[:, h * SSD_HEAD_DIM:(h + 1) * SSD_HEAD_DIM]
            xdt = xs * dt[:, h:h + 1]
            seg = acum[:, h:h + 1] - acumt[h:h + 1, :]
            decay = jnp.exp(jnp.where(mask, seg, -jnp.inf))
            y_diag = _dot((cb * decay).astype(BF16), xdt.astype(BF16))
            h_prev = h_ref[h]
            y_off = _dot(cm_b, h_prev.astype(BF16)) * from_start[:, h:h + 1]
            states = _dot(bt_b, (xdt * to_end[:, h:h + 1]).astype(BF16))
            h_ref[h] = h_prev * chunk_decay[:, h:h + 1] + states
            o_ref[0, :, h * SSD_HEAD_DIM:(h + 1) * SSD_HEAD_DIM] = y_diag + y_off


def ssd_scan(xbc, dt2, dt2t, bias2, bias2t, a2, a2t, tri2, *, n_lat_chunks):
    nc = n_lat_chunks
    ncx = CTX_LEN // SSD_CHUNK
    tot = nc + ncx
    q = SSD_CHUNK

    def chunk(d, s):
        return jnp.where(d == 0, lax.rem(s + nc, tot), tot - 1 - s)

    def out_chunk(d, s):
        c = chunk(d, s)
        return jnp.where(c >= nc, jnp.where(d == 0, 0, nc - 1), c)

    return pl.pallas_call(
        _ssd_kernel,
        grid=(2, tot),
        in_specs=[pl.BlockSpec((q, SSD_XBC), lambda d, s: (chunk(d, s), 0)),
                  pl.BlockSpec((1, q, SSD_HEADS), lambda d, s: (d, chunk(d, s), 0)),
                  pl.BlockSpec((1, 1, SSD_HEADS, q), lambda d, s: (d, chunk(d, s), 0, 0)),
                  pl.BlockSpec((1, 1, SSD_HEADS), lambda d, s: (d, 0, 0)),
                  pl.BlockSpec((1, SSD_HEADS, 1), lambda d, s: (d, 0, 0)),
                  pl.BlockSpec((1, 1, SSD_HEADS), lambda d, s: (d, 0, 0)),
                  pl.BlockSpec((1, SSD_HEADS, 1), lambda d, s: (d, 0, 0)),
                  pl.BlockSpec((1, q, q), lambda d, s: (d, 0, 0)),
                  pl.BlockSpec((1, q, q), lambda d, s: (1 - d, 0, 0))],
        out_specs=pl.BlockSpec((1, q, SSD_INNER), lambda d, s: (d, out_chunk(d, s), 0)),
        out_shape=jax.ShapeDtypeStruct((2, nc * q, SSD_INNER), F32),
        scratch_shapes=[pltpu.VMEM((SSD_HEADS, SSD_STATE, SSD_HEAD_DIM), F32)],
        compiler_params=_cparams("arbitrary", "arbitrary"),
        name="ssd_scan",
    )(xbc, dt2, dt2t, bias2, bias2t, a2, a2t, tri2, tri2)


def _merge_kernel(o_ref, yf_ref, yb_ref, xs_ref, z_ref, dsk_ref, nw_ref, w_ref, x_ref, g_ref, out_ref, a_ref):
    @pl.when(pl.program_id(1) == 0)
    def _():
        a_ref[:, :MLA_HEADS * MLA_V] = o_ref[...]
        y = yf_ref[0] + yb_ref[0] + dsk_ref[...] * xs_ref[...]
        gy = y * _silu(z_ref[...])
        gw = SSD_INNER // SSD_GROUPS
        for g in range(SSD_GROUPS):
            part = gy[:, g * gw:(g + 1) * gw]
            ms = jnp.mean(part * part, axis=-1, keepdims=True)
            a_ref[:, MLA_HEADS * MLA_V + g * gw:MLA_HEADS * MLA_V + (g + 1) * gw] = (
                part * lax.rsqrt(ms + EPS) * nw_ref[:, g * gw:(g + 1) * gw]).astype(BF16)

    out_ref[...] = x_ref[...] + g_ref[...] * _dot(a_ref[...], w_ref[...])


def merge_out(o_att, y2, xbc, p, dsk, nw, w_o, x, gate, *, tm, tn):
    m, n = x.shape
    kw = w_o.shape[0]
    return pl.pallas_call(
        _merge_kernel,
        grid=(m // tm, n // tn),
        in_specs=[pl.BlockSpec((tm, MLA_HEADS * MLA_V), lambda i, j: (i, 0)),
                  pl.BlockSpec((1, tm, SSD_INNER), lambda i, j: (0, i, 0)),
                  pl.BlockSpec((1, tm, SSD_INNER), lambda i, j: (1, i, 0)),
                  pl.BlockSpec((tm, SSD_INNER), lambda i, j: (i, 0)),
                  pl.BlockSpec((tm, SSD_INNER), lambda i, j: (i, 0)),
                  pl.BlockSpec((1, SSD_INNER), lambda i, j: (0, 0)),
                  pl.BlockSpec((1, SSD_INNER), lambda i, j: (0, 0)),
                  pl.BlockSpec((kw, tn), lambda i, j: (0, j)),
                  pl.BlockSpec((tm, tn), lambda i, j: (i, j)),
                  pl.BlockSpec((1, tn), lambda i, j: (0, j))],
        out_specs=pl.BlockSpec((tm, tn), lambda i, j: (i, j)),
        out_shape=jax.ShapeDtypeStruct((m, n), F32),
        scratch_shapes=[pltpu.VMEM((tm, kw), BF16)],
        compiler_params=_cparams("parallel", "arbitrary"),
        name="merge_out",
    )(o_att, y2, y2, xbc, p, dsk, nw, w_o, x, gate)


def _mmres_kernel(a_ref, w_ref, b_ref, x_ref, g_ref, o_ref):
    o_ref[...] = x_ref[...] + g_ref[...] * (_dot(a_ref[...].astype(BF16), w_ref[...]) + b_ref[...])


def mm_res(a, w, b, x, gate, *, tm, tn):
    m, k = a.shape
    n = w.shape[1]
    return pl.pallas_call(
        _mmres_kernel,
        grid=(m // tm, n // tn),
        in_specs=[pl.BlockSpec((tm, k), lambda i, j: (i, 0)),
                  pl.BlockSpec((k, tn), lambda i, j: (0, j)),
                  pl.BlockSpec((1, tn), lambda i, j: (0, j)),
                  pl.BlockSpec((tm, tn), lambda i, j: (i, j)),
                  pl.BlockSpec((1, tn), lambda i, j: (0, j))],
        out_specs=pl.BlockSpec((tm, tn), lambda i, j: (i, j)),
        out_shape=jax.ShapeDtypeStruct((m, n), F32),
        compiler_params=_cparams("parallel", "arbitrary"),
        name="mm_res",
    )(a, w, b.reshape(1, n), x, gate)


def _ffn_kernel(x_ref, nw_ref, sh_ref, sc_ref, g_ref, w1_ref, w3_ref, w2_ref, fw_ref, o_ref, xn_ref, acc_ref,
                *, final_norm):
    f = pl.program_id(1)

    @pl.when(f == 0)
    def _():
        xn_ref[...] = _modnorm(x_ref[...], nw_ref[...], sh_ref[...], sc_ref[...]).astype(BF16)
        acc_ref[...] = jnp.zeros(acc_ref.shape, F32)

    xn = xn_ref[...]
    a = _dot(xn, w1_ref[...])
    b = _dot(xn, w3_ref[...])
    acc_ref[...] += _dot((_silu(a) * b).astype(BF16), w2_ref[...])

    @pl.when(f == pl.num_programs(1) - 1)
    def _():
        y = x_ref[...] + g_ref[...] * acc_ref[...]
        if final_norm:
            ms = jnp.mean(y * y, axis=-1, keepdims=True)
            y = y * lax.rsqrt(ms + EPS) * fw_ref[...]
        o_ref[...] = y


def ffn(x, nw, sh, sc, gate, w1, w3, w2, fw, *, tm, tf, final_norm):
    m, dm = x.shape
    dff = w1.shape[1]
    vec = pl.BlockSpec((1, dm), lambda i, f: (0, 0))
    return pl.pallas_call(
        functools.partial(_ffn_kernel, final_norm=final_norm),
        grid=(m // tm, dff // tf),
        in_specs=[pl.BlockSpec((tm, dm), lambda i, f: (i, 0)), vec, vec, vec, vec,
                  pl.BlockSpec((dm, tf), lambda i, f: (0, f)),
                  pl.BlockSpec((dm, tf), lambda i, f: (0, f)),
                  pl.BlockSpec((tf, dm), lambda i, f: (f, 0)),
                  vec],
        out_specs=pl.BlockSpec((tm, dm), lambda i, f: (i, 0)),
        out_shape=jax.ShapeDtypeStruct((m, dm), F32),
        scratch_shapes=[pltpu.VMEM((tm, dm), BF16), pltpu.VMEM((tm, dm), F32)],
        compiler_params=_cparams("parallel", "arbitrary"),
        name="ffn",
    )(x, nw.reshape(1, dm), sh, sc, gate, w1, w3, w2, fw.reshape(1, dm))


def _filter_kernel(fvec_ref, w1h_ref, w1l_ref, b1_ref, wmh_ref, wml_ref, bm_ref, freq_ref, woh_ref, wol_ref,
                   delta_ref, k_ref, abs_ref, *, seq_len, tr):
    i = pl.program_id(0)
    n = i * tr + lax.broadcasted_iota(jnp.int32, (tr, 1), 0)
    lag = jnp.where(n < seq_len, n, 2 * seq_len - n).astype(F32)
    t = lag / float(seq_len - 1)
    ang = lag * (2.0 * math.pi / seq_len)
    lane = lax.broadcasted_iota(jnp.int32, (tr, 128), 1)
    bands = (HY_EMB - 1) // 2
    arg = ang * fvec_ref[...]
    emb = jnp.where(lane == 0, t, jnp.where(lane <= bands, jnp.cos(arg), jnp.where(lane <= 2 * bands, -jnp.sin(arg), 0.0)))
    fr = freq_ref[...]
    e_hi, e_lo = _split(emb)
    hid = jnp.sin(fr * (_dot3(e_hi, e_lo, w1h_ref[...], w1l_ref[...]) + b1_ref[...]))
    for j in range(wmh_ref.shape[0]):
        h_hi, h_lo = _split(hid)
        hid = jnp.sin(fr * (_dot3(h_hi, h_lo, wmh_ref[j], wml_ref[j]) + bm_ref[j]))
    h_hi, h_lo = _split(hid)
    k = _dot3(h_hi, h_lo, woh_ref[0], wol_ref[0]) * jnp.exp(-t * delta_ref[...])
    k = jnp.where(n == seq_len, 0.0, k)
    k_ref[...] = k

    @pl.when(i == 0)
    def _():
        abs_ref[...] = jnp.zeros(abs_ref.shape, F32)

    abs_ref[...] += jnp.sum(jnp.abs(k), axis=0, keepdims=True)


def hyena_filter(fvec, w1h, w1l, b1, wmh, wml, bm, freq, woh, wol, delta, *, seq_len, tr):
    d = delta.shape[1]
    half = seq_len // tr
    full = lambda shp: pl.BlockSpec(shp, lambda i: (0,) * len(shp))
    return pl.pallas_call(
        functools.partial(_filter_kernel, seq_len=seq_len, tr=tr),
        grid=(2 * seq_len // tr,),
        in_specs=[full((1, 128)), full(w1h.shape), full(w1l.shape), full(b1.shape), full(wmh.shape),
                  full(wml.shape), full(bm.shape), full(freq.shape),
                  pl.BlockSpec((1, HY_HID, d), lambda i: (i // half, 0, 0)),
                  pl.BlockSpec((1, HY_HID, d), lambda i: (i // half, 0, 0)),
                  full((1, d))],
        out_specs=[pl.BlockSpec((tr, d), lambda i: (i, 0)), pl.BlockSpec((1, d), lambda i: (0, 0))],
        out_shape=[jax.ShapeDtypeStruct((2 * seq_len, d), F32), jax.ShapeDtypeStruct((1, d), F32)],
        compiler_params=_cparams("arbitrary"),
        name="hyena_filter",
    )(fvec, w1h, w1l, b1, wmh, wml, bm, freq, woh, wol, delta)


def _fftmm_kernel(fh_ref, fl_ref, x_ref, o_ref):
    x_hi, x_lo = _split(x_ref[...])
    o_ref[...] = _dot3(fh_ref[...], fl_ref[...], x_hi, x_lo)


def _fftmm_gate_kernel(fh_ref, fl_ref, x_ref, v_ref, gate_ref, fb_ref, o_ref):
    x_hi, x_lo = _split(x_ref[...])
    conv = _dot3(fh_ref[...], fl_ref[...], x_hi, x_lo)
    v = v_ref[...]
    o_ref[...] = gate_ref[...] * (conv + fb_ref[...] * v)


def fft_stage1(fh, fl, x2, *, tn, gate_args=None):
    mrows, k = fh.shape
    cols = x2.shape[1]
    fspec = pl.BlockSpec((mrows, k), lambda j: (0, 0))
    if gate_args is None:
        return pl.pallas_call(
            _fftmm_kernel,
            grid=(cols // tn,),
            in_specs=[fspec, fspec, pl.BlockSpec((k, tn), lambda j: (0, j))],
            out_specs=pl.BlockSpec((mrows, tn), lambda j: (0, j)),
            out_shape=jax.ShapeDtypeStruct((mrows, cols), F32),
            compiler_params=_cparams("parallel"),
            name="fft_stage1",
        )(fh, fl, x2)
    v2, vmul, vofs, gate2, gmul, gofs, fb = gate_args
    return pl.pallas_call(
        _fftmm_gate_kernel,
        grid=(cols // tn,),
        in_specs=[fspec, fspec, pl.BlockSpec((k, tn), lambda j: (0, j)),
                  pl.BlockSpec((mrows, tn), lambda j: (0, j * vmul + vofs)),
                  pl.BlockSpec((mrows, tn), lambda j: (0, j * gmul + gofs)),
                  pl.BlockSpec((1, tn), lambda j: (0, 0))],
        out_specs=pl.BlockSpec((mrows, tn), lambda j: (0, j)),
        out_shape=jax.ShapeDtypeStruct((mrows, cols), F32),
        compiler_params=_cparams("parallel"),
        name="fft_stage1_inv",
    )(fh, fl, x2, v2, gate2, fb)


def _fftmid_kernel(a_ref, mfh_ref, mfl_ref, kf_ref, mih_ref, mil_ref, o_ref):
    half = FFT_N2
    for s in range(SLAB_TILE):
        a_hi, a_lo = _split(a_ref[s])
        y = _dot3(mfh_ref[s], mfl_ref[s], a_hi, a_lo)
        kf = kf_ref[s]
        yr, yi = y[:half], y[half:]
        kr, ki = kf[:half], kf[half:]
        pr = yr * kr - yi * ki
        pi = yr * ki + yi * kr
        p_hi, p_lo = _split(jnp.concatenate([pr, pi], axis=0))
        o_ref[s] = _dot3(mih_ref[s], mil_ref[s], p_hi, p_lo)


def fft_mid(a3, mfh, mfl, kf3, mih, mil, *, dc):
    nsp, rows, d = a3.shape
    blk = pl.BlockSpec((SLAB_TILE, rows, dc), lambda s, j: (s, 0, j))
    mat = pl.BlockSpec((SLAB_TILE, rows, rows), lambda s, j: (s, 0, 0))
    return pl.pallas_call(
        _fftmid_kernel,
        grid=(nsp // SLAB_TILE, d // dc),
        in_specs=[blk, mat, mat, blk, mat, mat],
        out_specs=blk,
        out_shape=jax.ShapeDtypeStruct((nsp, rows, d), F32),
        compiler_params=_cparams("parallel", "arbitrary"),
        name="fft_mid",
    )(a3, mfh, mfl, kf3, mih, mil)


def _fftspec_kernel(a_ref, mfh_ref, mfl_ref, scale_ref, o_ref):
    for s in range(SLAB_TILE):
        a_hi, a_lo = _split(a_ref[s])
        o_ref[s] = _dot3(mfh_ref[s], mfl_ref[s], a_hi, a_lo) * scale_ref[...]


def fft_spectrum(a3, mfh, mfl, scale, *, dc):
    nsp, rows, d = a3.shape
    blk = pl.BlockSpec((SLAB_TILE, rows, dc), lambda s, j: (s, 0, j))
    mat = pl.BlockSpec((SLAB_TILE, rows, rows), lambda s, j: (s, 0, 0))
    return pl.pallas_call(
        _fftspec_kernel,
        grid=(nsp // SLAB_TILE, d // dc),
        in_specs=[blk, mat, mat, pl.BlockSpec((1, dc), lambda s, j: (0, j))],
        out_specs=blk,
        out_shape=jax.ShapeDtypeStruct((nsp, rows, d), F32),
        compiler_params=_cparams("parallel", "arbitrary"),
        name="fft_spectrum",
    )(a3, mfh, mfl, scale)


def _fft_tables(seq_len):
    n = 2 * seq_len
    n2 = FFT_N2
    n1 = n // n2
    ns = n1 // 2 + 1
    nsp = -(-ns // SLAB_TILE) * SLAB_TILE
    k1 = jnp.arange(nsp, dtype=jnp.int32)
    valid = (k1 < ns)
    m1 = jnp.arange(n1, dtype=jnp.int32)
    ang1 = (2.0 * math.pi / n1) * lax.rem(k1[:, None] * m1[None, :], n1).astype(F32)
    vf = valid[:, None].astype(F32)
    f1 = jnp.stack([jnp.cos(ang1) * vf, -jnp.sin(ang1) * vf], axis=1).reshape(2 * nsp, n1)
    wgt = jnp.where((k1 == 0) | (k1 == n1 // 2), 1.0, 2.0) * valid.astype(F32) / n
    g1 = jnp.stack([jnp.cos(ang1) * wgt[:, None], -jnp.sin(ang1) * wgt[:, None]], axis=1)
    g1 = g1.reshape(2 * nsp, n1).T[: n1 // 2]
    k2 = jnp.arange(n2, dtype=jnp.int32)
    m2 = jnp.arange(n2, dtype=jnp.int32)
    f = k1[:, None, None] + n1 * k2[None, :, None]
    ang2 = (2.0 * math.pi / n) * lax.rem(f * m2[None, None, :], n).astype(F32)
    vm = valid[:, None, None].astype(F32)
    c2, s2 = jnp.cos(ang2) * vm, jnp.sin(ang2) * vm
    mf = jnp.concatenate([jnp.concatenate([c2, s2], axis=2), jnp.concatenate([-s2, c2], axis=2)], axis=1)
    c2t, s2t = jnp.swapaxes(c2, 1, 2), jnp.swapaxes(s2, 1, 2)
    mi = jnp.concatenate([jnp.concatenate([c2t, -s2t], axis=2), jnp.concatenate([s2t, c2t], axis=2)], axis=1)
    return dict(n1=n1, nsp=nsp, f1=_split(f1), g1=_split(g1), mf=_split(mf), mi=_split(mi))


def long_conv_gate(v, v_cols, gate, gate_cols, fbias, kf3, tabs, *, seq_len, d):
    n1, nsp = tabs["n1"], tabs["nsp"]
    half = n1 // 2
    vw = v.shape[1] // d
    v2 = v.reshape(half, FFT_N2 * v.shape[1])
    g2 = gate.reshape(half, FFT_N2 * gate.shape[1])
    gw = gate.shape[1] // d
    if vw == 1:
        a = fft_stage1(tabs["f1"][0][:, :half], tabs["f1"][1][:, :half], v2, tn=d)
    else:
        a = fft_stage1_cols(tabs["f1"][0][:, :half], tabs["f1"][1][:, :half], v2, tn=d, mul=vw, ofs=v_cols)
    a3 = a.reshape(nsp, 2 * FFT_N2, d)
    b3 = fft_mid(a3, tabs["mf"][0], tabs["mf"][1], kf3, tabs["mi"][0], tabs["mi"][1], dc=256)
    b2 = b3.reshape(2 * nsp, FFT_N2 * d)
    out = fft_stage1(tabs["g1"][0], tabs["g1"][1], b2, tn=d,
                     gate_args=(v2, vw, v_cols, g2, gw, gate_cols, fbias))
    return out.reshape(seq_len, d)


def fft_stage1_cols(fh, fl, x2, *, tn, mul, ofs):
    mrows, k = fh.shape
    cols = x2.shape[1] // mul
    fspec = pl.BlockSpec((mrows, k), lambda j: (0, 0))
    return pl.pallas_call(
        _fftmm_kernel,
        grid=(cols // tn,),
        in_specs=[fspec, fspec, pl.BlockSpec((k, tn), lambda j: (0, j * mul + ofs))],
        out_specs=pl.BlockSpec((mrows, tn), lambda j: (0, j)),
        out_shape=jax.ShapeDtypeStruct((mrows, cols), F32),
        compiler_params=_cparams("parallel"),
        name="fft_stage1",
    )(fh, fl, x2)


def filter_spectrum(k, kabs, tabs, *, d):
    n1, nsp = tabs["n1"], tabs["nsp"]
    k2 = k.reshape(n1, FFT_N2 * d)
    a = fft_stage1(tabs["f1"][0], tabs["f1"][1], k2, tn=d)
    return fft_spectrum(a.reshape(nsp, 2 * FFT_N2, d), tabs["mf"][0], tabs["mf"][1], 1.0 / kabs, dc=256)


def _rope_tables(seq_len, scale):
    n_freq = MLA_ROPE // 4
    rows = seq_len // GRID_W
    row = jnp.repeat(jnp.arange(rows, dtype=F32), GRID_W)
    col = jnp.tile(jnp.arange(GRID_W, dtype=F32), rows)
    inv = ROPE_THETA ** (-jnp.arange(n_freq, dtype=F32) / n_freq)
    ang = jnp.stack([row[:, None] * inv, col[:, None] * inv], axis=1)
    cos = jnp.broadcast_to(jnp.cos(ang)[:, :, None, :], (seq_len, 2, 2, n_freq)).reshape(seq_len, MLA_ROPE)
    sin = jnp.broadcast_to(jnp.sin(ang)[:, :, None, :], (seq_len, 2, 2, n_freq)).reshape(seq_len, MLA_ROPE)
    one = jnp.ones((seq_len, MLA_NOPE), F32)
    zero = jnp.zeros((seq_len, MLA_NOPE), F32)
    z64 = jnp.zeros((seq_len, MLA_HEAD_PAD - MLA_NOPE - MLA_ROPE), F32)
    ta = jnp.concatenate([one, cos, z64], axis=1) * scale
    tb = jnp.concatenate([zero, sin, z64], axis=1) * scale
    tc = jnp.concatenate([cos, z64], axis=1)
    ts = jnp.concatenate([sin, z64], axis=1)
    return ta, tb, tc, ts


def _rope_swap_cols(w):
    n_freq = MLA_ROPE // 4
    w4 = w.reshape(w.shape[0], 2, 2, n_freq)
    return jnp.stack([-w4[:, :, 1], w4[:, :, 0]], axis=2).reshape(w.shape[0], MLA_ROPE)


def _even_layer(x, ctx, mods, norm_mix_w, w_in, conv_w, conv_b, dt_bias, a_log, d_skip, ssd_norm_w,
                q_norm_w, w_uq, kv_norm_w, w_ukv, w_o):
    seq_len, d = x.shape
    sh1, sc1, g1 = (mods[0:1, i * d:(i + 1) * d] for i in range(3))
    csh1, csc1 = mods[1:2, 0:d], mods[1:2, d:2 * d]

    o1 = Q_SIDE + SSD_XBC
    o2 = o1 + SSD_DT
    o3 = o2 + MLA_KV_RANK
    w_kr = w_in[:, o3:]
    w_ext = jnp.concatenate([w_in[:, :Q_SIDE], w_in[:, Q_SIDE:o1], w_in[:, o2:o3], w_kr, _rope_swap_cols(w_kr),
                             w_in[:, o1:o2], jnp.zeros((d, P_COLS - P_DT - SSD_DT), F32)], axis=1).astype(BF16)
    zb = jnp.zeros((P_COLS,), F32)
    p_lat = normmm(x, norm_mix_w, sh1, sc1, w_ext, zb, tm=512, tn=768, out_dtype=F32)
    p_ctx = normmm(ctx, norm_mix_w, csh1, csc1, w_ext, zb, tm=CTX_LEN, tn=768, out_dtype=F32)

    scale = float(MLA_NOPE + MLA_ROPE) ** -0.5
    ta, tb, tc, ts = _rope_tables(seq_len, scale)
    wq = w_uq.reshape(MLA_Q_RANK, MLA_HEADS, MLA_NOPE + MLA_ROPE)
    zpad = jnp.zeros((MLA_Q_RANK, MLA_HEADS, MLA_HEAD_PAD - MLA_NOPE - MLA_ROPE), F32)
    wa = jnp.concatenate([wq, zpad], axis=2).reshape(MLA_Q_RANK, -1).astype(BF16)
    wr = wq[:, :, MLA_NOPE:]
    n_freq = MLA_ROPE // 4
    wr4 = wr.reshape(MLA_Q_RANK, MLA_HEADS, 2, 2, n_freq)
    wsw = jnp.stack([-wr4[:, :, :, 1], wr4[:, :, :, 0]], axis=3).reshape(MLA_Q_RANK, MLA_HEADS, MLA_ROPE)
    wb = jnp.concatenate([jnp.zeros((MLA_Q_RANK, MLA_HEADS, MLA_NOPE), F32), wsw, zpad], axis=2)
    wb = wb.reshape(MLA_Q_RANK, -1).astype(BF16)
    q = qproj(p_lat, q_norm_w, wa, wb, ta, tb, tm=512)

    zk = jnp.zeros((MLA_KV_RANK,), F32)
    w_ukv_b = w_ukv.astype(BF16)
    zkb = jnp.zeros((w_ukv.shape[1],), F32)
    kv_lat = normmm(p_lat, kv_norm_w, zk, zk, w_ukv_b, zkb, tm=512, tn=512, out_dtype=BF16,
                    x_col=P_CKV // MLA_KV_RANK)
    kv_ctx = normmm(p_ctx, kv_norm_w, zk, zk, w_ukv_b, zkb, tm=CTX_LEN, tn=512, out_dtype=BF16,
                    x_col=P_CKV // MLA_KV_RANK)
    k_lat, v_lat = kv_assemble(kv_lat, p_lat, tc, ts, tm=512, rope=True)
    k_ctx, v_ctx = kv_assemble(kv_ctx, p_ctx, tc, ts, tm=CTX_LEN, rope=False)
    k_all = jnp.concatenate([k_lat, k_ctx], axis=0)
    v_all = jnp.concatenate([v_lat, v_ctx], axis=0)
    s_tot = seq_len + CTX_LEN
    o_att = attention(q, k_all, v_all, tq=_pick(seq_len, (1024, 512, 256)),
                      tk=_pick(s_tot, (1280, 1024, 768, 640, 512, 384, 256, 128)))

    xbc_lat = dwconv3(p_lat, conv_w, conv_b, tm=512, tc=SSD_XBC, col0=P_XBC, width=SSD_XBC, act=True)
    xbc_ctx = dwconv3(p_ctx, conv_w, conv_b, tm=CTX_LEN, tc=SSD_XBC, col0=P_XBC, width=SSD_XBC, act=True)
    xbc_all = jnp.concatenate([xbc_lat, xbc_ctx], axis=0)
    dt_all = jnp.concatenate([p_lat[:, P_DT:P_DT + SSD_DT], p_ctx[:, P_DT:P_DT + SSD_DT]], axis=0)
    nch = s_tot // SSD_CHUNK
    dt2 = dt_all.reshape(s_tot, 2, SSD_HEADS).transpose(1, 0, 2)
    dt2t = dt2.reshape(2, nch, SSD_CHUNK, SSD_HEADS).transpose(0, 1, 3, 2)
    bias2 = dt_bias.reshape(2, 1, SSD_HEADS)
    bias2t = dt_bias.reshape(2, SSD_HEADS, 1)
    a_neg = -jnp.exp(a_log.astype(F32))
    a2 = a_neg.reshape(2, 1, SSD_HEADS)
    a2t = a_neg.reshape(2, SSD_HEADS, 1)
    lower = jnp.tril(jnp.ones((SSD_CHUNK, SSD_CHUNK), F32))
    tri2 = jnp.stack([lower, lower.T]).astype(BF16)
    y2 = ssd_scan(xbc_all, dt2, dt2t, bias2, bias2t, a2, a2t, tri2, n_lat_chunks=seq_len // SSD_CHUNK)

    dsk = jnp.repeat(d_skip[0] + d_skip[1], SSD_HEAD_DIM).reshape(1, SSD_INNER)
    return merge_out(o_att, y2, xbc_lat, p_lat, dsk, ssd_norm_w.reshape(1, SSD_INNER), w_o.astype(BF16), x, g1,
                     tm=512, tn=512)


def _odd_layer(x, mods, norm_mix_w, w_in, b_in, short_w, short_b, fw1, fb1, fw_mid, fb_mid, freq, fw_out,
               fbias, w_out, b_out):
    seq_len, d = x.shape
    sh1, sc1, g1 = (mods[0:1, i * d:(i + 1) * d] for i in range(3))
    proj = normmm(x, norm_mix_w, sh1, sc1, w_in.astype(BF16), b_in, tm=512, tn=768, out_dtype=F32)
    pc = dwconv3(proj, short_w, short_b, tm=512, tc=1536, col0=0, width=3 * d, act=False)

    tabs = _fft_tables(seq_len)
    bands = (HY_EMB - 1) // 2
    fband = jnp.linspace(1e-4, bands - 1, bands, dtype=F32)
    fvec = jnp.concatenate([jnp.zeros((1,), F32), fband, fband, jnp.zeros((128 - HY_EMB,), F32)]).reshape(1, 128)
    w1p = jnp.concatenate([fw1.astype(F32), jnp.zeros((128 - HY_EMB, HY_HID), F32)], axis=0)
    w1h, w1l = _split(w1p)
    wmh, wml = _split(fw_mid.astype(F32))
    lo = math.log(HY_SLOW_DECAY) / HY_TARGET
    hi = math.log(HY_FAST_DECAY) / HY_TARGET
    delta = jnp.abs(jnp.linspace(lo, hi, d, dtype=F32)).reshape(1, d)
    y_cols = 2
    y = None
    for i in range(2):
        wo = jnp.transpose(fw_out[:, i].astype(F32), (1, 0, 2))
        woh, wol = _split(wo)
        k, kabs = hyena_filter(fvec, w1h, w1l, fb1.reshape(1, HY_HID), wmh, wml,
                               fb_mid.reshape(-1, 1, HY_HID), freq.reshape(1, HY_HID), woh, wol, delta,
                               seq_len=seq_len, tr=256)
        kf3 = filter_spectrum(k, kabs, tabs, d=d)
        if y is None:
            y = long_conv_gate(pc, y_cols, pc, i, fbias[i].reshape(1, d), kf3, tabs, seq_len=seq_len, d=d)
        else:
            y = long_conv_gate(y, 0, pc, i, fbias[i].reshape(1, d), kf3, tabs, seq_len=seq_len, d=d)
    y = y.reshape(seq_len, d)
    return mm_res(y, w_out.astype(BF16), b_out, x, g1, tm=512, tn=512)


def kernel(x, c, ctx, c_ctx, mod_w, mod_b, norm_mix_w, norm_ffn_w, ffn_w1, ffn_w3, ffn_w2, ev_w_in, ev_conv_w, ev_conv_b, ev_dt_bias, ev_a_log, ev_d_skip, ev_ssd_norm_w, ev_q_norm_w, ev_w_uq, ev_kv_norm_w, ev_w_ukv, ev_w_o, hy_w_in, hy_b_in, hy_short_w, hy_short_b, hy_fw1, hy_fb1, hy_fw_mid, hy_fb_mid, hy_freq, hy_fw_out, hy_fbias, hy_w_out, hy_b_out, final_norm_w):
    assert x.shape[0] == 1 and mod_w.shape[0] == 2
    xs = x[0]
    xc = ctx[0]
    d = xs.shape[1]
    vecs = jnp.concatenate([c.reshape(1, d), c_ctx.reshape(1, d), jnp.zeros((6, d), F32)], axis=0)
    depth = mod_w.shape[0]
    for i in range(depth):
        mods = adaln_vectors(vecs, mod_w[i], mod_b[i])
        sh2, sc2, g2 = (mods[0:1, j * d:(j + 1) * d] for j in range(3, 6))
        if i % 2 == 0:
            e = i // 2
            xs = _even_layer(xs, xc, mods, norm_mix_w[i], ev_w_in[e], ev_conv_w[e], ev_conv_b[e], ev_dt_bias[e],
                             ev_a_log[e], ev_d_skip[e], ev_ssd_norm_w[e], ev_q_norm_w[e], ev_w_uq[e],
                             ev_kv_norm_w[e], ev_w_ukv[e], ev_w_o[e])
        else:
            o = i // 2
            xs = _odd_layer(xs, mods, norm_mix_w[i], hy_w_in[o], hy_b_in[o], hy_short_w[o], hy_short_b[o],
                            hy_fw1[o], hy_fb1[o], hy_fw_mid[o], hy_fb_mid[o], hy_freq[o], hy_fw_out[o],
                            hy_fbias[o], hy_w_out[o], hy_b_out[o])
        xs = ffn(xs, norm_ffn_w[i], sh2, sc2, g2, ffn_w1[i].astype(BF16), ffn_w3[i].astype(BF16),
                 ffn_w2[i].astype(BF16), final_norm_w, tm=512, tf=512, final_norm=(i == depth - 1))
    return xs[None]
```

```python
import functools
import math

import jax
import jax.numpy as jnp
import numpy as np
from jax import lax
from jax.experimental import pallas as pl
from jax.experimental.pallas import tpu as pltpu

F32 = jnp.float32
BF16 = jnp.bfloat16

EPS = 1e-6
GRID_W = 64
CTX_LEN = 256
SSD_HEADS = 16
SSD_HEAD_DIM = 64
SSD_INNER = 1024
SSD_GROUPS = 2
SSD_HPG = 8
SSD_STATE = 128
SSD_CHUNK = 128
SSD_XBC = 1536
SSD_DT = 32
MLA_HEADS = 8
MLA_NOPE = 128
MLA_ROPE = 64
MLA_V = 128
MLA_Q_RANK = 512
MLA_KV_RANK = 512
ROPE_THETA = 10000.0
MLA_HEAD_PAD = 256
Q_SIDE = SSD_INNER + MLA_Q_RANK
HY_EMB = 33
HY_HID = 64
HY_FAST_DECAY = 0.3
HY_SLOW_DECAY = 1.5
HY_TARGET = 1e-2
FFT_N2 = 128
SLAB_TILE = 8

P_Z, P_CQ, P_XBC, P_CKV, P_KR, P_DT, P_COLS = 0, 1024, 1536, 3072, 3584, 3712, 3840

VMEM_LIMIT = 56 * 1024 * 1024


def _cparams(*sem):
    return pltpu.CompilerParams(dimension_semantics=sem, vmem_limit_bytes=VMEM_LIMIT)


def _pick(n, cands):
    for c in cands:
        if n % c == 0:
            return c
    raise ValueError(f"no tile for {n}")


def _split(a):
    hi = a.astype(BF16)
    lo = (a - hi.astype(F32)).astype(BF16)
    return hi, lo


def _dot(a, b):
    return jnp.dot(a, b, preferred_element_type=F32)


def _dot3(a_hi, a_lo, b_hi, b_lo):
    return _dot(a_hi, b_hi) + (_dot(a_lo, b_hi) + _dot(a_hi, b_lo))


def _silu(x):
    return x * (1.0 / (1.0 + jnp.exp(-x)))


def _modnorm(x, nw, sh, sc):
    ms = jnp.mean(x * x, axis=-1, keepdims=True)
    return (x * lax.rsqrt(ms + EPS) * nw) * (1.0 + sc) + sh


def _matvec_kernel(x_ref, w_ref, b_ref, o_ref):
    x = x_ref[...]
    o_ref[...] = _dot(_silu(x).astype(BF16), w_ref[...].astype(BF16)) + b_ref[...]


def adaln_vectors(vecs, w, b):
    k, n = w.shape
    tn = 1024
    return pl.pallas_call(
        _matvec_kernel,
        grid=(n // tn,),
        in_specs=[pl.BlockSpec((8, k), lambda j: (0, 0)),
                  pl.BlockSpec((k, tn), lambda j: (0, j)),
                  pl.BlockSpec((1, tn), lambda j: (0, j))],
        out_specs=pl.BlockSpec((8, tn), lambda j: (0, j)),
        out_shape=jax.ShapeDtypeStruct((8, n), F32),
        compiler_params=_cparams("arbitrary"),
        name="adaln_vectors",
    )(vecs, w, b.reshape(1, n))


def _normmm_kernel(x_ref, nw_ref, sh_ref, sc_ref, w_ref, b_ref, o_ref, xn_ref):
    @pl.when(pl.program_id(1) == 0)
    def _():
        xn_ref[...] = _modnorm(x_ref[...].astype(F32), nw_ref[...], sh_ref[...], sc_ref[...]).astype(BF16)

    o_ref[...] = (_dot(xn_ref[...], w_ref[...]) + b_ref[...]).astype(o_ref.dtype)


def normmm(x, nw, sh, sc, w, b, *, tm, tn, out_dtype, x_col=0):
    m = x.shape[0]
    k, n = w.shape
    return pl.pallas_call(
        _normmm_kernel,
        grid=(m // tm, n // tn),
        in_specs=[pl.BlockSpec((tm, k), lambda i, j: (i, x_col)),
                  pl.BlockSpec((1, k), lambda i, j: (0, 0)),
                  pl.BlockSpec((1, k), lambda i, j: (0, 0)),
                  pl.BlockSpec((1, k), lambda i, j: (0, 0)),
                  pl.BlockSpec((k, tn), lambda i, j: (0, j)),
                  pl.BlockSpec((1, tn), lambda i, j: (0, j))],
        out_specs=pl.BlockSpec((tm, tn), lambda i, j: (i, j)),
        out_shape=jax.ShapeDtypeStruct((m, n), out_dtype),
        scratch_shapes=[pltpu.VMEM((tm, k), BF16)],
        compiler_params=_cparams("parallel", "arbitrary"),
        name="normmm",
    )(x, nw.reshape(1, k), sh.reshape(1, k), sc.reshape(1, k), w, b.reshape(1, n))


def _qproj_kernel(x_ref, nw_ref, wa_ref, wb_ref, ta_ref, tb_ref, o_ref, xn_ref):
    @pl.when(pl.program_id(1) == 0)
    def _():
        x = x_ref[...]
        ms = jnp.mean(x * x, axis=-1, keepdims=True)
        xn_ref[...] = (x * lax.rsqrt(ms + EPS) * nw_ref[...]).astype(BF16)

    xn = xn_ref[...]
    o_ref[...] = (_dot(xn, wa_ref[...]) * ta_ref[...] + _dot(xn, wb_ref[...]) * tb_ref[...]).astype(o_ref.dtype)


def qproj(p, nw, wa, wb, ta, tb, *, tm):
    m = ta.shape[0]
    k = MLA_Q_RANK
    hp = MLA_HEAD_PAD
    return pl.pallas_call(
        _qproj_kernel,
        grid=(m // tm, MLA_HEADS),
        in_specs=[pl.BlockSpec((tm, k), lambda i, j: (i, P_CQ // MLA_Q_RANK)),
                  pl.BlockSpec((1, k), lambda i, j: (0, 0)),
                  pl.BlockSpec((k, hp), lambda i, j: (0, j)),
                  pl.BlockSpec((k, hp), lambda i, j: (0, j)),
                  pl.BlockSpec((tm, hp), lambda i, j: (i, 0)),
                  pl.BlockSpec((tm, hp), lambda i, j: (i, 0))],
        out_specs=pl.BlockSpec((tm, hp), lambda i, j: (i, j)),
        out_shape=jax.ShapeDtypeStruct((m, MLA_HEADS * hp), BF16),
        scratch_shapes=[pltpu.VMEM((tm, k), BF16)],
        compiler_params=_cparams("parallel", "arbitrary"),
        name="mla_qproj",
    )(p, nw.reshape(1, k), wa, wb, ta, tb)


def _kv_assemble_kernel(kv_ref, kr_ref, tc_ref, ts_ref, k_ref, v_ref, *, rope):
    krr = kr_ref[...]
    kr = krr[:, :MLA_ROPE]
    if rope:
        kr = kr * tc_ref[...][:, :MLA_ROPE] + krr[:, MLA_ROPE:] * ts_ref[...][:, :MLA_ROPE]
    tail = jnp.concatenate([kr, jnp.zeros_like(kr)], axis=-1).astype(BF16)
    for h in range(MLA_HEADS):
        base = h * (MLA_NOPE + MLA_V)
        k_ref[:, h * MLA_HEAD_PAD:h * MLA_HEAD_PAD + MLA_NOPE] = kv_ref[:, base:base + MLA_NOPE]
        k_ref[:, h * MLA_HEAD_PAD + MLA_NOPE:(h + 1) * MLA_HEAD_PAD] = tail
        v_ref[:, h * MLA_V:(h + 1) * MLA_V] = kv_ref[:, base + MLA_NOPE:base + MLA_NOPE + MLA_V]


def kv_assemble(kvp, p, tc, ts, *, tm, rope):
    m = kvp.shape[0]
    return pl.pallas_call(
        functools.partial(_kv_assemble_kernel, rope=rope),
        grid=(m // tm,),
        in_specs=[pl.BlockSpec((tm, MLA_HEADS * (MLA_NOPE + MLA_V)), lambda i: (i, 0)),
                  pl.BlockSpec((tm, 128), lambda i: (i, P_KR // 128)),
                  pl.BlockSpec((tm, 128), lambda i: (i, 0)),
                  pl.BlockSpec((tm, 128), lambda i: (i, 0))],
        out_specs=[pl.BlockSpec((tm, MLA_HEADS * MLA_HEAD_PAD), lambda i: (i, 0)),
                   pl.BlockSpec((tm, MLA_HEADS * MLA_V), lambda i: (i, 0))],
        out_shape=[jax.ShapeDtypeStruct((m, MLA_HEADS * MLA_HEAD_PAD), BF16),
                   jax.ShapeDtypeStruct((m, MLA_HEADS * MLA_V), BF16)],
        compiler_params=_cparams("parallel"),
        name="mla_kv_assemble",
    )(kvp, p, tc, ts)


def _attn_kernel(q_ref, k_ref, v_ref, o_ref, s0_ref, s1_ref, m_ref, acc_ref, *, nk):
    j = pl.program_id(2)

    def qk(s_ref):
        s_ref[...] = lax.dot_general(q_ref[...], k_ref[...], (((1,), (1,)), ((), ())), preferred_element_type=F32)

    def softmax_pv(s_ref):
        s = s_ref[...]
        m_prev = m_ref[...]
        m_new = jnp.maximum(m_prev, jnp.max(s, axis=-1, keepdims=True))
        alpha = jnp.exp2(m_prev - m_new)
        p = jnp.exp2(s - m_new).astype(BF16)
        v = v_ref[...]
        lane = lax.broadcasted_iota(jnp.int32, v.shape, 1)
        ones_col = jnp.where(lane == 0, 1.0, 0.0).astype(BF16)
        v_ext = jnp.concatenate([v, ones_col], axis=1)
        acc_ref[...] = alpha * acc_ref[...] + _dot(p, v_ext)
        m_ref[...] = m_new

    @pl.when(j == 0)
    def _():
        m_ref[...] = jnp.full(m_ref.shape, -jnp.inf, F32)
        acc_ref[...] = jnp.zeros(acc_ref.shape, F32)
        qk(s0_ref)

    mid = jnp.logical_and(j > 0, j < nk)

    @pl.when(jnp.logical_and(mid, j % 2 == 1))
    def _():
        softmax_pv(s0_ref)
        qk(s1_ref)

    @pl.when(jnp.logical_and(mid, j % 2 == 0))
    def _():
        softmax_pv(s1_ref)
        qk(s0_ref)

    @pl.when(j == nk)
    def _():
        softmax_pv(s1_ref if (nk - 1) % 2 else s0_ref)
        acc = acc_ref[...]
        o_ref[...] = (acc[:, :MLA_V] / acc[:, MLA_V:MLA_V + 1]).astype(o_ref.dtype)


def attention(q, k, v, *, tq, tk):
    lq = q.shape[0]
    s = k.shape[0]
    nk = s // tk
    return pl.pallas_call(
        functools.partial(_attn_kernel, nk=nk),
        grid=(MLA_HEADS, lq // tq, nk + 1),
        in_specs=[pl.BlockSpec((tq, MLA_HEAD_PAD), lambda h, i, j: (i, h)),
                  pl.BlockSpec((tk, MLA_HEAD_PAD), lambda h, i, j: (jnp.minimum(j, nk - 1), h)),
                  pl.BlockSpec((tk, MLA_V), lambda h, i, j: (jnp.maximum(j - 1, 0), h))],
        out_specs=pl.BlockSpec((tq, MLA_V), lambda h, i, j: (i, h)),
        out_shape=jax.ShapeDtypeStruct((lq, MLA_HEADS * MLA_V), BF16),
        scratch_shapes=[pltpu.VMEM((tq, tk), F32), pltpu.VMEM((tq, tk), F32), pltpu.VMEM((tq, 1), F32),
                        pltpu.VMEM((tq, 2 * MLA_V), F32)],
        compiler_params=_cparams("parallel", "parallel", "arbitrary"),
        name="mla_attention",
    )(q, k, v)


def _shift_rows(x, prev_row, next_row):
    tm = x.shape[0]
    rows = lax.broadcasted_iota(jnp.int32, x.shape, 0)
    xp = jnp.where(rows == 0, prev_row, pltpu.roll(x, 1, 0))
    xn = jnp.where(rows == tm - 1, next_row, pltpu.roll(x, tm - 1, 0))
    return xp, xn


def _dwconv_kernel(x_ref, xprev_ref, xnext_ref, w_ref, b_ref, o_ref, *, act):
    i = pl.program_id(0)
    x = x_ref[...]
    prev_row = jnp.where(i == 0, 0.0, xprev_ref[7:8, :])
    next_row = jnp.where(i == pl.num_programs(0) - 1, 0.0, xnext_ref[0:1, :])
    xp, xn = _shift_rows(x, prev_row, next_row)
    y = xp * w_ref[0:1, :] + x * w_ref[1:2, :] + xn * w_ref[2:3, :] + b_ref[...]
    if act:
        y = _silu(y)
    o_ref[...] = y


def dwconv3(x, w, b, *, tm, tc, col0, width, act):
    m = x.shape[0]
    cb = col0 // tc
    nrb = m // 8
    tb = tm // 8
    return pl.pallas_call(
        functools.partial(_dwconv_kernel, act=act),
        grid=(m // tm, width // tc),
        in_specs=[pl.BlockSpec((tm, tc), lambda i, j: (i, cb + j)),
                  pl.BlockSpec((8, tc), lambda i, j: (jnp.maximum(i * tb - 1, 0), cb + j)),
                  pl.BlockSpec((8, tc), lambda i, j: (jnp.minimum((i + 1) * tb, nrb - 1), cb + j)),
                  pl.BlockSpec((3, tc), lambda i, j: (0, j)),
                  pl.BlockSpec((1, tc), lambda i, j: (0, j))],
        out_specs=pl.BlockSpec((tm, tc), lambda i, j: (i, j)),
        out_shape=jax.ShapeDtypeStruct((m, width), F32),
        compiler_params=_cparams("arbitrary", "arbitrary"),
        name="dwconv3",
    )(x, x, x, w, b.reshape(1, width))


def _softplus(x):
    return jnp.maximum(x, 0.0) + jnp.log(1.0 + jnp.exp(-jnp.abs(x)))


def _ssd_kernel(xbc_ref, dt_ref, dtt_ref, bias_ref, biast_ref, a_ref, at_ref, tri_ref, trit_ref,
                o_ref, h_ref):
    d = pl.program_id(0)

    @pl.when(pl.program_id(1) == 0)
    def _():
        h_ref[...] = jnp.zeros(h_ref.shape, F32)

    q = SSD_CHUNK
    xbc = xbc_ref[...]
    dt = _softplus(dt_ref[0] + bias_ref[0])
    dtt = _softplus(dtt_ref[0, 0] + biast_ref[0])
    dta_hi, dta_lo = _split(dt * a_ref[0])
    dtat_hi, dtat_lo = _split(dtt * at_ref[0])
    tri = tri_ref[0]
    trit = trit_ref[0]
    acum = _dot(tri, dta_hi) + _dot(tri, dta_lo)
    acumt = _dot(dtat_hi, trit) + _dot(dtat_lo, trit)
    total = jnp.where(d == 0, acum[q - 1:q, :], acum[0:1, :])
    to_end = jnp.exp(total - acum)
    from_start = jnp.exp(acum)
    chunk_decay = jnp.exp(total)
    mask = tri > 0.5

    for g in range(SSD_GROUPS):
        bm = xbc[:, SSD_INNER + g * SSD_STATE:SSD_INNER + (g + 1) * SSD_STATE]
        cm = xbc[:, SSD_INNER + (SSD_GROUPS + g) * SSD_STATE:SSD_INNER + (SSD_GROUPS + g + 1) * SSD_STATE]
        bm_b = bm.astype(BF16)
        cm_b = cm.astype(BF16)
        cb = lax.dot_general(cm_b, bm_b, (((1,), (1,)), ((), ())), preferred_element_type=F32)
        bt_b = bm.T.astype(BF16)
        for r in range(SSD_HPG):
            h = g * SSD_HPG + r
            xs = xbc[:, h * SSD_HEAD_DIM:(h + 1) * SSD_HEAD_DIM]
            xdt = xs * dt[:, h:h + 1]
            seg = acum[:, h:h + 1] - acumt[h:h + 1, :]
            decay = jnp.exp(jnp.where(mask, seg, -jnp.inf))
            y_diag = _dot((cb * decay).astype(BF16), xdt.astype(BF16))
            h_prev = h_ref[h]
            y_off = _dot(cm_b, h_prev.astype(BF16)) * from_start[:, h:h + 1]
            states = _dot(bt_b, (xdt * to_end[:, h:h + 1]).astype(BF16))
            h_ref[h] = h_prev * chunk_decay[:, h:h + 1] + states
            o_ref[0, :, h * SSD_HEAD_DIM:(h + 1) * SSD_HEAD_DIM] = y_diag + y_off


def ssd_scan(xbc, dt2, dt2t, bias2, bias2t, a2, a2t, tri2, *, n_lat_chunks):
    nc = n_lat_chunks
    ncx = CTX_LEN // SSD_CHUNK
    tot = nc + ncx
    q = SSD_CHUNK

    def chunk(d, s):
        return jnp.where(d == 0, lax.rem(s + nc, tot), tot - 1 - s)

    def out_chunk(d, s):
        c = chunk(d, s)
        return jnp.where(c >= nc, jnp.where(d == 0, 0, nc - 1), c)

    return pl.pallas_call(
        _ssd_kernel,
        grid=(2, tot),
        in_specs=[pl.BlockSpec((q, SSD_XBC), lambda d, s: (chunk(d, s), 0)),
                  pl.BlockSpec((1, q, SSD_HEADS), lambda d, s: (d, chunk(d, s), 0)),
                  pl.BlockSpec((1, 1, SSD_HEADS, q), lambda d, s: (d, chunk(d, s), 0, 0)),
                  pl.BlockSpec((1, 1, SSD_HEADS), lambda d, s: (d, 0, 0)),
                  pl.BlockSpec((1, SSD_HEADS, 1), lambda d, s: (d, 0, 0)),
                  pl.BlockSpec((1, 1, SSD_HEADS), lambda d, s: (d, 0, 0)),
                  pl.BlockSpec((1, SSD_HEADS, 1), lambda d, s: (d, 0, 0)),
                  pl.BlockSpec((1, q, q), lambda d, s: (d, 0, 0)),
                  pl.BlockSpec((1, q, q), lambda d, s: (1 - d, 0, 0))],
        out_specs=pl.BlockSpec((1, q, SSD_INNER), lambda d, s: (d, out_chunk(d, s), 0)),
        out_shape=jax.ShapeDtypeStruct((2, nc * q, SSD_INNER), F32),
        scratch_shapes=[pltpu.VMEM((SSD_HEADS, SSD_STATE, SSD_HEAD_DIM), F32)],
        compiler_params=_cparams("arbitrary", "arbitrary"),
        name="ssd_scan",
    )(xbc, dt2, dt2t, bias2, bias2t, a2, a2t, tri2, tri2)


def _merge_kernel(o_ref, yf_ref, yb_ref, xs_ref, z_ref, dsk_ref, nw_ref, w_ref, x_ref, g_ref, out_ref, a_ref):
    @pl.when(pl.program_id(1) == 0)
    def _():
        a_ref[:, :MLA_HEADS * MLA_V] = o_ref[...]
        y = yf_ref[0] + yb_ref[0] + dsk_ref[...] * xs_ref[...]
        gy = y * _silu(z_ref[...])
        gw = SSD_INNER // SSD_GROUPS
        for g in range(SSD_GROUPS):
            part = gy[:, g * gw:(g + 1) * gw]
            ms = jnp.mean(part * part, axis=-1, keepdims=True)
            a_ref[:, MLA_HEADS * MLA_V + g * gw:MLA_HEADS * MLA_V + (g + 1) * gw] = (
                part * lax.rsqrt(ms + EPS) * nw_ref[:, g * gw:(g + 1) * gw]).astype(BF16)

    out_ref[...] = x_ref[...] + g_ref[...] * _dot(a_ref[...], w_ref[...])


def merge_out(o_att, y2, xbc, p, dsk, nw, w_o, x, gate, *, tm, tn):
    m, n = x.shape
    kw = w_o.shape[0]
    return pl.pallas_call(
        _merge_kernel,
        grid=(m // tm, n // tn),
        in_specs=[pl.BlockSpec((tm, MLA_HEADS * MLA_V), lambda i, j: (i, 0)),
                  pl.BlockSpec((1, tm, SSD_INNER), lambda i, j: (0, i, 0)),
                  pl.BlockSpec((1, tm, SSD_INNER), lambda i, j: (1, i, 0)),
                  pl.BlockSpec((tm, SSD_INNER), lambda i, j: (i, 0)),
                  pl.BlockSpec((tm, SSD_INNER), lambda i, j: (i, 0)),
                  pl.BlockSpec((1, SSD_INNER), lambda i, j: (0, 0)),
                  pl.BlockSpec((1, SSD_INNER), lambda i, j: (0, 0)),
                  pl.BlockSpec((kw, tn), lambda i, j: (0, j)),
                  pl.BlockSpec((tm, tn), lambda i, j: (i, j)),
                  pl.BlockSpec((1, tn), lambda i, j: (0, j))],
        out_specs=pl.BlockSpec((tm, tn), lambda i, j: (i, j)),
        out_shape=jax.ShapeDtypeStruct((m, n), F32),
        scratch_shapes=[pltpu.VMEM((tm, kw), BF16)],
        compiler_params=_cparams("parallel", "arbitrary"),
        name="merge_out",
    )(o_att, y2, y2, xbc, p, dsk, nw, w_o, x, gate)


def _mmres_kernel(a_ref, w_ref, b_ref, x_ref, g_ref, o_ref):
    o_ref[...] = x_ref[...] + g_ref[...] * (_dot(a_ref[...].astype(BF16), w_ref[...]) + b_ref[...])


def mm_res(a, w, b, x, gate, *, tm, tn):
    m, k = a.shape
    n = w.shape[1]
    return pl.pallas_call(
        _mmres_kernel,
        grid=(m // tm, n // tn),
        in_specs=[pl.BlockSpec((tm, k), lambda i, j: (i, 0)),
                  pl.BlockSpec((k, tn), lambda i, j: (0, j)),
                  pl.BlockSpec((1, tn), lambda i, j: (0, j)),
                  pl.BlockSpec((tm, tn), lambda i, j: (i, j)),
                  pl.BlockSpec((1, tn), lambda i, j: (0, j))],
        out_specs=pl.BlockSpec((tm, tn), lambda i, j: (i, j)),
        out_shape=jax.ShapeDtypeStruct((m, n), F32),
        compiler_params=_cparams("parallel", "arbitrary"),
        name="mm_res",
    )(a, w, b.reshape(1, n), x, gate)


def _ffn_kernel(x_ref, nw_ref, sh_ref, sc_ref, g_ref, w1_ref, w3_ref, w2_ref, fw_ref, o_ref, xn_ref, acc_ref,
                *, final_norm):
    f = pl.program_id(1)

    @pl.when(f == 0)
    def _():
        xn_ref[...] = _modnorm(x_ref[...], nw_ref[...], sh_ref[...], sc_ref[...]).astype(BF16)
        acc_ref[...] = jnp.zeros(acc_ref.shape, F32)

    xn = xn_ref[...]
    a = _dot(xn, w1_ref[...])
    b = _dot(xn, w3_ref[...])
    acc_ref[...] += _dot((_silu(a) * b).astype(BF16), w2_ref[...])

    @pl.when(f == pl.num_programs(1) - 1)
    def _():
        y = x_ref[...] + g_ref[...] * acc_ref[...]
        if final_norm:
            ms = jnp.mean(y * y, axis=-1, keepdims=True)
            y = y * lax.rsqrt(ms + EPS) * fw_ref[...]
        o_ref[...] = y


def ffn(x, nw, sh, sc, gate, w1, w3, w2, fw, *, tm, tf, final_norm):
    m, dm = x.shape
    dff = w1.shape[1]
    vec = pl.BlockSpec((1, dm), lambda i, f: (0, 0))
    return pl.pallas_call(
        functools.partial(_ffn_kernel, final_norm=final_norm),
        grid=(m // tm, dff // tf),
        in_specs=[pl.BlockSpec((tm, dm), lambda i, f: (i, 0)), vec, vec, vec, vec,
                  pl.BlockSpec((dm, tf), lambda i, f: (0, f)),
                  pl.BlockSpec((dm, tf), lambda i, f: (0, f)),
                  pl.BlockSpec((tf, dm), lambda i, f: (f, 0)),
                  vec],
        out_specs=pl.BlockSpec((tm, dm), lambda i, f: (i, 0)),
        out_shape=jax.ShapeDtypeStruct((m, dm), F32),
        scratch_shapes=[pltpu.VMEM((tm, dm), BF16), pltpu.VMEM((tm, dm), F32)],
        compiler_params=_cparams("parallel", "arbitrary"),
        name="ffn",
    )(x, nw.reshape(1, dm), sh, sc, gate, w1, w3, w2, fw.reshape(1, dm))


def _filter_kernel(fvec_ref, w1h_ref, w1l_ref, b1_ref, wmh_ref, wml_ref, bm_ref, freq_ref, woh_ref, wol_ref,
                   delta_ref, k_ref, abs_ref, *, seq_len, tr):
    i = pl.program_id(0)
    n = i * tr + lax.broadcasted_iota(jnp.int32, (tr, 1), 0)
    lag = jnp.where(n < seq_len, n, 2 * seq_len - n).astype(F32)
    t = lag / float(seq_len - 1)
    ang = lag * (2.0 * math.pi / seq_len)
    lane = lax.broadcasted_iota(jnp.int32, (tr, 128), 1)
    bands = (HY_EMB - 1) // 2
    arg = ang * fvec_ref[...]
    emb = jnp.where(lane == 0, t, jnp.where(lane <= bands, jnp.cos(arg), jnp.where(lane <= 2 * bands, -jnp.sin(arg), 0.0)))
    fr = freq_ref[...]
    e_hi, e_lo = _split(emb)
    hid = jnp.sin(fr * (_dot3(e_hi, e_lo, w1h_ref[...], w1l_ref[...]) + b1_ref[...]))
    for j in range(wmh_ref.shape[0]):
        h_hi, h_lo = _split(hid)
        hid = jnp.sin(fr * (_dot3(h_hi, h_lo, wmh_ref[j], wml_ref[j]) + bm_ref[j]))
    h_hi, h_lo = _split(hid)
    k = _dot3(h_hi, h_lo, woh_ref[0], wol_ref[0]) * jnp.exp(-t * delta_ref[...])
    k = jnp.where(n == seq_len, 0.0, k)
    k_ref[...] = k

    @pl.when(i == 0)
    def _():
        abs_ref[...] = jnp.zeros(abs_ref.shape, F32)

    abs_ref[...] += jnp.sum(jnp.abs(k), axis=0, keepdims=True)


def hyena_filter(fvec, w1h, w1l, b1, wmh, wml, bm, freq, woh, wol, delta, *, seq_len, tr):
    d = delta.shape[1]
    half = seq_len // tr
    full = lambda shp: pl.BlockSpec(shp, lambda i: (0,) * len(shp))
    return pl.pallas_call(
        functools.partial(_filter_kernel, seq_len=seq_len, tr=tr),
        grid=(2 * seq_len // tr,),
        in_specs=[full((1, 128)), full(w1h.shape), full(w1l.shape), full(b1.shape), full(wmh.shape),
                  full(wml.shape), full(bm.shape), full(freq.shape),
                  pl.BlockSpec((1, HY_HID, d), lambda i: (i // half, 0, 0)),
                  pl.BlockSpec((1, HY_HID, d), lambda i: (i // half, 0, 0)),
                  full((1, d))],
        out_specs=[pl.BlockSpec((tr, d), lambda i: (i, 0)), pl.BlockSpec((1, d), lambda i: (0, 0))],
        out_shape=[jax.ShapeDtypeStruct((2 * seq_len, d), F32), jax.ShapeDtypeStruct((1, d), F32)],
        compiler_params=_cparams("arbitrary"),
        name="hyena_filter",
    )(fvec, w1h, w1l, b1, wmh, wml, bm, freq, woh, wol, delta)


ROW_GROUP = 8


def _fft1_kernel(fh_ref, fl_ref, x_ref, o_ref, xs_ref):
    for r in range(ROW_GROUP):
        xs_ref[...] = x_ref[:, r, :]
        x_hi, x_lo = _split(xs_ref[...])
        o_ref[:, r, :] = _dot3(fh_ref[...], fl_ref[...], x_hi, x_lo)


def _fft1_gate_kernel(fh_ref, fl_ref, x_ref, v_ref, gate_ref, fb_ref, o_ref, xs_ref):
    for r in range(ROW_GROUP):
        xs_ref[...] = x_ref[:, r, :]
        x_hi, x_lo = _split(xs_ref[...])
        o_ref[:, r, :] = _dot3(fh_ref[...], fl_ref[...], x_hi, x_lo)
    o_ref[...] = gate_ref[...] * (o_ref[...] + fb_ref[...] * v_ref[...])


def fft_stage1(fh, fl, x3, *, d, tn, col_ofs=0, gate_args=None):
    mrows, k = fh.shape
    n2 = x3.shape[1]
    nd = d // tn
    fspec = pl.BlockSpec((mrows, k), lambda i, j: (0, 0))
    xspec = pl.BlockSpec((k, ROW_GROUP, tn), lambda i, j: (0, i, col_ofs * nd + j))
    ospec = pl.BlockSpec((mrows, ROW_GROUP, tn), lambda i, j: (0, i, j))
    common = dict(grid=(n2 // ROW_GROUP, nd), out_specs=ospec,
                  out_shape=jax.ShapeDtypeStruct((mrows, n2, d), F32),
                  scratch_shapes=[pltpu.VMEM((k, tn), F32)],
                  compiler_params=_cparams("parallel", "arbitrary"))
    if gate_args is None:
        return pl.pallas_call(_fft1_kernel, in_specs=[fspec, fspec, xspec], name="fft_stage1", **common)(fh, fl, x3)
    v3, vofs, gate3, gofs, fb = gate_args
    return pl.pallas_call(
        _fft1_gate_kernel,
        in_specs=[fspec, fspec, xspec,
                  pl.BlockSpec((mrows, ROW_GROUP, tn), lambda i, j: (0, i, vofs * nd + j)),
                  pl.BlockSpec((mrows, ROW_GROUP, tn), lambda i, j: (0, i, gofs * nd + j)),
                  pl.BlockSpec((1, tn), lambda i, j: (0, j))],
        name="fft_stage1_inv", **common)(fh, fl, x3, v3, gate3, fb)


def _fftmid_kernel(a_ref, mfh_ref, mfl_ref, kf_ref, mih_ref, mil_ref, o_ref):
    half = FFT_N2
    for s in range(SLAB_TILE):
        a_hi, a_lo = _split(a_ref[s])
        y = _dot3(mfh_ref[s], mfl_ref[s], a_hi, a_lo)
        kf = kf_ref[s]
        yr, yi = y[:half], y[half:]
        kr, ki = kf[:half], kf[half:]
        pr = yr * kr - yi * ki
        pi = yr * ki + yi * kr
        p_hi, p_lo = _split(jnp.concatenate([pr, pi], axis=0))
        o_ref[s] = _dot3(mih_ref[s], mil_ref[s], p_hi, p_lo)


def fft_mid(a3, mfh, mfl, kf3, mih, mil, *, dc):
    nsp, rows, d = a3.shape
    blk = pl.BlockSpec((SLAB_TILE, rows, dc), lambda s, j: (s, 0, j))
    mat = pl.BlockSpec((SLAB_TILE, rows, rows), lambda s, j: (s, 0, 0))
    return pl.pallas_call(
        _fftmid_kernel,
        grid=(nsp // SLAB_TILE, d // dc),
        in_specs=[blk, mat, mat, blk, mat, mat],
        out_specs=blk,
        out_shape=jax.ShapeDtypeStruct((nsp, rows, d), F32),
        compiler_params=_cparams("parallel", "arbitrary"),
        name="fft_mid",
    )(a3, mfh, mfl, kf3, mih, mil)


def _fftspec_kernel(a_ref, mfh_ref, mfl_ref, scale_ref, o_ref):
    for s in range(SLAB_TILE):
        a_hi, a_lo = _split(a_ref[s])
        o_ref[s] = _dot3(mfh_ref[s], mfl_ref[s], a_hi, a_lo) * scale_ref[...]


def fft_spectrum(a3, mfh, mfl, scale, *, dc):
    nsp, rows, d = a3.shape
    blk = pl.BlockSpec((SLAB_TILE, rows, dc), lambda s, j: (s, 0, j))
    mat = pl.BlockSpec((SLAB_TILE, rows, rows), lambda s, j: (s, 0, 0))
    return pl.pallas_call(
        _fftspec_kernel,
        grid=(nsp // SLAB_TILE, d // dc),
        in_specs=[blk, mat, mat, pl.BlockSpec((1, dc), lambda s, j: (0, j))],
        out_specs=blk,
        out_shape=jax.ShapeDtypeStruct((nsp, rows, d), F32),
        compiler_params=_cparams("parallel", "arbitrary"),
        name="fft_spectrum",
    )(a3, mfh, mfl, scale)


def _fft_tables(seq_len):
    n = 2 * seq_len
    n2 = FFT_N2
    n1 = n // n2
    ns = n1 // 2 + 1
    nsp = -(-ns // SLAB_TILE) * SLAB_TILE
    k1 = jnp.arange(nsp, dtype=jnp.int32)
    valid = (k1 < ns)
    m1 = jnp.arange(n1, dtype=jnp.int32)
    ang1 = (2.0 * math.pi / n1) * lax.rem(k1[:, None] * m1[None, :], n1).astype(F32)
    vf = valid[:, None].astype(F32)
    f1 = jnp.stack([jnp.cos(ang1) * vf, -jnp.sin(ang1) * vf], axis=1).reshape(2 * nsp, n1)
    wgt = jnp.where((k1 == 0) | (k1 == n1 // 2), 1.0, 2.0) * valid.astype(F32) / n
    g1 = jnp.stack([jnp.cos(ang1) * wgt[:, None], -jnp.sin(ang1) * wgt[:, None]], axis=1)
    g1 = g1.reshape(2 * nsp, n1).T[: n1 // 2]
    k2 = jnp.arange(n2, dtype=jnp.int32)
    m2 = jnp.arange(n2, dtype=jnp.int32)
    f = k1[:, None, None] + n1 * k2[None, :, None]
    ang2 = (2.0 * math.pi / n) * lax.rem(f * m2[None, None, :], n).astype(F32)
    vm = valid[:, None, None].astype(F32)
    c2, s2 = jnp.cos(ang2) * vm, jnp.sin(ang2) * vm
    mf = jnp.concatenate([jnp.concatenate([c2, s2], axis=2), jnp.concatenate([-s2, c2], axis=2)], axis=1)
    c2t, s2t = jnp.swapaxes(c2, 1, 2), jnp.swapaxes(s2, 1, 2)
    mi = jnp.concatenate([jnp.concatenate([c2t, -s2t], axis=2), jnp.concatenate([s2t, c2t], axis=2)], axis=1)
    return dict(n1=n1, nsp=nsp, f1=_split(f1), g1=_split(g1), mf=_split(mf), mi=_split(mi))


def long_conv_gate(v, v_cols, gate, gate_cols, fbias, kf3, tabs, *, seq_len, d):
    n1, nsp = tabs["n1"], tabs["nsp"]
    half = n1 // 2
    v3 = v.reshape(half, FFT_N2, v.shape[1])
    g3 = gate.reshape(half, FFT_N2, gate.shape[1])
    a = fft_stage1(tabs["f1"][0][:, :half], tabs["f1"][1][:, :half], v3, d=d, tn=512, col_ofs=v_cols)
    a3 = a.reshape(nsp, 2 * FFT_N2, d)
    b3 = fft_mid(a3, tabs["mf"][0], tabs["mf"][1], kf3, tabs["mi"][0], tabs["mi"][1], dc=256)
    b = b3.reshape(2 * nsp, FFT_N2, d)
    out = fft_stage1(tabs["g1"][0], tabs["g1"][1], b, d=d, tn=512,
                     gate_args=(v3, v_cols, g3, gate_cols, fbias))
    return out.reshape(seq_len, d)


def filter_spectrum(k, kabs, tabs, *, d):
    n1, nsp = tabs["n1"], tabs["nsp"]
    a = fft_stage1(tabs["f1"][0], tabs["f1"][1], k.reshape(n1, FFT_N2, d), d=d, tn=512)
    return fft_spectrum(a.reshape(nsp, 2 * FFT_N2, d), tabs["mf"][0], tabs["mf"][1], 1.0 / kabs, dc=256)


def _rope_tables(seq_len, scale):
    n_freq = MLA_ROPE // 4
    rows = seq_len // GRID_W
    row = jnp.repeat(jnp.arange(rows, dtype=F32), GRID_W)
    col = jnp.tile(jnp.arange(GRID_W, dtype=F32), rows)
    inv = ROPE_THETA ** (-jnp.arange(n_freq, dtype=F32) / n_freq)
    ang = jnp.stack([row[:, None] * inv, col[:, None] * inv], axis=1)
    cos = jnp.broadcast_to(jnp.cos(ang)[:, :, None, :], (seq_len, 2, 2, n_freq)).reshape(seq_len, MLA_ROPE)
    sin = jnp.broadcast_to(jnp.sin(ang)[:, :, None, :], (seq_len, 2, 2, n_freq)).reshape(seq_len, MLA_ROPE)
    one = jnp.ones((seq_len, MLA_NOPE), F32)
    zero = jnp.zeros((seq_len, MLA_NOPE), F32)
    z64 = jnp.zeros((seq_len, MLA_HEAD_PAD - MLA_NOPE - MLA_ROPE), F32)
    ta = jnp.concatenate([one, cos, z64], axis=1) * scale
    tb = jnp.concatenate([zero, sin, z64], axis=1) * scale
    tc = jnp.concatenate([cos, z64], axis=1)
    ts = jnp.concatenate([sin, z64], axis=1)
    return ta, tb, tc, ts


def _rope_swap_cols(w):
    n_freq = MLA_ROPE // 4
    w4 = w.reshape(w.shape[0], 2, 2, n_freq)
    return jnp.stack([-w4[:, :, 1], w4[:, :, 0]], axis=2).reshape(w.shape[0], MLA_ROPE)


def _even_layer(x, ctx, mods, norm_mix_w, w_in, conv_w, conv_b, dt_bias, a_log, d_skip, ssd_norm_w,
                q_norm_w, w_uq, kv_norm_w, w_ukv, w_o):
    seq_len, d = x.shape
    sh1, sc1, g1 = (mods[0:1, i * d:(i + 1) * d] for i in range(3))
    csh1, csc1 = mods[1:2, 0:d], mods[1:2, d:2 * d]

    o1 = Q_SIDE + SSD_XBC
    o2 = o1 + SSD_DT
    o3 = o2 + MLA_KV_RANK
    w_kr = w_in[:, o3:]
    w_ext = jnp.concatenate([w_in[:, :Q_SIDE], w_in[:, Q_SIDE:o1], w_in[:, o2:o3], w_kr, _rope_swap_cols(w_kr),
                             w_in[:, o1:o2], jnp.zeros((d, P_COLS - P_DT - SSD_DT), F32)], axis=1).astype(BF16)
    zb = jnp.zeros((P_COLS,), F32)
    p_lat = normmm(x, norm_mix_w, sh1, sc1, w_ext, zb, tm=512, tn=768, out_dtype=F32)
    p_ctx = normmm(ctx, norm_mix_w, csh1, csc1, w_ext, zb, tm=CTX_LEN, tn=768, out_dtype=F32)

    scale = float(MLA_NOPE + MLA_ROPE) ** -0.5 * math.log2(math.e)
    ta, tb, tc, ts = _rope_tables(seq_len, scale)
    wq = w_uq.reshape(MLA_Q_RANK, MLA_HEADS, MLA_NOPE + MLA_ROPE)
    zpad = jnp.zeros((MLA_Q_RANK, MLA_HEADS, MLA_HEAD_PAD - MLA_NOPE - MLA_ROPE), F32)
    wa = jnp.concatenate([wq, zpad], axis=2).reshape(MLA_Q_RANK, -1).astype(BF16)
    wr = wq[:, :, MLA_NOPE:]
    n_freq = MLA_ROPE // 4
    wr4 = wr.reshape(MLA_Q_RANK, MLA_HEADS, 2, 2, n_freq)
    wsw = jnp.stack([-wr4[:, :, :, 1], wr4[:, :, :, 0]], axis=3).reshape(MLA_Q_RANK, MLA_HEADS, MLA_ROPE)
    wb = jnp.concatenate([jnp.zeros((MLA_Q_RANK, MLA_HEADS, MLA_NOPE), F32), wsw, zpad], axis=2)
    wb = wb.reshape(MLA_Q_RANK, -1).astype(BF16)
    q = qproj(p_lat, q_norm_w, wa, wb, ta, tb, tm=512)

    zk = jnp.zeros((MLA_KV_RANK,), F32)
    w_ukv_b = w_ukv.astype(BF16)
    zkb = jnp.zeros((w_ukv.shape[1],), F32)
    kv_lat = normmm(p_lat, kv_norm_w, zk, zk, w_ukv_b, zkb, tm=512, tn=512, out_dtype=BF16,
                    x_col=P_CKV // MLA_KV_RANK)
    kv_ctx = normmm(p_ctx, kv_norm_w, zk, zk, w_ukv_b, zkb, tm=CTX_LEN, tn=512, out_dtype=BF16,
                    x_col=P_CKV // MLA_KV_RANK)
    k_lat, v_lat = kv_assemble(kv_lat, p_lat, tc, ts, tm=512, rope=True)
    k_ctx, v_ctx = kv_assemble(kv_ctx, p_ctx, tc, ts, tm=CTX_LEN, rope=False)
    k_all = jnp.concatenate([k_lat, k_ctx], axis=0)
    v_all = jnp.concatenate([v_lat, v_ctx], axis=0)
    s_tot = seq_len + CTX_LEN
    o_att = attention(q, k_all, v_all, tq=_pick(seq_len, (2048, 1024, 512, 256)),
                      tk=_pick(s_tot, (1280, 1024, 768, 640, 512, 384, 256, 128)))

    xbc_lat = dwconv3(p_lat, conv_w, conv_b, tm=512, tc=SSD_XBC, col0=P_XBC, width=SSD_XBC, act=True)
    xbc_ctx = dwconv3(p_ctx, conv_w, conv_b, tm=CTX_LEN, tc=SSD_XBC, col0=P_XBC, width=SSD_XBC, act=True)
    xbc_all = jnp.concatenate([xbc_lat, xbc_ctx], axis=0)
    dt_all = jnp.concatenate([p_lat[:, P_DT:P_DT + SSD_DT], p_ctx[:, P_DT:P_DT + SSD_DT]], axis=0)
    nch = s_tot // SSD_CHUNK
    dt2 = dt_all.reshape(s_tot, 2, SSD_HEADS).transpose(1, 0, 2)
    dt2t = dt2.reshape(2, nch, SSD_CHUNK, SSD_HEADS).transpose(0, 1, 3, 2)
    bias2 = dt_bias.reshape(2, 1, SSD_HEADS)
    bias2t = dt_bias.reshape(2, SSD_HEADS, 1)
    a_neg = -jnp.exp(a_log.astype(F32))
    a2 = a_neg.reshape(2, 1, SSD_HEADS)
    a2t = a_neg.reshape(2, SSD_HEADS, 1)
    lower = jnp.tril(jnp.ones((SSD_CHUNK, SSD_CHUNK), F32))
    tri2 = jnp.stack([lower, lower.T]).astype(BF16)
    y2 = ssd_scan(xbc_all, dt2, dt2t, bias2, bias2t, a2, a2t, tri2, n_lat_chunks=seq_len // SSD_CHUNK)

    dsk = jnp.repeat(d_skip[0] + d_skip[1], SSD_HEAD_DIM).reshape(1, SSD_INNER)
    return merge_out(o_att, y2, xbc_lat, p_lat, dsk, ssd_norm_w.reshape(1, SSD_INNER), w_o.astype(BF16), x, g1,
                     tm=512, tn=512)


def _odd_layer(x, mods, norm_mix_w, w_in, b_in, short_w, short_b, fw1, fb1, fw_mid, fb_mid, freq, fw_out,
               fbias, w_out, b_out):
    seq_len, d = x.shape
    sh1, sc1, g1 = (mods[0:1, i * d:(i + 1) * d] for i in range(3))
    proj = normmm(x, norm_mix_w, sh1, sc1, w_in.astype(BF16), b_in, tm=512, tn=768, out_dtype=F32)
    pc = dwconv3(proj, short_w, short_b, tm=512, tc=1536, col0=0, width=3 * d, act=False)

    tabs = _fft_tables(seq_len)
    bands = (HY_EMB - 1) // 2
    fband = jnp.linspace(1e-4, bands - 1, bands, dtype=F32)
    fvec = jnp.concatenate([jnp.zeros((1,), F32), fband, fband, jnp.zeros((128 - HY_EMB,), F32)]).reshape(1, 128)
    w1p = jnp.concatenate([fw1.astype(F32), jnp.zeros((128 - HY_EMB, HY_HID), F32)], axis=0)
    w1h, w1l = _split(w1p)
    wmh, wml = _split(fw_mid.astype(F32))
    lo = math.log(HY_SLOW_DECAY) / HY_TARGET
    hi = math.log(HY_FAST_DECAY) / HY_TARGET
    delta = jnp.abs(jnp.linspace(lo, hi, d, dtype=F32)).reshape(1, d)
    y_cols = 2
    y = None
    for i in range(2):
        wo = jnp.transpose(fw_out[:, i].astype(F32), (1, 0, 2))
        woh, wol = _split(wo)
        k, kabs = hyena_filter(fvec, w1h, w1l, fb1.reshape(1, HY_HID), wmh, wml,
                               fb_mid.reshape(-1, 1, HY_HID), freq.reshape(1, HY_HID), woh, wol, delta,
                               seq_len=seq_len, tr=256)
        kf3 = filter_spectrum(k, kabs, tabs, d=d)
        if y is None:
            y = long_conv_gate(pc, y_cols, pc, i, fbias[i].reshape(1, d), kf3, tabs, seq_len=seq_len, d=d)
        else:
            y = long_conv_gate(y, 0, pc, i, fbias[i].reshape(1, d), kf3, tabs, seq_len=seq_len, d=d)
    y = y.reshape(seq_len, d)
    return mm_res(y, w_out.astype(BF16), b_out, x, g1, tm=512, tn=512)


def kernel(x, c, ctx, c_ctx, mod_w, mod_b, norm_mix_w, norm_ffn_w, ffn_w1, ffn_w3, ffn_w2, ev_w_in, ev_conv_w, ev_conv_b, ev_dt_bias, ev_a_log, ev_d_skip, ev_ssd_norm_w, ev_q_norm_w, ev_w_uq, ev_kv_norm_w, ev_w_ukv, ev_w_o, hy_w_in, hy_b_in, hy_short_w, hy_short_b, hy_fw1, hy_fb1, hy_fw_mid, hy_fb_mid, hy_freq, hy_fw_out, hy_fbias, hy_w_out, hy_b_out, final_norm_w):
    assert x.shape[0] == 1 and mod_w.shape[0] == 2
    xs = x[0]
    xc = ctx[0]
    d = xs.shape[1]
    vecs = jnp.concatenate([c.reshape(1, d), c_ctx.reshape(1, d), jnp.zeros((6, d), F32)], axis=0)
    depth = mod_w.shape[0]
    for i in range(depth):
        mods = adaln_vectors(vecs, mod_w[i], mod_b[i])
        sh2, sc2, g2 = (mods[0:1, j * d:(j + 1) * d] for j in range(3, 6))
        if i % 2 == 0:
            e = i // 2
            xs = _even_layer(xs, xc, mods, norm_mix_w[i], ev_w_in[e], ev_conv_w[e], ev_conv_b[e], ev_dt_bias[e],
                             ev_a_log[e], ev_d_skip[e], ev_ssd_norm_w[e], ev_q_norm_w[e], ev_w_uq[e],
                             ev_kv_norm_w[e], ev_w_ukv[e], ev_w_o[e])
        else:
            o = i // 2
            xs = _odd_layer(xs, mods, norm_mix_w[i], hy_w_in[o], hy_b_in[o], hy_short_w[o], hy_short_b[o],
                            hy_fw1[o], hy_fb1[o], hy_fw_mid[o], hy_fb_mid[o], hy_freq[o], hy_fw_out[o],
                            hy_fbias[o], hy_w_out[o], hy_b_out[o])
        xs = ffn(xs, norm_ffn_w[i], sh2, sc2, g2, ffn_w1[i].astype(BF16), ffn_w3[i].astype(BF16),
                 ffn_w2[i].astype(BF16), final_norm_w, tm=512, tf=512, final_norm=(i == depth - 1))
    return xs[None]
```

```python
import functools
import math

import jax
import jax.numpy as jnp
import numpy as np
from jax import lax
from jax.experimental import pallas as pl
from jax.experimental.pallas import tpu as pltpu

F32 = jnp.float32
BF16 = jnp.bfloat16

EPS = 1e-6
GRID_W = 64
CTX_LEN = 256
SSD_HEADS = 16
SSD_HEAD_DIM = 64
SSD_INNER = 1024
SSD_GROUPS = 2
SSD_HPG = 8
SSD_STATE = 128
SSD_CHUNK = 128
SSD_XBC = 1536
SSD_DT = 32
MLA_HEADS = 8
MLA_NOPE = 128
MLA_ROPE = 64
MLA_V = 128
MLA_Q_RANK = 512
MLA_KV_RANK = 512
ROPE_THETA = 10000.0
MLA_HEAD_PAD = 256
Q_SIDE = SSD_INNER + MLA_Q_RANK
HY_EMB = 33
HY_HID = 64
HY_FAST_DECAY = 0.3
HY_SLOW_DECAY = 1.5
HY_TARGET = 1e-2
FFT_N2 = 128
SLAB_TILE = 8

P_Z, P_CQ, P_XBC, P_CKV, P_KR, P_DT, P_COLS = 0, 1024, 1536, 3072, 3584, 3712, 3840

VMEM_LIMIT = 56 * 1024 * 1024


def _cparams(*sem):
    return pltpu.CompilerParams(dimension_semantics=sem, vmem_limit_bytes=VMEM_LIMIT)


def _pick(n, cands):
    for c in cands:
        if n % c == 0:
            return c
    raise ValueError(f"no tile for {n}")


def _split(a):
    hi = a.astype(BF16)
    lo = (a - hi.astype(F32)).astype(BF16)
    return hi, lo


def _dot(a, b):
    return jnp.dot(a, b, preferred_element_type=F32)


def _dot3(a_hi, a_lo, b_hi, b_lo):
    return _dot(a_hi, b_hi) + (_dot(a_lo, b_hi) + _dot(a_hi, b_lo))


def _silu(x):
    return x * (1.0 / (1.0 + jnp.exp(-x)))


def _modnorm(x, nw, sh, sc):
    ms = jnp.mean(x * x, axis=-1, keepdims=True)
    return (x * lax.rsqrt(ms + EPS) * nw) * (1.0 + sc) + sh


def _matvec_kernel(x_ref, w_ref, b_ref, o_ref):
    x = x_ref[...]
    o_ref[...] = _dot(_silu(x).astype(BF16), w_ref[...].astype(BF16)) + b_ref[...]


def adaln_vectors(vecs, w_all, b_all, layer):
    nl, k, n = w_all.shape
    tn = 1024
    return pl.pallas_call(
        _matvec_kernel,
        grid=(n // tn,),
        in_specs=[pl.BlockSpec((8, k), lambda j: (0, 0)),
                  pl.BlockSpec((None, k, tn), lambda j: (layer, 0, j)),
                  pl.BlockSpec((None, 1, tn), lambda j: (layer, 0, j))],
        out_specs=pl.BlockSpec((8, tn), lambda j: (0, j)),
        out_shape=jax.ShapeDtypeStruct((8, n), F32),
        compiler_params=_cparams("arbitrary"),
        name="adaln_vectors",
    )(vecs, w_all, b_all.reshape(nl, 1, n))


def _normmm_kernel(x_ref, nw_ref, sh_ref, sc_ref, w_ref, b_ref, o_ref, xn_ref):
    @pl.when(pl.program_id(1) == 0)
    def _():
        xn_ref[...] = _modnorm(x_ref[...].astype(F32), nw_ref[...], sh_ref[...], sc_ref[...]).astype(BF16)

    o_ref[...] = (_dot(xn_ref[...], w_ref[...]) + b_ref[...]).astype(o_ref.dtype)


def normmm(x, nw, sh, sc, w, b, *, tm, tn, out_dtype, x_col=0):
    m = x.shape[0]
    k, n = w.shape
    return pl.pallas_call(
        _normmm_kernel,
        grid=(m // tm, n // tn),
        in_specs=[pl.BlockSpec((tm, k), lambda i, j: (i, x_col)),
                  pl.BlockSpec((1, k), lambda i, j: (0, 0)),
                  pl.BlockSpec((1, k), lambda i, j: (0, 0)),
                  pl.BlockSpec((1, k), lambda i, j: (0, 0)),
                  pl.BlockSpec((k, tn), lambda i, j: (0, j)),
                  pl.BlockSpec((1, tn), lambda i, j: (0, j))],
        out_specs=pl.BlockSpec((tm, tn), lambda i, j: (i, j)),
        out_shape=jax.ShapeDtypeStruct((m, n), out_dtype),
        scratch_shapes=[pltpu.VMEM((tm, k), BF16)],
        compiler_params=_cparams("parallel", "arbitrary"),
        name="normmm",
    )(x, nw.reshape(1, k), sh.reshape(1, k), sc.reshape(1, k), w, b.reshape(1, n))


def _normmm_conv_kernel(x_ref, xprev_ref, xnext_ref, nw_ref, sh_ref, sc_ref, w_ref, b_ref, cw_ref, cb_ref,
                        o_ref, xn_ref, xh_ref):
    i = pl.program_id(0)

    @pl.when(pl.program_id(1) == 0)
    def _():
        nw, sh, sc = nw_ref[...], sh_ref[...], sc_ref[...]
        xn_ref[...] = _modnorm(x_ref[...], nw, sh, sc).astype(BF16)
        xh_ref[0:8, :] = _modnorm(xprev_ref[...], nw, sh, sc).astype(BF16)
        xh_ref[8:16, :] = _modnorm(xnext_ref[...], nw, sh, sc).astype(BF16)

    w = w_ref[...]
    b = b_ref[...]
    y = _dot(xn_ref[...], w) + b
    yh = _dot(xh_ref[...], w) + b
    prev_row = jnp.where(i == 0, 0.0, yh[7:8, :])
    next_row = jnp.where(i == pl.num_programs(0) - 1, 0.0, yh[8:9, :])
    yp, yn = _shift_rows(y, prev_row, next_row)
    o_ref[...] = yp * cw_ref[0:1, :] + y * cw_ref[1:2, :] + yn * cw_ref[2:3, :] + cb_ref[...]


def normmm_conv(x, nw, sh, sc, w, b, cw, cb, *, tm, tn):
    m, k = x.shape
    n = w.shape[1]
    nrb = m // 8
    tb = tm // 8
    vec = pl.BlockSpec((1, k), lambda i, j: (0, 0))
    return pl.pallas_call(
        _normmm_conv_kernel,
        grid=(m // tm, n // tn),
        in_specs=[pl.BlockSpec((tm, k), lambda i, j: (i, 0)),
                  pl.BlockSpec((8, k), lambda i, j: (jnp.maximum(i * tb - 1, 0), 0)),
                  pl.BlockSpec((8, k), lambda i, j: (jnp.minimum((i + 1) * tb, nrb - 1), 0)),
                  vec, vec, vec,
                  pl.BlockSpec((k, tn), lambda i, j: (0, j)),
                  pl.BlockSpec((1, tn), lambda i, j: (0, j)),
                  pl.BlockSpec((3, tn), lambda i, j: (0, j)),
                  pl.BlockSpec((1, tn), lambda i, j: (0, j))],
        out_specs=pl.BlockSpec((tm, tn), lambda i, j: (i, j)),
        out_shape=jax.ShapeDtypeStruct((m, n), F32),
        scratch_shapes=[pltpu.VMEM((tm, k), BF16), pltpu.VMEM((16, k), BF16)],
        compiler_params=_cparams("parallel", "arbitrary"),
        name="normmm_conv",
    )(x, x, x, nw.reshape(1, k), sh.reshape(1, k), sc.reshape(1, k), w, b.reshape(1, n), cw, cb.reshape(1, n))


def _qproj_kernel(x_ref, nw_ref, wa_ref, wb_ref, ta_ref, tb_ref, o_ref, xn_ref):
    @pl.when(pl.program_id(1) == 0)
    def _():
        x = x_ref[...]
        ms = jnp.mean(x * x, axis=-1, keepdims=True)
        xn_ref[...] = (x * lax.rsqrt(ms + EPS) * nw_ref[...]).astype(BF16)

    xn = xn_ref[...]
    o_ref[...] = (_dot(xn, wa_ref[...]) * ta_ref[...] + _dot(xn, wb_ref[...]) * tb_ref[...]).astype(o_ref.dtype)


def qproj(p, nw, wa, wb, ta, tb, *, tm):
    m = ta.shape[0]
    k = MLA_Q_RANK
    hp = MLA_HEAD_PAD
    return pl.pallas_call(
        _qproj_kernel,
        grid=(m // tm, MLA_HEADS),
        in_specs=[pl.BlockSpec((tm, k), lambda i, j: (i, P_CQ // MLA_Q_RANK)),
                  pl.BlockSpec((1, k), lambda i, j: (0, 0)),
                  pl.BlockSpec((k, hp), lambda i, j: (0, j)),
                  pl.BlockSpec((k, hp), lambda i, j: (0, j)),
                  pl.BlockSpec((tm, hp), lambda i, j: (i, 0)),
                  pl.BlockSpec((tm, hp), lambda i, j: (i, 0))],
        out_specs=pl.BlockSpec((tm, hp), lambda i, j: (i, j)),
        out_shape=jax.ShapeDtypeStruct((m, MLA_HEADS * hp), BF16),
        scratch_shapes=[pltpu.VMEM((tm, k), BF16)],
        compiler_params=_cparams("parallel", "arbitrary"),
        name="mla_qproj",
    )(p, nw.reshape(1, k), wa, wb, ta, tb)


def _kv_assemble_kernel(kv_ref, kr_ref, tc_ref, ts_ref, k_ref, v_ref, *, rope):
    krr = kr_ref[...]
    kr = krr[:, :MLA_ROPE]
    if rope:
        kr = kr * tc_ref[...][:, :MLA_ROPE] + krr[:, MLA_ROPE:] * ts_ref[...][:, :MLA_ROPE]
    tail = jnp.concatenate([kr, jnp.zeros_like(kr)], axis=-1).astype(BF16)
    for h in range(MLA_HEADS):
        base = h * (MLA_NOPE + MLA_V)
        k_ref[:, h * MLA_HEAD_PAD:h * MLA_HEAD_PAD + MLA_NOPE] = kv_ref[:, base:base + MLA_NOPE]
        k_ref[:, h * MLA_HEAD_PAD + MLA_NOPE:(h + 1) * MLA_HEAD_PAD] = tail
        v_ref[:, h * MLA_V:(h + 1) * MLA_V] = kv_ref[:, base + MLA_NOPE:base + MLA_NOPE + MLA_V]


def kv_assemble(kvp, p, tc, ts, *, tm, rope):
    m = kvp.shape[0]
    return pl.pallas_call(
        functools.partial(_kv_assemble_kernel, rope=rope),
        grid=(m // tm,),
        in_specs=[pl.BlockSpec((tm, MLA_HEADS * (MLA_NOPE + MLA_V)), lambda i: (i, 0)),
                  pl.BlockSpec((tm, 128), lambda i: (i, P_KR // 128)),
                  pl.BlockSpec((tm, 128), lambda i: (i, 0)),
                  pl.BlockSpec((tm, 128), lambda i: (i, 0))],
        out_specs=[pl.BlockSpec((tm, MLA_HEADS * MLA_HEAD_PAD), lambda i: (i, 0)),
                   pl.BlockSpec((tm, MLA_HEADS * MLA_V), lambda i: (i, 0))],
        out_shape=[jax.ShapeDtypeStruct((m, MLA_HEADS * MLA_HEAD_PAD), BF16),
                   jax.ShapeDtypeStruct((m, MLA_HEADS * MLA_V), BF16)],
        compiler_params=_cparams("parallel"),
        name="mla_kv_assemble",
    )(kvp, p, tc, ts)


def _attn_kernel(q_ref, k_ref, v_ref, o_ref, s0_ref, s1_ref, m_ref, acc_ref, *, nk):
    j = pl.program_id(2)

    def qk(s_ref):
        s_ref[...] = lax.dot_general(q_ref[...], k_ref[...], (((1,), (1,)), ((), ())), preferred_element_type=F32)

    def softmax_pv(s_ref):
        s = s_ref[...]
        m_prev = m_ref[...]
        m_new = jnp.maximum(m_prev, jnp.max(s, axis=-1, keepdims=True))
        alpha = jnp.exp2(m_prev - m_new)
        p = jnp.exp2(s - m_new).astype(BF16)
        v = v_ref[...]
        lane = lax.broadcasted_iota(jnp.int32, v.shape, 1)
        ones_col = jnp.where(lane == 0, 1.0, 0.0).astype(BF16)
        v_ext = jnp.concatenate([v, ones_col], axis=1)
        acc_ref[...] = alpha * acc_ref[...] + _dot(p, v_ext)
        m_ref[...] = m_new

    @pl.when(j == 0)
    def _():
        m_ref[...] = jnp.full(m_ref.shape, -jnp.inf, F32)
        acc_ref[...] = jnp.zeros(acc_ref.shape, F32)
        qk(s0_ref)

    mid = jnp.logical_and(j > 0, j < nk)

    @pl.when(jnp.logical_and(mid, j % 2 == 1))
    def _():
        softmax_pv(s0_ref)
        qk(s1_ref)

    @pl.when(jnp.logical_and(mid, j % 2 == 0))
    def _():
        softmax_pv(s1_ref)
        qk(s0_ref)

    @pl.when(j == nk)
    def _():
        softmax_pv(s1_ref if (nk - 1) % 2 else s0_ref)
        acc = acc_ref[...]
        o_ref[...] = (acc[:, :MLA_V] / acc[:, MLA_V:MLA_V + 1]).astype(o_ref.dtype)


def attention(q, k, v, *, tq, tk):
    lq = q.shape[0]
    s = k.shape[0]
    nk = s // tk
    return pl.pallas_call(
        functools.partial(_attn_kernel, nk=nk),
        grid=(MLA_HEADS, lq // tq, nk + 1),
        in_specs=[pl.BlockSpec((tq, MLA_HEAD_PAD), lambda h, i, j: (i, h)),
                  pl.BlockSpec((tk, MLA_HEAD_PAD), lambda h, i, j: (jnp.minimum(j, nk - 1), h)),
                  pl.BlockSpec((tk, MLA_V), lambda h, i, j: (jnp.maximum(j - 1, 0), h))],
        out_specs=pl.BlockSpec((tq, MLA_V), lambda h, i, j: (i, h)),
        out_shape=jax.ShapeDtypeStruct((lq, MLA_HEADS * MLA_V), BF16),
        scratch_shapes=[pltpu.VMEM((tq, tk), F32), pltpu.VMEM((tq, tk), F32), pltpu.VMEM((tq, 1), F32),
                        pltpu.VMEM((tq, 2 * MLA_V), F32)],
        compiler_params=_cparams("parallel", "parallel", "arbitrary"),
        name="mla_attention",
    )(q, k, v)


def _shift_rows(x, prev_row, next_row):
    tm = x.shape[0]
    rows = lax.broadcasted_iota(jnp.int32, x.shape, 0)
    xp = jnp.where(rows == 0, prev_row, pltpu.roll(x, 1, 0))
    xn = jnp.where(rows == tm - 1, next_row, pltpu.roll(x, tm - 1, 0))
    return xp, xn


def _dwconv_kernel(x_ref, xprev_ref, xnext_ref, w_ref, b_ref, o_ref, *, act):
    i = pl.program_id(0)
    x = x_ref[...]
    prev_row = jnp.where(i == 0, 0.0, xprev_ref[7:8, :])
    next_row = jnp.where(i == pl.num_programs(0) - 1, 0.0, xnext_ref[0:1, :])
    xp, xn = _shift_rows(x, prev_row, next_row)
    y = xp * w_ref[0:1, :] + x * w_ref[1:2, :] + xn * w_ref[2:3, :] + b_ref[...]
    if act:
        y = _silu(y)
    o_ref[...] = y


def dwconv3(x, w, b, *, tm, tc, col0, width, act):
    m = x.shape[0]
    cb = col0 // tc
    nrb = m // 8
    tb = tm // 8
    return pl.pallas_call(
        functools.partial(_dwconv_kernel, act=act),
        grid=(m // tm, width // tc),
        in_specs=[pl.BlockSpec((tm, tc), lambda i, j: (i, cb + j)),
                  pl.BlockSpec((8, tc), lambda i, j: (jnp.maximum(i * tb - 1, 0), cb + j)),
                  pl.BlockSpec((8, tc), lambda i, j: (jnp.minimum((i + 1) * tb, nrb - 1), cb + j)),
                  pl.BlockSpec((3, tc), lambda i, j: (0, j)),
                  pl.BlockSpec((1, tc), lambda i, j: (0, j))],
        out_specs=pl.BlockSpec((tm, tc), lambda i, j: (i, j)),
        out_shape=jax.ShapeDtypeStruct((m, width), F32),
        compiler_params=_cparams("arbitrary", "arbitrary"),
        name="dwconv3",
    )(x, x, x, w, b.reshape(1, width))


def _softplus(x):
    return jnp.maximum(x, 0.0) + jnp.log(1.0 + jnp.exp(-jnp.abs(x)))


def _ssd_one_direction(d, xbc_ref, dt_ref, dtt_ref, bias_ref, biast_ref, a_ref, at_ref, tri_ref, o_ref, h_ref):
    q = SSD_CHUNK
    xbc = xbc_ref[...]
    dt = _softplus(dt_ref[0] + bias_ref[d])
    dtt = _softplus(dtt_ref[0, 0] + biast_ref[d])
    dta_hi, dta_lo = _split(dt * a_ref[d])
    dtat_hi, dtat_lo = _split(dtt * at_ref[d])
    tri = tri_ref[d]
    trit = tri_ref[1 - d]
    acum = _dot(tri, dta_hi) + _dot(tri, dta_lo)
    acumt = _dot(dtat_hi, trit) + _dot(dtat_lo, trit)
    total = acum[q - 1:q, :] if d == 0 else acum[0:1, :]
    to_end = jnp.exp(total - acum)
    from_start = jnp.exp(acum)
    chunk_decay = jnp.exp(total)
    mask = tri > 0.5

    for g in range(SSD_GROUPS):
        bm = xbc[:, SSD_INNER + g * SSD_STATE:SSD_INNER + (g + 1) * SSD_STATE]
        cm = xbc[:, SSD_INNER + (SSD_GROUPS + g) * SSD_STATE:SSD_INNER + (SSD_GROUPS + g + 1) * SSD_STATE]
        bm_b = bm.astype(BF16)
        cm_b = cm.astype(BF16)
        cb = lax.dot_general(cm_b, bm_b, (((1,), (1,)), ((), ())), preferred_element_type=F32)
        bt_b = bm.T.astype(BF16)
        for r in range(SSD_HPG):
            h = g * SSD_HPG + r
            xs = xbc[:, h * SSD_HEAD_DIM:(h + 1) * SSD_HEAD_DIM]
            xdt = xs * dt[:, h:h + 1]
            seg = acum[:, h:h + 1] - acumt[h:h + 1, :]
            decay = jnp.exp(jnp.where(mask, seg, -jnp.inf))
            y_diag = _dot((cb * decay).astype(BF16), xdt.astype(BF16))
            h_prev = h_ref[d * SSD_HEADS + h]
            y_off = _dot(cm_b, h_prev.astype(BF16)) * from_start[:, h:h + 1]
            states = _dot(bt_b, (xdt * to_end[:, h:h + 1]).astype(BF16))
            h_ref[d * SSD_HEADS + h] = h_prev * chunk_decay[:, h:h + 1] + states
            o_ref[:, h * SSD_HEAD_DIM:(h + 1) * SSD_HEAD_DIM] = y_diag + y_off


def _ssd_kernel(xf_ref, xb_ref, dtf_ref, dtb_ref, dttf_ref, dttb_ref, bias_ref, biast_ref, a_ref, at_ref, tri_ref,
                of_ref, ob_ref, h_ref):
    @pl.when(pl.program_id(0) == 0)
    def _():
        h_ref[...] = jnp.zeros(h_ref.shape, F32)

    _ssd_one_direction(0, xf_ref, dtf_ref, dttf_ref, bias_ref, biast_ref, a_ref, at_ref, tri_ref, of_ref, h_ref)
    _ssd_one_direction(1, xb_ref, dtb_ref, dttb_ref, bias_ref, biast_ref, a_ref, at_ref, tri_ref, ob_ref, h_ref)


def ssd_scan(xbc, dt2, dt2t, bias2, bias2t, a2, a2t, tri2, *, n_lat_chunks):
    nc = n_lat_chunks
    ncx = CTX_LEN // SSD_CHUNK
    tot = nc + ncx
    q = SSD_CHUNK

    def cf(s):
        return lax.rem(s + nc, tot)

    def cbk(s):
        return tot - 1 - s

    def full(shape):
        return pl.BlockSpec(shape, lambda s: (0,) * len(shape))

    return pl.pallas_call(
        _ssd_kernel,
        grid=(tot,),
        in_specs=[pl.BlockSpec((q, SSD_XBC), lambda s: (cf(s), 0)),
                  pl.BlockSpec((q, SSD_XBC), lambda s: (cbk(s), 0)),
                  pl.BlockSpec((1, q, SSD_HEADS), lambda s: (0, cf(s), 0)),
                  pl.BlockSpec((1, q, SSD_HEADS), lambda s: (1, cbk(s), 0)),
                  pl.BlockSpec((1, 1, SSD_HEADS, q), lambda s: (0, cf(s), 0, 0)),
                  pl.BlockSpec((1, 1, SSD_HEADS, q), lambda s: (1, cbk(s), 0, 0)),
                  full((2, 1, SSD_HEADS)), full((2, SSD_HEADS, 1)), full((2, 1, SSD_HEADS)),
                  full((2, SSD_HEADS, 1)), full((2, q, q))],
        out_specs=[pl.BlockSpec((q, SSD_INNER), lambda s: (jnp.where(cf(s) >= nc, 0, cf(s)), 0)),
                   pl.BlockSpec((q, SSD_INNER), lambda s: (jnp.where(cbk(s) >= nc, nc - 1, cbk(s)), 0))],
        out_shape=[jax.ShapeDtypeStruct((nc * q, SSD_INNER), F32), jax.ShapeDtypeStruct((nc * q, SSD_INNER), F32)],
        scratch_shapes=[pltpu.VMEM((2 * SSD_HEADS, SSD_STATE, SSD_HEAD_DIM), F32)],
        compiler_params=_cparams("arbitrary"),
        name="ssd_scan",
    )(xbc, xbc, dt2, dt2, dt2t, dt2t, bias2, bias2t, a2, a2t, tri2)


def _merge_kernel(o_ref, yf_ref, yb_ref, xs_ref, z_ref, dsk_ref, nw_ref, w_ref, x_ref, g_ref, out_ref, a_ref):
    @pl.when(pl.program_id(1) == 0)
    def _():
        a_ref[:, :MLA_HEADS * MLA_V] = o_ref[...]
        y = yf_ref[...] + yb_ref[...] + dsk_ref[...] * xs_ref[...]
        gy = y * _silu(z_ref[...])
        gw = SSD_INNER // SSD_GROUPS
        for g in range(SSD_GROUPS):
            part = gy[:, g * gw:(g + 1) * gw]
            ms = jnp.mean(part * part, axis=-1, keepdims=True)
            a_ref[:, MLA_HEADS * MLA_V + g * gw:MLA_HEADS * MLA_V + (g + 1) * gw] = (
                part * lax.rsqrt(ms + EPS) * nw_ref[:, g * gw:(g + 1) * gw]).astype(BF16)

    out_ref[...] = x_ref[...] + g_ref[...] * _dot(a_ref[...], w_ref[...])


def merge_out(o_att, yf, yb, xbc, p, dsk, nw, w_o, x, gate, *, tm, tn):
    m, n = x.shape
    kw = w_o.shape[0]
    return pl.pallas_call(
        _merge_kernel,
        grid=(m // tm, n // tn),
        in_specs=[pl.BlockSpec((tm, MLA_HEADS * MLA_V), lambda i, j: (i, 0)),
                  pl.BlockSpec((tm, SSD_INNER), lambda i, j: (i, 0)),
                  pl.BlockSpec((tm, SSD_INNER), lambda i, j: (i, 0)),
                  pl.BlockSpec((tm, SSD_INNER), lambda i, j: (i, 0)),
                  pl.BlockSpec((tm, SSD_INNER), lambda i, j: (i, 0)),
                  pl.BlockSpec((1, SSD_INNER), lambda i, j: (0, 0)),
                  pl.BlockSpec((1, SSD_INNER), lambda i, j: (0, 0)),
                  pl.BlockSpec((kw, tn), lambda i, j: (0, j)),
                  pl.BlockSpec((tm, tn), lambda i, j: (i, j)),
                  pl.BlockSpec((1, tn), lambda i, j: (0, j))],
        out_specs=pl.BlockSpec((tm, tn), lambda i, j: (i, j)),
        out_shape=jax.ShapeDtypeStruct((m, n), F32),
        scratch_shapes=[pltpu.VMEM((tm, kw), BF16)],
        compiler_params=_cparams("parallel", "arbitrary"),
        name="merge_out",
    )(o_att, yf, yb, xbc, p, dsk, nw, w_o, x, gate)


def _mmres_kernel(a_ref, w_ref, b_ref, x_ref, g_ref, o_ref):
    o_ref[...] = x_ref[...] + g_ref[...] * (_dot(a_ref[...].astype(BF16), w_ref[...]) + b_ref[...])


def mm_res(a, w, b, x, gate, *, tm, tn):
    m, k = a.shape
    n = w.shape[1]
    return pl.pallas_call(
        _mmres_kernel,
        grid=(m // tm, n // tn),
        in_specs=[pl.BlockSpec((tm, k), lambda i, j: (i, 0)),
                  pl.BlockSpec((k, tn), lambda i, j: (0, j)),
                  pl.BlockSpec((1, tn), lambda i, j: (0, j)),
                  pl.BlockSpec((tm, tn), lambda i, j: (i, j)),
                  pl.BlockSpec((1, tn), lambda i, j: (0, j))],
        out_specs=pl.BlockSpec((tm, tn), lambda i, j: (i, j)),
        out_shape=jax.ShapeDtypeStruct((m, n), F32),
        compiler_params=_cparams("parallel", "arbitrary"),
        name="mm_res",
    )(a, w, b.reshape(1, n), x, gate)


def _ffn_kernel(x_ref, nw_ref, sh_ref, sc_ref, g_ref, w1_ref, w3_ref, w2_ref, fw_ref, o_ref, xn_ref, acc_ref,
                *, final_norm):
    f = pl.program_id(1)

    @pl.when(f == 0)
    def _():
        xn_ref[...] = _modnorm(x_ref[...], nw_ref[...], sh_ref[...], sc_ref[...]).astype(BF16)
        acc_ref[...] = jnp.zeros(acc_ref.shape, F32)

    xn = xn_ref[...]
    a = _dot(xn, w1_ref[...])
    b = _dot(xn, w3_ref[...])
    acc_ref[...] += _dot((_silu(a) * b).astype(BF16), w2_ref[...])

    @pl.when(f == pl.num_programs(1) - 1)
    def _():
        y = x_ref[...] + g_ref[...] * acc_ref[...]
        if final_norm:
            ms = jnp.mean(y * y, axis=-1, keepdims=True)
            y = y * lax.rsqrt(ms + EPS) * fw_ref[...]
        o_ref[...] = y


def ffn(x, nw, sh, sc, gate, w1, w3, w2, fw, *, layer, tm, tf, final_norm):
    m, dm = x.shape
    dff = w1.shape[2]
    vec = pl.BlockSpec((1, dm), lambda i, f: (0, 0))
    return pl.pallas_call(
        functools.partial(_ffn_kernel, final_norm=final_norm),
        grid=(m // tm, dff // tf),
        in_specs=[pl.BlockSpec((tm, dm), lambda i, f: (i, 0)), vec, vec, vec, vec,
                  pl.BlockSpec((None, dm, tf), lambda i, f: (layer, 0, f)),
                  pl.BlockSpec((None, dm, tf), lambda i, f: (layer, 0, f)),
                  pl.BlockSpec((None, tf, dm), lambda i, f: (layer, f, 0)),
                  vec],
        out_specs=pl.BlockSpec((tm, dm), lambda i, f: (i, 0)),
        out_shape=jax.ShapeDtypeStruct((m, dm), F32),
        scratch_shapes=[pltpu.VMEM((tm, dm), BF16), pltpu.VMEM((tm, dm), F32)],
        compiler_params=_cparams("parallel", "arbitrary"),
        name="ffn",
    )(x, nw.reshape(1, dm), sh, sc, gate, w1, w3, w2, fw.reshape(1, dm))


def _filter_kernel(fvec_ref, w1_ref, b1_ref, wm_ref, bm_ref, freq_ref, wo_ref, delta_ref, k_ref, abs_ref,
                   *, seq_len, tr):
    i = pl.program_id(0)
    n = i * tr + lax.broadcasted_iota(jnp.int32, (tr, 1), 0)
    lag = jnp.where(n < seq_len, n, 2 * seq_len - n).astype(F32)
    t = lag / float(seq_len - 1)
    ang = lag * (2.0 * math.pi / seq_len)
    lane = lax.broadcasted_iota(jnp.int32, (tr, 128), 1)
    bands = (HY_EMB - 1) // 2
    arg = ang * fvec_ref[...]
    emb = jnp.where(lane == 0, t, jnp.where(lane <= bands, jnp.cos(arg), jnp.where(lane <= 2 * bands, -jnp.sin(arg), 0.0)))
    fr = freq_ref[...]
    hid = jnp.sin(fr * (_dot(emb.astype(BF16), w1_ref[...]) + b1_ref[...]))
    for j in range(wm_ref.shape[0]):
        hid = jnp.sin(fr * (_dot(hid.astype(BF16), wm_ref[j]) + bm_ref[j]))
    k = _dot(hid.astype(BF16), wo_ref[0]) * jnp.exp(-t * delta_ref[...])
    k = jnp.where(n == seq_len, 0.0, k)
    k_ref[...] = k

    @pl.when(i == 0)
    def _():
        abs_ref[...] = jnp.zeros(abs_ref.shape, F32)

    abs_ref[...] += jnp.sum(jnp.abs(k), axis=0, keepdims=True)


def hyena_filter(fvec, w1, b1, wm, bm, freq, wo, delta, *, seq_len, tr):
    dd = delta.shape[1]
    half = seq_len // tr
    full = lambda shp: pl.BlockSpec(shp, lambda i: (0,) * len(shp))
    return pl.pallas_call(
        functools.partial(_filter_kernel, seq_len=seq_len, tr=tr),
        grid=(2 * seq_len // tr,),
        in_specs=[full((1, 128)), full(w1.shape), full(b1.shape), full(wm.shape), full(bm.shape), full(freq.shape),
                  pl.BlockSpec((1, HY_HID, dd), lambda i: (i // half, 0, 0)),
                  full((1, dd))],
        out_specs=[pl.BlockSpec((tr, dd), lambda i: (i, 0)), pl.BlockSpec((1, dd), lambda i: (0, 0))],
        out_shape=[jax.ShapeDtypeStruct((2 * seq_len, dd), F32), jax.ShapeDtypeStruct((1, dd), F32)],
        compiler_params=_cparams("arbitrary"),
        name="hyena_filter",
    )(fvec, w1, b1, wm, bm, freq, wo, delta)


ROW_GROUP = 8


def _fft1_kernel(f_ref, x_ref, o_ref, xs_ref):
    for r in range(ROW_GROUP):
        xs_ref[...] = x_ref[:, r, :]
        o_ref[:, r, :] = _dot(f_ref[...], xs_ref[...].astype(BF16))


def _fft1_gate_kernel(f_ref, x_ref, v_ref, gate_ref, fb_ref, o_ref, xs_ref):
    for r in range(ROW_GROUP):
        xs_ref[...] = x_ref[:, r, :]
        o_ref[:, r, :] = _dot(f_ref[...], xs_ref[...].astype(BF16))
    o_ref[...] = gate_ref[...] * (o_ref[...] + fb_ref[...] * v_ref[...])


def fft_stage1(fh, x3, *, d, tn, col_ofs=0, gate_args=None):
    mrows, k = fh.shape
    n2 = x3.shape[1]
    nd = d // tn
    fspec = pl.BlockSpec((mrows, k), lambda i, j: (0, 0))
    xspec = pl.BlockSpec((k, ROW_GROUP, tn), lambda i, j: (0, i, col_ofs * nd + j))
    ospec = pl.BlockSpec((mrows, ROW_GROUP, tn), lambda i, j: (0, i, j))
    common = dict(grid=(n2 // ROW_GROUP, nd), out_specs=ospec,
                  out_shape=jax.ShapeDtypeStruct((mrows, n2, d), F32),
                  scratch_shapes=[pltpu.VMEM((k, tn), F32)],
                  compiler_params=_cparams("parallel", "arbitrary"))
    if gate_args is None:
        return pl.pallas_call(_fft1_kernel, in_specs=[fspec, xspec], name="fft_stage1", **common)(fh, x3)
    v3, vofs, gate3, gofs, fb = gate_args
    return pl.pallas_call(
        _fft1_gate_kernel,
        in_specs=[fspec, xspec,
                  pl.BlockSpec((mrows, ROW_GROUP, tn), lambda i, j: (0, i, vofs * nd + j)),
                  pl.BlockSpec((mrows, ROW_GROUP, tn), lambda i, j: (0, i, gofs * nd + j)),
                  pl.BlockSpec((1, tn), lambda i, j: (0, j))],
        name="fft_stage1_inv", **common)(fh, x3, v3, gate3, fb)


def _fftmid_kernel(a_ref, mf_ref, kf_ref, mi_ref, o_ref):
    half = FFT_N2
    for s in range(SLAB_TILE):
        y = _dot(mf_ref[s], a_ref[s].astype(BF16))
        kf = kf_ref[s]
        yr, yi = y[:half], y[half:]
        kr, ki = kf[:half], kf[half:]
        pr = yr * kr - yi * ki
        pi = yr * ki + yi * kr
        o_ref[s] = _dot(mi_ref[s], jnp.concatenate([pr, pi], axis=0).astype(BF16))


def fft_mid(a3, mf, kf3, mi, *, dc):
    nsp, rows, d = a3.shape
    blk = pl.BlockSpec((SLAB_TILE, rows, dc), lambda s, j: (s, 0, j))
    mat = pl.BlockSpec((SLAB_TILE, rows, rows), lambda s, j: (s, 0, 0))
    return pl.pallas_call(
        _fftmid_kernel,
        grid=(nsp // SLAB_TILE, d // dc),
        in_specs=[blk, mat, blk, mat],
        out_specs=blk,
        out_shape=jax.ShapeDtypeStruct((nsp, rows, d), F32),
        compiler_params=_cparams("parallel", "arbitrary"),
        name="fft_mid",
    )(a3, mf, kf3, mi)


def _fftspec_kernel(a_ref, mf_ref, scale_ref, o_ref):
    for s in range(SLAB_TILE):
        o_ref[s] = _dot(mf_ref[s], a_ref[s].astype(BF16)) * scale_ref[...]


def fft_spectrum(a3, mf, scale, *, dc):
    nsp, rows, d = a3.shape
    blk = pl.BlockSpec((SLAB_TILE, rows, dc), lambda s, j: (s, 0, j))
    mat = pl.BlockSpec((SLAB_TILE, rows, rows), lambda s, j: (s, 0, 0))
    return pl.pallas_call(
        _fftspec_kernel,
        grid=(nsp // SLAB_TILE, d // dc),
        in_specs=[blk, mat, pl.BlockSpec((1, dc), lambda s, j: (0, j))],
        out_specs=blk,
        out_shape=jax.ShapeDtypeStruct((nsp, rows, d), F32),
        compiler_params=_cparams("parallel", "arbitrary"),
        name="fft_spectrum",
    )(a3, mf, scale)


def _fft_tables(seq_len):
    n = 2 * seq_len
    n2 = FFT_N2
    n1 = n // n2
    ns = n1 // 2 + 1
    nsp = -(-ns // SLAB_TILE) * SLAB_TILE
    k1 = jnp.arange(nsp, dtype=jnp.int32)
    valid = (k1 < ns)
    m1 = jnp.arange(n1, dtype=jnp.int32)
    ang1 = (2.0 * math.pi / n1) * lax.rem(k1[:, None] * m1[None, :], n1).astype(F32)
    vf = valid[:, None].astype(F32)
    f1 = jnp.stack([jnp.cos(ang1) * vf, -jnp.sin(ang1) * vf], axis=1).reshape(2 * nsp, n1)
    wgt = jnp.where((k1 == 0) | (k1 == n1 // 2), 1.0, 2.0) * valid.astype(F32) / n
    g1 = jnp.stack([jnp.cos(ang1) * wgt[:, None], -jnp.sin(ang1) * wgt[:, None]], axis=1)
    g1 = g1.reshape(2 * nsp, n1).T[: n1 // 2]
    k2 = jnp.arange(n2, dtype=jnp.int32)
    m2 = jnp.arange(n2, dtype=jnp.int32)
    f = k1[:, None, None] + n1 * k2[None, :, None]
    ang2 = (2.0 * math.pi / n) * lax.rem(f * m2[None, None, :], n).astype(F32)
    vm = valid[:, None, None].astype(F32)
    c2, s2 = jnp.cos(ang2) * vm, jnp.sin(ang2) * vm
    mf = jnp.concatenate([jnp.concatenate([c2, s2], axis=2), jnp.concatenate([-s2, c2], axis=2)], axis=1)
    c2t, s2t = jnp.swapaxes(c2, 1, 2), jnp.swapaxes(s2, 1, 2)
    mi = jnp.concatenate([jnp.concatenate([c2t, -s2t], axis=2), jnp.concatenate([s2t, c2t], axis=2)], axis=1)
    return dict(n1=n1, nsp=nsp, f1=f1.astype(BF16), g1=g1.astype(BF16), mf=mf.astype(BF16), mi=mi.astype(BF16))


def long_conv_gate(v, v_cols, gate, gate_cols, fbias, kf3, tabs, *, seq_len, d):
    n1, nsp = tabs["n1"], tabs["nsp"]
    half = n1 // 2
    v3 = v.reshape(half, FFT_N2, v.shape[1])
    g3 = gate.reshape(half, FFT_N2, gate.shape[1])
    a = fft_stage1(tabs["f1"][:, :half], v3, d=d, tn=512, col_ofs=v_cols)
    a3 = a.reshape(nsp, 2 * FFT_N2, d)
    b3 = fft_mid(a3, tabs["mf"], kf3, tabs["mi"], dc=256)
    b = b3.reshape(2 * nsp, FFT_N2, d)
    out = fft_stage1(tabs["g1"], b, d=d, tn=512, gate_args=(v3, v_cols, g3, gate_cols, fbias))
    return out.reshape(seq_len, d)


def filter_spectrum(k, k_cols, kabs, tabs, *, d):
    n1, nsp = tabs["n1"], tabs["nsp"]
    a = fft_stage1(tabs["f1"], k.reshape(n1, FFT_N2, k.shape[1]), d=d, tn=512, col_ofs=k_cols)
    return fft_spectrum(a.reshape(nsp, 2 * FFT_N2, d), tabs["mf"], 1.0 / kabs, dc=256)


def _rope_tables(seq_len, scale):
    n_freq = MLA_ROPE // 4
    rows = seq_len // GRID_W
    row = jnp.repeat(jnp.arange(rows, dtype=F32), GRID_W)
    col = jnp.tile(jnp.arange(GRID_W, dtype=F32), rows)
    inv = ROPE_THETA ** (-jnp.arange(n_freq, dtype=F32) / n_freq)
    ang = jnp.stack([row[:, None] * inv, col[:, None] * inv], axis=1)
    cos = jnp.broadcast_to(jnp.cos(ang)[:, :, None, :], (seq_len, 2, 2, n_freq)).reshape(seq_len, MLA_ROPE)
    sin = jnp.broadcast_to(jnp.sin(ang)[:, :, None, :], (seq_len, 2, 2, n_freq)).reshape(seq_len, MLA_ROPE)
    one = jnp.ones((seq_len, MLA_NOPE), F32)
    zero = jnp.zeros((seq_len, MLA_NOPE), F32)
    z64 = jnp.zeros((seq_len, MLA_HEAD_PAD - MLA_NOPE - MLA_ROPE), F32)
    ta = jnp.concatenate([one, cos, z64], axis=1) * scale
    tb = jnp.concatenate([zero, sin, z64], axis=1) * scale
    tc = jnp.concatenate([cos, z64], axis=1)
    ts = jnp.concatenate([sin, z64], axis=1)
    return ta, tb, tc, ts


def _rope_swap_cols(w):
    n_freq = MLA_ROPE // 4
    w4 = w.reshape(w.shape[0], 2, 2, n_freq)
    return jnp.stack([-w4[:, :, 1], w4[:, :, 0]], axis=2).reshape(w.shape[0], MLA_ROPE)


def _even_layer(x, ctx, mods, norm_mix_w, w_in, conv_w, conv_b, dt_bias, a_log, d_skip, ssd_norm_w,
                q_norm_w, w_uq, kv_norm_w, w_ukv, w_o):
    seq_len, d = x.shape
    sh1, sc1, g1 = (mods[0:1, i * d:(i + 1) * d] for i in range(3))
    csh1, csc1 = mods[1:2, 0:d], mods[1:2, d:2 * d]

    o1 = Q_SIDE + SSD_XBC
    o2 = o1 + SSD_DT
    o3 = o2 + MLA_KV_RANK
    w_kr = w_in[:, o3:]
    w_ext = jnp.concatenate([w_in[:, :Q_SIDE], w_in[:, Q_SIDE:o1], w_in[:, o2:o3], w_kr, _rope_swap_cols(w_kr),
                             w_in[:, o1:o2], jnp.zeros((d, P_COLS - P_DT - SSD_DT), F32)], axis=1).astype(BF16)
    zb = jnp.zeros((P_COLS,), F32)
    p_lat = normmm(x, norm_mix_w, sh1, sc1, w_ext, zb, tm=512, tn=768, out_dtype=F32)
    p_ctx = normmm(ctx, norm_mix_w, csh1, csc1, w_ext, zb, tm=CTX_LEN, tn=768, out_dtype=F32)

    scale = float(MLA_NOPE + MLA_ROPE) ** -0.5 * math.log2(math.e)
    ta, tb, tc, ts = _rope_tables(seq_len, scale)
    wq = w_uq.reshape(MLA_Q_RANK, MLA_HEADS, MLA_NOPE + MLA_ROPE)
    zpad = jnp.zeros((MLA_Q_RANK, MLA_HEADS, MLA_HEAD_PAD - MLA_NOPE - MLA_ROPE), F32)
    wa = jnp.concatenate([wq, zpad], axis=2).reshape(MLA_Q_RANK, -1).astype(BF16)
    wr = wq[:, :, MLA_NOPE:]
    n_freq = MLA_ROPE // 4
    wr4 = wr.reshape(MLA_Q_RANK, MLA_HEADS, 2, 2, n_freq)
    wsw = jnp.stack([-wr4[:, :, :, 1], wr4[:, :, :, 0]], axis=3).reshape(MLA_Q_RANK, MLA_HEADS, MLA_ROPE)
    wb = jnp.concatenate([jnp.zeros((MLA_Q_RANK, MLA_HEADS, MLA_NOPE), F32), wsw, zpad], axis=2)
    wb = wb.reshape(MLA_Q_RANK, -1).astype(BF16)
    q = qproj(p_lat, q_norm_w, wa, wb, ta, tb, tm=512)

    zk = jnp.zeros((MLA_KV_RANK,), F32)
    w_ukv_b = w_ukv.astype(BF16)
    zkb = jnp.zeros((w_ukv.shape[1],), F32)
    kv_lat = normmm(p_lat, kv_norm_w, zk, zk, w_ukv_b, zkb, tm=512, tn=512, out_dtype=BF16,
                    x_col=P_CKV // MLA_KV_RANK)
    kv_ctx = normmm(p_ctx, kv_norm_w, zk, zk, w_ukv_b, zkb, tm=CTX_LEN, tn=512, out_dtype=BF16,
                    x_col=P_CKV // MLA_KV_RANK)
    k_lat, v_lat = kv_assemble(kv_lat, p_lat, tc, ts, tm=512, rope=True)
    k_ctx, v_ctx = kv_assemble(kv_ctx, p_ctx, tc, ts, tm=CTX_LEN, rope=False)
    k_all = jnp.concatenate([k_lat, k_ctx], axis=0)
    v_all = jnp.concatenate([v_lat, v_ctx], axis=0)
    s_tot = seq_len + CTX_LEN
    o_att = attention(q, k_all, v_all, tq=_pick(seq_len, (2048, 1024, 512, 256)),
                      tk=_pick(s_tot, (1280, 1024, 768, 640, 512, 384, 256, 128)))

    xbc_lat = dwconv3(p_lat, conv_w, conv_b, tm=512, tc=SSD_XBC, col0=P_XBC, width=SSD_XBC, act=True)
    xbc_ctx = dwconv3(p_ctx, conv_w, conv_b, tm=CTX_LEN, tc=SSD_XBC, col0=P_XBC, width=SSD_XBC, act=True)
    xbc_all = jnp.concatenate([xbc_lat, xbc_ctx], axis=0)
    dt_all = jnp.concatenate([p_lat[:, P_DT:P_DT + SSD_DT], p_ctx[:, P_DT:P_DT + SSD_DT]], axis=0)
    nch = s_tot // SSD_CHUNK
    dt2 = dt_all.reshape(s_tot, 2, SSD_HEADS).transpose(1, 0, 2)
    dt2t = dt2.reshape(2, nch, SSD_CHUNK, SSD_HEADS).transpose(0, 1, 3, 2)
    bias2 = dt_bias.reshape(2, 1, SSD_HEADS)
    bias2t = dt_bias.reshape(2, SSD_HEADS, 1)
    a_neg = -jnp.exp(a_log.astype(F32))
    a2 = a_neg.reshape(2, 1, SSD_HEADS)
    a2t = a_neg.reshape(2, SSD_HEADS, 1)
    lower = jnp.tril(jnp.ones((SSD_CHUNK, SSD_CHUNK), F32))
    tri2 = jnp.stack([lower, lower.T]).astype(BF16)
    yf, yb = ssd_scan(xbc_all, dt2, dt2t, bias2, bias2t, a2, a2t, tri2, n_lat_chunks=seq_len // SSD_CHUNK)

    dsk = jnp.repeat(d_skip[0] + d_skip[1], SSD_HEAD_DIM).reshape(1, SSD_INNER)
    return merge_out(o_att, yf, yb, xbc_lat, p_lat, dsk, ssd_norm_w.reshape(1, SSD_INNER), w_o.astype(BF16), x, g1,
                     tm=512, tn=512)


def _odd_layer(x, mods, norm_mix_w, w_in, b_in, short_w, short_b, fw1, fb1, fw_mid, fb_mid, freq, fw_out,
               fbias, w_out, b_out):
    seq_len, d = x.shape
    sh1, sc1, g1 = (mods[0:1, i * d:(i + 1) * d] for i in range(3))
    pc = normmm_conv(x, norm_mix_w, sh1, sc1, w_in.astype(BF16), b_in, short_w, short_b, tm=512, tn=768)

    tabs = _fft_tables(seq_len)
    bands = (HY_EMB - 1) // 2
    fband = jnp.linspace(1e-4, bands - 1, bands, dtype=F32)
    fvec = jnp.concatenate([jnp.zeros((1,), F32), fband, fband, jnp.zeros((128 - HY_EMB,), F32)]).reshape(1, 128)
    w1p = jnp.concatenate([fw1.astype(F32), jnp.zeros((128 - HY_EMB, HY_HID), F32)], axis=0).astype(BF16)
    lo = math.log(HY_SLOW_DECAY) / HY_TARGET
    hi = math.log(HY_FAST_DECAY) / HY_TARGET
    delta = jnp.abs(jnp.linspace(lo, hi, d, dtype=F32)).reshape(1, d)
    n_ord = fw_out.shape[1]
    wo = jnp.transpose(fw_out, (2, 0, 1, 3)).reshape(2, HY_HID, n_ord * d).astype(BF16)
    k_all, kabs_all = hyena_filter(fvec, w1p, fb1.reshape(1, HY_HID), fw_mid.astype(BF16),
                                   fb_mid.reshape(-1, 1, HY_HID), freq.reshape(1, HY_HID), wo,
                                   jnp.tile(delta, (1, n_ord)), seq_len=seq_len, tr=256)
    y_cols = 2
    y = None
    for i in range(n_ord):
        kf3 = filter_spectrum(k_all, i, kabs_all[:, i * d:(i + 1) * d], tabs, d=d)
        if y is None:
            y = long_conv_gate(pc, y_cols, pc, i, fbias[i].reshape(1, d), kf3, tabs, seq_len=seq_len, d=d)
        else:
            y = long_conv_gate(y, 0, pc, i, fbias[i].reshape(1, d), kf3, tabs, seq_len=seq_len, d=d)
    y = y.reshape(seq_len, d)
    return mm_res(y, w_out.astype(BF16), b_out, x, g1, tm=512, tn=512)


def kernel(x, c, ctx, c_ctx, mod_w, mod_b, norm_mix_w, norm_ffn_w, ffn_w1, ffn_w3, ffn_w2, ev_w_in, ev_conv_w, ev_conv_b, ev_dt_bias, ev_a_log, ev_d_skip, ev_ssd_norm_w, ev_q_norm_w, ev_w_uq, ev_kv_norm_w, ev_w_ukv, ev_w_o, hy_w_in, hy_b_in, hy_short_w, hy_short_b, hy_fw1, hy_fb1, hy_fw_mid, hy_fb_mid, hy_freq, hy_fw_out, hy_fbias, hy_w_out, hy_b_out, final_norm_w):
    assert x.shape[0] == 1 and mod_w.shape[0] == 2
    xs = x[0]
    xc = ctx[0]
    d = xs.shape[1]
    vecs = jnp.concatenate([c.reshape(1, d), c_ctx.reshape(1, d), jnp.zeros((6, d), F32)], axis=0)
    depth = mod_w.shape[0]
    w1_b, w3_b, w2_b = ffn_w1.astype(BF16), ffn_w3.astype(BF16), ffn_w2.astype(BF16)
    for i in range(depth):
        mods = adaln_vectors(vecs, mod_w, mod_b, i)
        sh2, sc2, g2 = (mods[0:1, j * d:(j + 1) * d] for j in range(3, 6))
        if i % 2 == 0:
            e = i // 2
            xs = _even_layer(xs, xc, mods, norm_mix_w[i], ev_w_in[e], ev_conv_w[e], ev_conv_b[e], ev_dt_bias[e],
                             ev_a_log[e], ev_d_skip[e], ev_ssd_norm_w[e], ev_q_norm_w[e], ev_w_uq[e],
                             ev_kv_norm_w[e], ev_w_ukv[e], ev_w_o[e])
        else:
            o = i // 2
            xs = _odd_layer(xs, mods, norm_mix_w[i], hy_w_in[o], hy_b_in[o], hy_short_w[o], hy_short_b[o],
                            hy_fw1[o], hy_fb1[o], hy_fw_mid[o], hy_fb_mid[o], hy_freq[o], hy_fw_out[o],
                            hy_fbias[o], hy_w_out[o], hy_b_out[o])
        xs = ffn(xs, norm_ffn_w[i], sh2, sc2, g2, w1_b, w3_b, w2_b, final_norm_w, layer=i, tm=512, tf=512,
                 final_norm=(i == depth - 1))
    return xs[None]
```

```python
import functools
import math

import jax
import jax.numpy as jnp
from jax import lax
from jax.experimental import pallas as pl
from jax.experimental.pallas import tpu as pltpu

F32 = jnp.float32
BF16 = jnp.bfloat16

EPS = 1e-6
GRID_W = 64
CTX_LEN = 256
SSD_HEADS = 16
SSD_HEAD_DIM = 64
SSD_INNER = 1024
SSD_GROUPS = 2
SSD_HPG = 8
SSD_STATE = 128
SSD_CHUNK = 128
SSD_XBC = 1536
SSD_DT = 32
MLA_HEADS = 8
MLA_NOPE = 128
MLA_ROPE = 64
MLA_V = 128
MLA_Q_RANK = 512
MLA_KV_RANK = 512
ROPE_THETA = 10000.0
MLA_HEAD_PAD = 256
Q_SIDE = SSD_INNER + MLA_Q_RANK
HY_EMB = 33
HY_HID = 64
HY_FAST_DECAY = 0.3
HY_SLOW_DECAY = 1.5
HY_TARGET = 1e-2
FFT_N2 = 128
SLAB_TILE = 8

P_Z, P_CQ, P_XBC, P_CKV, P_KR, P_DT, P_COLS = 0, 1024, 1536, 3072, 3584, 3712, 3840

VMEM_LIMIT = 56 * 1024 * 1024


def _cparams(*sem):
    return pltpu.CompilerParams(dimension_semantics=sem, vmem_limit_bytes=VMEM_LIMIT)


def _pick(n, cands):
    for c in cands:
        if n % c == 0:
            return c
    raise ValueError(f"no tile for {n}")


def _split(a):
    hi = a.astype(BF16)
    lo = (a - hi.astype(F32)).astype(BF16)
    return hi, lo


def _dot(a, b):
    return jnp.dot(a, b, preferred_element_type=F32)


def _silu(x):
    return x * (1.0 / (1.0 + jnp.exp(-x)))


def _modnorm(x, nw, sh, sc):
    ms = jnp.mean(x * x, axis=-1, keepdims=True)
    return (x * lax.rsqrt(ms + EPS) * nw) * (1.0 + sc) + sh


def _matvec_kernel(x_ref, w_ref, b_ref, o_ref):
    x = x_ref[...]
    o_ref[...] = _dot(_silu(x).astype(BF16), w_ref[...].astype(BF16)) + b_ref[...]


def adaln_vectors(vecs, w_all, b_all, layer):
    nl, k, n = w_all.shape
    tn = 1024
    return pl.pallas_call(
        _matvec_kernel,
        grid=(n // tn,),
        in_specs=[pl.BlockSpec((8, k), lambda j: (0, 0)),
                  pl.BlockSpec((None, k, tn), lambda j: (layer, 0, j)),
                  pl.BlockSpec((None, 1, tn), lambda j: (layer, 0, j))],
        out_specs=pl.BlockSpec((8, tn), lambda j: (0, j)),
        out_shape=jax.ShapeDtypeStruct((8, n), F32),
        compiler_params=_cparams("arbitrary"),
        name="adaln_vectors",
    )(vecs, w_all, b_all.reshape(nl, 1, n))


def _normmm_kernel(x_ref, nw_ref, sh_ref, sc_ref, w_ref, b_ref, o_ref, xn_ref):
    @pl.when(pl.program_id(1) == 0)
    def _():
        xn_ref[...] = _modnorm(x_ref[...].astype(F32), nw_ref[...], sh_ref[...], sc_ref[...]).astype(BF16)

    o_ref[...] = (_dot(xn_ref[...], w_ref[...]) + b_ref[...]).astype(o_ref.dtype)


def normmm(x, nw, sh, sc, w, b, *, tm, tn, out_dtype, x_col=0):
    m = x.shape[0]
    k, n = w.shape
    return pl.pallas_call(
        _normmm_kernel,
        grid=(m // tm, n // tn),
        in_specs=[pl.BlockSpec((tm, k), lambda i, j: (i, x_col)),
                  pl.BlockSpec((1, k), lambda i, j: (0, 0)),
                  pl.BlockSpec((1, k), lambda i, j: (0, 0)),
                  pl.BlockSpec((1, k), lambda i, j: (0, 0)),
                  pl.BlockSpec((k, tn), lambda i, j: (0, j)),
                  pl.BlockSpec((1, tn), lambda i, j: (0, j))],
        out_specs=pl.BlockSpec((tm, tn), lambda i, j: (i, j)),
        out_shape=jax.ShapeDtypeStruct((m, n), out_dtype),
        scratch_shapes=[pltpu.VMEM((tm, k), BF16)],
        compiler_params=_cparams("parallel", "arbitrary"),
        name="normmm",
    )(x, nw.reshape(1, k), sh.reshape(1, k), sc.reshape(1, k), w, b.reshape(1, n))


def _normmm_conv_kernel(x_ref, xprev_ref, xnext_ref, nw_ref, sh_ref, sc_ref, w_ref, b_ref, cw_ref, cb_ref,
                        o_ref, xn_ref, xh_ref):
    i = pl.program_id(0)

    @pl.when(pl.program_id(1) == 0)
    def _():
        nw, sh, sc = nw_ref[...], sh_ref[...], sc_ref[...]
        xn_ref[...] = _modnorm(x_ref[...], nw, sh, sc).astype(BF16)
        xh_ref[0:8, :] = _modnorm(xprev_ref[...], nw, sh, sc).astype(BF16)
        xh_ref[8:16, :] = _modnorm(xnext_ref[...], nw, sh, sc).astype(BF16)

    w = w_ref[...]
    b = b_ref[...]
    y = _dot(xn_ref[...], w) + b
    yh = _dot(xh_ref[...], w) + b
    prev_row = jnp.where(i == 0, 0.0, yh[7:8, :])
    next_row = jnp.where(i == pl.num_programs(0) - 1, 0.0, yh[8:9, :])
    yp, yn = _shift_rows(y, prev_row, next_row)
    o_ref[...] = yp * cw_ref[0:1, :] + y * cw_ref[1:2, :] + yn * cw_ref[2:3, :] + cb_ref[...]


def normmm_conv(x, nw, sh, sc, w, b, cw, cb, *, tm, tn):
    m, k = x.shape
    n = w.shape[1]
    nrb = m // 8
    tb = tm // 8
    vec = pl.BlockSpec((1, k), lambda i, j: (0, 0))
    return pl.pallas_call(
        _normmm_conv_kernel,
        grid=(m // tm, n // tn),
        in_specs=[pl.BlockSpec((tm, k), lambda i, j: (i, 0)),
                  pl.BlockSpec((8, k), lambda i, j: (jnp.maximum(i * tb - 1, 0), 0)),
                  pl.BlockSpec((8, k), lambda i, j: (jnp.minimum((i + 1) * tb, nrb - 1), 0)),
                  vec, vec, vec,
                  pl.BlockSpec((k, tn), lambda i, j: (0, j)),
                  pl.BlockSpec((1, tn), lambda i, j: (0, j)),
                  pl.BlockSpec((3, tn), lambda i, j: (0, j)),
                  pl.BlockSpec((1, tn), lambda i, j: (0, j))],
        out_specs=pl.BlockSpec((tm, tn), lambda i, j: (i, j)),
        out_shape=jax.ShapeDtypeStruct((m, n), F32),
        scratch_shapes=[pltpu.VMEM((tm, k), BF16), pltpu.VMEM((16, k), BF16)],
        compiler_params=_cparams("parallel", "arbitrary"),
        name="normmm_conv",
    )(x, x, x, nw.reshape(1, k), sh.reshape(1, k), sc.reshape(1, k), w, b.reshape(1, n), cw, cb.reshape(1, n))


def _qproj_kernel(x_ref, nw_ref, wa_ref, wb_ref, ta_ref, tb_ref, o_ref, xn_ref):
    @pl.when(pl.program_id(1) == 0)
    def _():
        x = x_ref[...]
        ms = jnp.mean(x * x, axis=-1, keepdims=True)
        xn_ref[...] = (x * lax.rsqrt(ms + EPS) * nw_ref[...]).astype(BF16)

    xn = xn_ref[...]
    o_ref[...] = (_dot(xn, wa_ref[...]) * ta_ref[...] + _dot(xn, wb_ref[...]) * tb_ref[...]).astype(o_ref.dtype)


def qproj(p, nw, wa, wb, ta, tb, *, tm):
    m = ta.shape[0]
    k = MLA_Q_RANK
    hp = MLA_HEAD_PAD
    return pl.pallas_call(
        _qproj_kernel,
        grid=(m // tm, MLA_HEADS),
        in_specs=[pl.BlockSpec((tm, k), lambda i, j: (i, P_CQ // MLA_Q_RANK)),
                  pl.BlockSpec((1, k), lambda i, j: (0, 0)),
                  pl.BlockSpec((k, hp), lambda i, j: (0, j)),
                  pl.BlockSpec((k, hp), lambda i, j: (0, j)),
                  pl.BlockSpec((tm, hp), lambda i, j: (i, 0)),
                  pl.BlockSpec((tm, hp), lambda i, j: (i, 0))],
        out_specs=pl.BlockSpec((tm, hp), lambda i, j: (i, j)),
        out_shape=jax.ShapeDtypeStruct((m, MLA_HEADS * hp), BF16),
        scratch_shapes=[pltpu.VMEM((tm, k), BF16)],
        compiler_params=_cparams("parallel", "arbitrary"),
        name="mla_qproj",
    )(p, nw.reshape(1, k), wa, wb, ta, tb)


def _kv_assemble_kernel(kv_ref, kr_ref, tc_ref, ts_ref, k_ref, v_ref, *, rope):
    krr = kr_ref[...]
    kr = krr[:, :MLA_ROPE]
    if rope:
        kr = kr * tc_ref[...][:, :MLA_ROPE] + krr[:, MLA_ROPE:] * ts_ref[...][:, :MLA_ROPE]
    tail = jnp.concatenate([kr, jnp.zeros_like(kr)], axis=-1).astype(BF16)
    for h in range(MLA_HEADS):
        base = h * (MLA_NOPE + MLA_V)
        k_ref[:, h * MLA_HEAD_PAD:h * MLA_HEAD_PAD + MLA_NOPE] = kv_ref[:, base:base + MLA_NOPE]
        k_ref[:, h * MLA_HEAD_PAD + MLA_NOPE:(h + 1) * MLA_HEAD_PAD] = tail
        v_ref[:, h * MLA_V:(h + 1) * MLA_V] = kv_ref[:, base + MLA_NOPE:base + MLA_NOPE + MLA_V]


def kv_assemble(kvp, p, tc, ts, *, tm, rope):
    m = kvp.shape[0]
    return pl.pallas_call(
        functools.partial(_kv_assemble_kernel, rope=rope),
        grid=(m // tm,),
        in_specs=[pl.BlockSpec((tm, MLA_HEADS * (MLA_NOPE + MLA_V)), lambda i: (i, 0)),
                  pl.BlockSpec((tm, 128), lambda i: (i, P_KR // 128)),
                  pl.BlockSpec((tm, 128), lambda i: (i, 0)),
                  pl.BlockSpec((tm, 128), lambda i: (i, 0))],
        out_specs=[pl.BlockSpec((tm, MLA_HEADS * MLA_HEAD_PAD), lambda i: (i, 0)),
                   pl.BlockSpec((tm, MLA_HEADS * MLA_V), lambda i: (i, 0))],
        out_shape=[jax.ShapeDtypeStruct((m, MLA_HEADS * MLA_HEAD_PAD), BF16),
                   jax.ShapeDtypeStruct((m, MLA_HEADS * MLA_V), BF16)],
        compiler_params=_cparams("parallel"),
        name="mla_kv_assemble",
    )(kvp, p, tc, ts)


def _attn_kernel(q_ref, k_ref, v_ref, o_ref, s0_ref, s1_ref, m_ref, acc_ref, *, nk):
    j = pl.program_id(2)

    def qk(s_ref):
        s_ref[...] = lax.dot_general(q_ref[...], k_ref[...], (((1,), (1,)), ((), ())), preferred_element_type=F32)

    def softmax_pv(s_ref):
        s = s_ref[...]
        m_prev = m_ref[...]
        m_new = jnp.maximum(m_prev, jnp.max(s, axis=-1, keepdims=True))
        alpha = jnp.exp2(m_prev - m_new)
        p = jnp.exp2(s - m_new).astype(BF16)
        v = v_ref[...]
        lane = lax.broadcasted_iota(jnp.int32, v.shape, 1)
        ones_col = jnp.where(lane == 0, 1.0, 0.0).astype(BF16)
        v_ext = jnp.concatenate([v, ones_col], axis=1)
        acc_ref[...] = alpha * acc_ref[...] + _dot(p, v_ext)
        m_ref[...] = m_new

    @pl.when(j == 0)
    def _():
        m_ref[...] = jnp.full(m_ref.shape, -jnp.inf, F32)
        acc_ref[...] = jnp.zeros(acc_ref.shape, F32)
        qk(s0_ref)

    mid = jnp.logical_and(j > 0, j < nk)

    @pl.when(jnp.logical_and(mid, j % 2 == 1))
    def _():
        softmax_pv(s0_ref)
        qk(s1_ref)

    @pl.when(jnp.logical_and(mid, j % 2 == 0))
    def _():
        softmax_pv(s1_ref)
        qk(s0_ref)

    @pl.when(j == nk)
    def _():
        softmax_pv(s1_ref if (nk - 1) % 2 else s0_ref)
        acc = acc_ref[...]
        o_ref[...] = (acc[:, :MLA_V] / acc[:, MLA_V:MLA_V + 1]).astype(o_ref.dtype)


def attention(q, k, v, *, tq, tk):
    lq = q.shape[0]
    s = k.shape[0]
    nk = s // tk
    return pl.pallas_call(
        functools.partial(_attn_kernel, nk=nk),
        grid=(MLA_HEADS, lq // tq, nk + 1),
        in_specs=[pl.BlockSpec((tq, MLA_HEAD_PAD), lambda h, i, j: (i, h)),
                  pl.BlockSpec((tk, MLA_HEAD_PAD), lambda h, i, j: (jnp.minimum(j, nk - 1), h)),
                  pl.BlockSpec((tk, MLA_V), lambda h, i, j: (jnp.maximum(j - 1, 0), h))],
        out_specs=pl.BlockSpec((tq, MLA_V), lambda h, i, j: (i, h)),
        out_shape=jax.ShapeDtypeStruct((lq, MLA_HEADS * MLA_V), BF16),
        scratch_shapes=[pltpu.VMEM((tq, tk), F32), pltpu.VMEM((tq, tk), F32), pltpu.VMEM((tq, 1), F32),
                        pltpu.VMEM((tq, 2 * MLA_V), F32)],
        compiler_params=_cparams("parallel", "parallel", "arbitrary"),
        name="mla_attention",
    )(q, k, v)


def _shift_rows(x, prev_row, next_row):
    tm = x.shape[0]
    rows = lax.broadcasted_iota(jnp.int32, x.shape, 0)
    xp = jnp.where(rows == 0, prev_row, pltpu.roll(x, 1, 0))
    xn = jnp.where(rows == tm - 1, next_row, pltpu.roll(x, tm - 1, 0))
    return xp, xn


def _dwconv_kernel(x_ref, xprev_ref, xnext_ref, w_ref, b_ref, o_ref, *, act):
    i = pl.program_id(0)
    x = x_ref[...]
    prev_row = jnp.where(i == 0, 0.0, xprev_ref[7:8, :])
    next_row = jnp.where(i == pl.num_programs(0) - 1, 0.0, xnext_ref[0:1, :])
    xp, xn = _shift_rows(x, prev_row, next_row)
    y = xp * w_ref[0:1, :] + x * w_ref[1:2, :] + xn * w_ref[2:3, :] + b_ref[...]
    if act:
        y = _silu(y)
    o_ref[...] = y


def dwconv3(x, w, b, *, tm, tc, col0, width, act):
    m = x.shape[0]
    cb = col0 // tc
    nrb = m // 8
    tb = tm // 8
    return pl.pallas_call(
        functools.partial(_dwconv_kernel, act=act),
        grid=(m // tm, width // tc),
        in_specs=[pl.BlockSpec((tm, tc), lambda i, j: (i, cb + j)),
                  pl.BlockSpec((8, tc), lambda i, j: (jnp.maximum(i * tb - 1, 0), cb + j)),
                  pl.BlockSpec((8, tc), lambda i, j: (jnp.minimum((i + 1) * tb, nrb - 1), cb + j)),
                  pl.BlockSpec((3, tc), lambda i, j: (0, j)),
                  pl.BlockSpec((1, tc), lambda i, j: (0, j))],
        out_specs=pl.BlockSpec((tm, tc), lambda i, j: (i, j)),
        out_shape=jax.ShapeDtypeStruct((m, width), F32),
        compiler_params=_cparams("arbitrary", "arbitrary"),
        name="dwconv3",
    )(x, x, x, w, b.reshape(1, width))


def _softplus(x):
    return jnp.maximum(x, 0.0) + jnp.log(1.0 + jnp.exp(-jnp.abs(x)))


def _ssd_prepare(d, xbc_ref, dt_ref, dtt_ref, bias_ref, biast_ref, a_ref, at_ref, tri_ref):
    q = SSD_CHUNK
    dt = _softplus(dt_ref[0] + bias_ref[d])
    dtt = _softplus(dtt_ref[0, 0] + biast_ref[d])
    dta_hi, dta_lo = _split(dt * a_ref[d])
    dtat_hi, dtat_lo = _split(dtt * at_ref[d])
    tri = tri_ref[d]
    trit = tri_ref[1 - d]
    acum = _dot(tri, dta_hi) + _dot(tri, dta_lo)
    acumt = _dot(dtat_hi, trit) + _dot(dtat_lo, trit)
    total = acum[q - 1:q, :] if d == 0 else acum[0:1, :]
    return dict(xbc=xbc_ref[...], dt=dt, acum=acum, acumt=acumt, to_end=jnp.exp(total - acum),
                from_start=jnp.exp(acum), chunk_decay=jnp.exp(total), mask=tri > 0.5)


def _per_head_lanes(a, g):
    return jnp.concatenate([jnp.broadcast_to(a[:, h:h + 1], (a.shape[0], SSD_HEAD_DIM))
                            for h in range(g * SSD_HPG, (g + 1) * SSD_HPG)], axis=1)


def _ssd_group(d, p, g, h_ref):
    xbc = p["xbc"]
    gw = SSD_HPG * SSD_HEAD_DIM
    bm = xbc[:, SSD_INNER + g * SSD_STATE:SSD_INNER + (g + 1) * SSD_STATE]
    cm = xbc[:, SSD_INNER + (SSD_GROUPS + g) * SSD_STATE:SSD_INNER + (SSD_GROUPS + g + 1) * SSD_STATE]
    bm_b = bm.astype(BF16)
    cm_b = cm.astype(BF16)
    cb = lax.dot_general(cm_b, bm_b, (((1,), (1,)), ((), ())), preferred_element_type=F32)
    xdt = xbc[:, g * gw:(g + 1) * gw] * _per_head_lanes(p["dt"], g)
    h_prev = h_ref[d * SSD_GROUPS + g]
    y_off = _dot(cm_b, h_prev.astype(BF16)) * _per_head_lanes(p["from_start"], g)
    states = _dot(bm.T.astype(BF16), (xdt * _per_head_lanes(p["to_end"], g)).astype(BF16))
    h_ref[d * SSD_GROUPS + g] = h_prev * _per_head_lanes(p["chunk_decay"], g) + states
    return dict(cb=cb, xdt_b=xdt.astype(BF16), y_off=y_off)


def _ssd_head(p, grp, g, r, o_ref):
    h = g * SSD_HPG + r
    seg = p["acum"][:, h:h + 1] - p["acumt"][h:h + 1, :]
    decay = jnp.exp(jnp.where(p["mask"], seg, -jnp.inf))
    lanes = slice(r * SSD_HEAD_DIM, (r + 1) * SSD_HEAD_DIM)
    y_diag = _dot((grp["cb"] * decay).astype(BF16), grp["xdt_b"][:, lanes])
    o_ref[:, h * SSD_HEAD_DIM:(h + 1) * SSD_HEAD_DIM] = y_diag + grp["y_off"][:, lanes]


def _ssd_kernel(xf_ref, xb_ref, dtf_ref, dtb_ref, dttf_ref, dttb_ref, bias_ref, biast_ref, a_ref, at_ref, tri_ref,
                of_ref, ob_ref, h_ref):
    @pl.when(pl.program_id(0) == 0)
    def _():
        h_ref[...] = jnp.zeros(h_ref.shape, F32)

    prm = (_ssd_prepare(0, xf_ref, dtf_ref, dttf_ref, bias_ref, biast_ref, a_ref, at_ref, tri_ref),
           _ssd_prepare(1, xb_ref, dtb_ref, dttb_ref, bias_ref, biast_ref, a_ref, at_ref, tri_ref))
    outs = (of_ref, ob_ref)
    for g in range(SSD_GROUPS):
        grp = (_ssd_group(0, prm[0], g, h_ref), _ssd_group(1, prm[1], g, h_ref))
        for r in range(SSD_HPG):
            for d in range(2):
                _ssd_head(prm[d], grp[d], g, r, outs[d])


def ssd_scan(xbc, dt2, dt2t, bias2, bias2t, a2, a2t, tri2, *, n_lat_chunks):
    nc = n_lat_chunks
    ncx = CTX_LEN // SSD_CHUNK
    tot = nc + ncx
    q = SSD_CHUNK

    def cf(s):
        return lax.rem(s + nc, tot)

    def cbk(s):
        return tot - 1 - s

    def full(shape):
        return pl.BlockSpec(shape, lambda s: (0,) * len(shape))

    return pl.pallas_call(
        _ssd_kernel,
        grid=(tot,),
        in_specs=[pl.BlockSpec((q, SSD_XBC), lambda s: (cf(s), 0)),
                  pl.BlockSpec((q, SSD_XBC), lambda s: (cbk(s), 0)),
                  pl.BlockSpec((1, q, SSD_HEADS), lambda s: (0, cf(s), 0)),
                  pl.BlockSpec((1, q, SSD_HEADS), lambda s: (1, cbk(s), 0)),
                  pl.BlockSpec((1, 1, SSD_HEADS, q), lambda s: (0, cf(s), 0, 0)),
                  pl.BlockSpec((1, 1, SSD_HEADS, q), lambda s: (1, cbk(s), 0, 0)),
                  full((2, 1, SSD_HEADS)), full((2, SSD_HEADS, 1)), full((2, 1, SSD_HEADS)),
                  full((2, SSD_HEADS, 1)), full((2, q, q))],
        out_specs=[pl.BlockSpec((q, SSD_INNER), lambda s: (jnp.where(cf(s) >= nc, 0, cf(s)), 0)),
                   pl.BlockSpec((q, SSD_INNER), lambda s: (jnp.where(cbk(s) >= nc, nc - 1, cbk(s)), 0))],
        out_shape=[jax.ShapeDtypeStruct((nc * q, SSD_INNER), F32), jax.ShapeDtypeStruct((nc * q, SSD_INNER), F32)],
        scratch_shapes=[pltpu.VMEM((2 * SSD_GROUPS, SSD_STATE, SSD_HPG * SSD_HEAD_DIM), F32)],
        compiler_params=_cparams("arbitrary"),
        name="ssd_scan",
    )(xbc, xbc, dt2, dt2, dt2t, dt2t, bias2, bias2t, a2, a2t, tri2)


def _merge_kernel(o_ref, yf_ref, yb_ref, xs_ref, z_ref, dsk_ref, nw_ref, w_ref, x_ref, g_ref, out_ref, a_ref):
    @pl.when(pl.program_id(1) == 0)
    def _():
        a_ref[:, :MLA_HEADS * MLA_V] = o_ref[...]
        y = yf_ref[...] + yb_ref[...] + dsk_ref[...] * xs_ref[...]
        gy = y * _silu(z_ref[...])
        gw = SSD_INNER // SSD_GROUPS
        for g in range(SSD_GROUPS):
            part = gy[:, g * gw:(g + 1) * gw]
            ms = jnp.mean(part * part, axis=-1, keepdims=True)
            a_ref[:, MLA_HEADS * MLA_V + g * gw:MLA_HEADS * MLA_V + (g + 1) * gw] = (
                part * lax.rsqrt(ms + EPS) * nw_ref[:, g * gw:(g + 1) * gw]).astype(BF16)

    out_ref[...] = x_ref[...] + g_ref[...] * _dot(a_ref[...], w_ref[...])


def merge_out(o_att, yf, yb, xbc, p, dsk, nw, w_o, x, gate, *, tm, tn):
    m, n = x.shape
    kw = w_o.shape[0]
    return pl.pallas_call(
        _merge_kernel,
        grid=(m // tm, n // tn),
        in_specs=[pl.BlockSpec((tm, MLA_HEADS * MLA_V), lambda i, j: (i, 0)),
                  pl.BlockSpec((tm, SSD_INNER), lambda i, j: (i, 0)),
                  pl.BlockSpec((tm, SSD_INNER), lambda i, j: (i, 0)),
                  pl.BlockSpec((tm, SSD_INNER), lambda i, j: (i, 0)),
                  pl.BlockSpec((tm, SSD_INNER), lambda i, j: (i, 0)),
                  pl.BlockSpec((1, SSD_INNER), lambda i, j: (0, 0)),
                  pl.BlockSpec((1, SSD_INNER), lambda i, j: (0, 0)),
                  pl.BlockSpec((kw, tn), lambda i, j: (0, j)),
                  pl.BlockSpec((tm, tn), lambda i, j: (i, j)),
                  pl.BlockSpec((1, tn), lambda i, j: (0, j))],
        out_specs=pl.BlockSpec((tm, tn), lambda i, j: (i, j)),
        out_shape=jax.ShapeDtypeStruct((m, n), F32),
        scratch_shapes=[pltpu.VMEM((tm, kw), BF16)],
        compiler_params=_cparams("parallel", "arbitrary"),
        name="merge_out",
    )(o_att, yf, yb, xbc, p, dsk, nw, w_o, x, gate)


def _mmres_kernel(a_ref, w_ref, b_ref, x_ref, g_ref, o_ref):
    o_ref[...] = x_ref[...] + g_ref[...] * (_dot(a_ref[...].astype(BF16), w_ref[...]) + b_ref[...])


def mm_res(a, w, b, x, gate, *, tm, tn):
    m, k = a.shape
    n = w.shape[1]
    return pl.pallas_call(
        _mmres_kernel,
        grid=(m // tm, n // tn),
        in_specs=[pl.BlockSpec((tm, k), lambda i, j: (i, 0)),
                  pl.BlockSpec((k, tn), lambda i, j: (0, j)),
                  pl.BlockSpec((1, tn), lambda i, j: (0, j)),
                  pl.BlockSpec((tm, tn), lambda i, j: (i, j)),
                  pl.BlockSpec((1, tn), lambda i, j: (0, j))],
        out_specs=pl.BlockSpec((tm, tn), lambda i, j: (i, j)),
        out_shape=jax.ShapeDtypeStruct((m, n), F32),
        compiler_params=_cparams("parallel", "arbitrary"),
        name="mm_res",
    )(a, w, b.reshape(1, n), x, gate)


def _ffn_kernel(x_ref, nw_ref, sh_ref, sc_ref, g_ref, w1_ref, w3_ref, w2_ref, fw_ref, o_ref, xn_ref, acc_ref,
                *, final_norm):
    f = pl.program_id(1)

    @pl.when(f == 0)
    def _():
        xn_ref[...] = _modnorm(x_ref[...], nw_ref[...], sh_ref[...], sc_ref[...]).astype(BF16)
        acc_ref[...] = jnp.zeros(acc_ref.shape, F32)

    xn = xn_ref[...]
    a = _dot(xn, w1_ref[...])
    b = _dot(xn, w3_ref[...])
    acc_ref[...] += _dot((_silu(a) * b).astype(BF16), w2_ref[...])

    @pl.when(f == pl.num_programs(1) - 1)
    def _():
        y = x_ref[...] + g_ref[...] * acc_ref[...]
        if final_norm:
            ms = jnp.mean(y * y, axis=-1, keepdims=True)
            y = y * lax.rsqrt(ms + EPS) * fw_ref[...]
        o_ref[...] = y


def ffn(x, nw, sh, sc, gate, w1, w3, w2, fw, *, layer, tm, tf, final_norm):
    m, dm = x.shape
    dff = w1.shape[2]
    vec = pl.BlockSpec((1, dm), lambda i, f: (0, 0))
    return pl.pallas_call(
        functools.partial(_ffn_kernel, final_norm=final_norm),
        grid=(m // tm, dff // tf),
        in_specs=[pl.BlockSpec((tm, dm), lambda i, f: (i, 0)), vec, vec, vec, vec,
                  pl.BlockSpec((None, dm, tf), lambda i, f: (layer, 0, f)),
                  pl.BlockSpec((None, dm, tf), lambda i, f: (layer, 0, f)),
                  pl.BlockSpec((None, tf, dm), lambda i, f: (layer, f, 0)),
                  vec],
        out_specs=pl.BlockSpec((tm, dm), lambda i, f: (i, 0)),
        out_shape=jax.ShapeDtypeStruct((m, dm), F32),
        scratch_shapes=[pltpu.VMEM((tm, dm), BF16), pltpu.VMEM((tm, dm), F32)],
        compiler_params=_cparams("parallel", "arbitrary"),
        name="ffn",
    )(x, nw.reshape(1, dm), sh, sc, gate, w1, w3, w2, fw.reshape(1, dm))


def _filter_hidden_kernel(fvec_ref, w1_ref, b1_ref, wm_ref, bm_ref, freq_ref, h_ref, *, seq_len, tr):
    i = pl.program_id(0)
    n = i * tr + lax.broadcasted_iota(jnp.int32, (tr, 1), 0)
    lag = jnp.where(n < seq_len, n, 2 * seq_len - n).astype(F32)
    t = lag / float(seq_len - 1)
    ang = lag * (2.0 * math.pi / seq_len)
    lane = lax.broadcasted_iota(jnp.int32, (tr, 128), 1)
    bands = (HY_EMB - 1) // 2
    arg = ang * fvec_ref[...]
    emb = jnp.where(lane == 0, t, jnp.where(lane <= bands, jnp.cos(arg), jnp.where(lane <= 2 * bands, -jnp.sin(arg), 0.0)))
    fr = freq_ref[...]
    hid = jnp.sin(fr * (_dot(emb.astype(BF16), w1_ref[...]) + b1_ref[...]))
    for j in range(wm_ref.shape[0]):
        hid = jnp.sin(fr * (_dot(hid.astype(BF16), wm_ref[j]) + bm_ref[j]))
    h_ref[...] = hid


def filter_hidden(fvec, w1, b1, wm, bm, freq, *, seq_len, tr):
    full = lambda shp: pl.BlockSpec(shp, lambda i: (0,) * len(shp))
    return pl.pallas_call(
        functools.partial(_filter_hidden_kernel, seq_len=seq_len, tr=tr),
        grid=(2 * seq_len // tr,),
        in_specs=[full((1, 128)), full(w1.shape), full(b1.shape), full(wm.shape), full(bm.shape), full(freq.shape)],
        out_specs=pl.BlockSpec((tr, HY_HID), lambda i: (i, 0)),
        out_shape=jax.ShapeDtypeStruct((2 * seq_len, HY_HID), F32),
        compiler_params=_cparams("parallel"),
        name="hyena_filter_hidden",
    )(fvec, w1, b1, wm, bm, freq)


ROW_GROUP = 16


def _filter_stage1_kernel(f_ref, hid_ref, wo_ref, delta_ref, o_ref, abs_ref, hs_ref, os_ref, *, seq_len):
    i = pl.program_id(1)
    n1 = hid_ref.shape[0]
    half = n1 // 2
    n1_idx = lax.broadcasted_iota(jnp.int32, (n1, 1), 0)
    delta = delta_ref[...]
    f = f_ref[...]
    wo_past, wo_future = wo_ref[0], wo_ref[1]

    @pl.when(i == 0)
    def _():
        abs_ref[...] = jnp.zeros(abs_ref.shape, F32)

    asum = jnp.zeros(abs_ref.shape, F32)
    for r in range(ROW_GROUP):
        hs_ref[...] = hid_ref[:, r, :]
        hb = hs_ref[...].astype(BF16)
        k = jnp.concatenate([_dot(hb[:half], wo_past), _dot(hb[half:], wo_future)], axis=0)
        n = n1_idx * FFT_N2 + (i * ROW_GROUP + r)
        lag = jnp.where(n < seq_len, n, 2 * seq_len - n).astype(F32)
        k = k * jnp.exp(-(lag / float(seq_len - 1)) * delta)
        k = jnp.where(n == seq_len, 0.0, k)
        asum = asum + jnp.sum(jnp.abs(k), axis=0, keepdims=True)
        os_ref[:, r, :] = _dot(f, k.astype(BF16))
    abs_ref[...] += asum
    o_ref[...] = os_ref[...].astype(o_ref.dtype)


def filter_stage1(f1, hid3, wo, delta, *, seq_len, tn):
    mrows, n1 = f1.shape
    dd = delta.shape[1]
    return pl.pallas_call(
        functools.partial(_filter_stage1_kernel, seq_len=seq_len),
        grid=(dd // tn, FFT_N2 // ROW_GROUP),
        in_specs=[pl.BlockSpec((mrows, n1), lambda j, i: (0, 0)),
                  pl.BlockSpec((n1, ROW_GROUP, HY_HID), lambda j, i: (0, i, 0)),
                  pl.BlockSpec((2, HY_HID, tn), lambda j, i: (0, 0, j)),
                  pl.BlockSpec((1, tn), lambda j, i: (0, j))],
        out_specs=[pl.BlockSpec((mrows, ROW_GROUP, tn), lambda j, i: (0, i, j)),
                   pl.BlockSpec((1, tn), lambda j, i: (0, j))],
        out_shape=[jax.ShapeDtypeStruct((mrows, FFT_N2, dd), BF16), jax.ShapeDtypeStruct((1, dd), F32)],
        scratch_shapes=[pltpu.VMEM((n1, HY_HID), F32), pltpu.VMEM((mrows, ROW_GROUP, tn), F32)],
        compiler_params=_cparams("parallel", "arbitrary"),
        name="filter_stage1",
    )(f1, hid3, wo, delta)


def _fft1_kernel(f_ref, x_ref, o_ref, xs_ref, os_ref):
    for r in range(ROW_GROUP):
        xs_ref[...] = x_ref[:, r, :]
        os_ref[:, r, :] = _dot(f_ref[...], xs_ref[...].astype(BF16))
    o_ref[...] = os_ref[...].astype(o_ref.dtype)


def _fft1_inv_kernel(g_ref, b_ref, v_ref, gate_ref, fb_ref, o_ref, bf_ref, xs_ref):
    bf_ref[...] = b_ref[...].astype(F32)
    for r in range(ROW_GROUP):
        xs_ref[...] = bf_ref[:, r, :]
        o_ref[:, r, :] = _dot(g_ref[...], xs_ref[...].astype(BF16))
    o_ref[...] = gate_ref[...] * (o_ref[...] + fb_ref[...] * v_ref[...])


def fft_stage1(f1, x3, *, d, tn, col_ofs=0):
    mrows, k = f1.shape
    n2 = x3.shape[1]
    nd = d // tn
    return pl.pallas_call(
        _fft1_kernel,
        grid=(n2 // ROW_GROUP, nd),
        in_specs=[pl.BlockSpec((mrows, k), lambda i, j: (0, 0)),
                  pl.BlockSpec((k, ROW_GROUP, tn), lambda i, j: (0, i, col_ofs * nd + j))],
        out_specs=pl.BlockSpec((mrows, ROW_GROUP, tn), lambda i, j: (0, i, j)),
        out_shape=jax.ShapeDtypeStruct((mrows, n2, d), BF16),
        scratch_shapes=[pltpu.VMEM((k, tn), F32), pltpu.VMEM((mrows, ROW_GROUP, tn), F32)],
        compiler_params=_cparams("parallel", "arbitrary"),
        name="fft_stage1",
    )(f1, x3)


def fft_stage1_inv(g1, b3, v3, v_ofs, gate3, gate_ofs, fbias, *, d, tn):
    mrows, k = g1.shape
    n2 = b3.shape[1]
    nd = d // tn
    return pl.pallas_call(
        _fft1_inv_kernel,
        grid=(n2 // ROW_GROUP, nd),
        in_specs=[pl.BlockSpec((mrows, k), lambda i, j: (0, 0)),
                  pl.BlockSpec((k, ROW_GROUP, tn), lambda i, j: (0, i, j)),
                  pl.BlockSpec((mrows, ROW_GROUP, tn), lambda i, j: (0, i, v_ofs * nd + j)),
                  pl.BlockSpec((mrows, ROW_GROUP, tn), lambda i, j: (0, i, gate_ofs * nd + j)),
                  pl.BlockSpec((1, tn), lambda i, j: (0, j))],
        out_specs=pl.BlockSpec((mrows, ROW_GROUP, tn), lambda i, j: (0, i, j)),
        out_shape=jax.ShapeDtypeStruct((mrows, n2, d), F32),
        scratch_shapes=[pltpu.VMEM((k, ROW_GROUP, tn), F32), pltpu.VMEM((k, tn), F32)],
        compiler_params=_cparams("parallel", "arbitrary"),
        name="fft_stage1_inv",
    )(g1, b3, v3, gate3, fbias)


def _fftmid_kernel(a_ref, ak_ref, mf_ref, mi_ref, scale_ref, o_ref):
    half = FFT_N2
    scale = scale_ref[...]
    for s in range(SLAB_TILE):
        mf = mf_ref[s]
        y = _dot(mf, a_ref[s])
        kf = _dot(mf, ak_ref[s])
        yr, yi = y[:half], y[half:]
        kr, ki = kf[:half], kf[half:]
        pr = (yr * kr - yi * ki) * scale
        pi = (yr * ki + yi * kr) * scale
        o_ref[s] = _dot(mi_ref[s], jnp.concatenate([pr, pi], axis=0).astype(BF16)).astype(o_ref.dtype)


def fft_mid(a3, ak3, ak_ofs, mf, mi, scale, *, dc):
    nsp, rows, d = a3.shape
    nd = d // dc
    blk = pl.BlockSpec((SLAB_TILE, rows, dc), lambda s, j: (s, 0, j))
    mat = pl.BlockSpec((SLAB_TILE, rows, rows), lambda s, j: (s, 0, 0))
    return pl.pallas_call(
        _fftmid_kernel,
        grid=(nsp // SLAB_TILE, nd),
        in_specs=[blk, pl.BlockSpec((SLAB_TILE, rows, dc), lambda s, j: (s, 0, ak_ofs * nd + j)), mat, mat,
                  pl.BlockSpec((1, dc), lambda s, j: (0, j))],
        out_specs=blk,
        out_shape=jax.ShapeDtypeStruct((nsp, rows, d), BF16),
        compiler_params=_cparams("parallel", "arbitrary"),
        name="fft_mid",
    )(a3, ak3, mf, mi, scale)


def _fft_tables(seq_len):
    n = 2 * seq_len
    n2 = FFT_N2
    n1 = n // n2
    ns = n1 // 2 + 1
    nsp = -(-ns // SLAB_TILE) * SLAB_TILE
    k1 = jnp.arange(nsp, dtype=jnp.int32)
    valid = (k1 < ns)
    m1 = jnp.arange(n1, dtype=jnp.int32)
    ang1 = (2.0 * math.pi / n1) * lax.rem(k1[:, None] * m1[None, :], n1).astype(F32)
    vf = valid[:, None].astype(F32)
    f1 = jnp.stack([jnp.cos(ang1) * vf, -jnp.sin(ang1) * vf], axis=1).reshape(2 * nsp, n1)
    wgt = jnp.where((k1 == 0) | (k1 == n1 // 2), 1.0, 2.0) * valid.astype(F32) / n
    g1 = jnp.stack([jnp.cos(ang1) * wgt[:, None], -jnp.sin(ang1) * wgt[:, None]], axis=1)
    g1 = g1.reshape(2 * nsp, n1).T[: n1 // 2]
    k2 = jnp.arange(n2, dtype=jnp.int32)
    m2 = jnp.arange(n2, dtype=jnp.int32)
    f = k1[:, None, None] + n1 * k2[None, :, None]
    ang2 = (2.0 * math.pi / n) * lax.rem(f * m2[None, None, :], n).astype(F32)
    vm = valid[:, None, None].astype(F32)
    c2, s2 = jnp.cos(ang2) * vm, jnp.sin(ang2) * vm
    mf = jnp.concatenate([jnp.concatenate([c2, s2], axis=2), jnp.concatenate([-s2, c2], axis=2)], axis=1)
    c2t, s2t = jnp.swapaxes(c2, 1, 2), jnp.swapaxes(s2, 1, 2)
    mi = jnp.concatenate([jnp.concatenate([c2t, -s2t], axis=2), jnp.concatenate([s2t, c2t], axis=2)], axis=1)
    return dict(n1=n1, nsp=nsp, f1=f1.astype(BF16), g1=g1.astype(BF16), mf=mf.astype(BF16), mi=mi.astype(BF16))


def long_conv_gate(v, v_cols, gate, gate_cols, fbias, ak3, ak_cols, scale, tabs, *, seq_len, d):
    n1, nsp = tabs["n1"], tabs["nsp"]
    half = n1 // 2
    v3 = v.reshape(half, FFT_N2, v.shape[1])
    g3 = gate.reshape(half, FFT_N2, gate.shape[1])
    a = fft_stage1(tabs["f1"][:, :half], v3, d=d, tn=256, col_ofs=v_cols)
    b3 = fft_mid(a.reshape(nsp, 2 * FFT_N2, d), ak3, ak_cols, tabs["mf"], tabs["mi"], scale, dc=512)
    out = fft_stage1_inv(tabs["g1"], b3.reshape(2 * nsp, FFT_N2, d), v3, v_cols, g3, gate_cols, fbias, d=d, tn=256)
    return out.reshape(seq_len, d)


def _rope_tables(seq_len, scale):
    n_freq = MLA_ROPE // 4
    rows = seq_len // GRID_W
    row = jnp.repeat(jnp.arange(rows, dtype=F32), GRID_W)
    col = jnp.tile(jnp.arange(GRID_W, dtype=F32), rows)
    inv = ROPE_THETA ** (-jnp.arange(n_freq, dtype=F32) / n_freq)
    ang = jnp.stack([row[:, None] * inv, col[:, None] * inv], axis=1)
    cos = jnp.broadcast_to(jnp.cos(ang)[:, :, None, :], (seq_len, 2, 2, n_freq)).reshape(seq_len, MLA_ROPE)
    sin = jnp.broadcast_to(jnp.sin(ang)[:, :, None, :], (seq_len, 2, 2, n_freq)).reshape(seq_len, MLA_ROPE)
    one = jnp.ones((seq_len, MLA_NOPE), F32)
    zero = jnp.zeros((seq_len, MLA_NOPE), F32)
    z64 = jnp.zeros((seq_len, MLA_HEAD_PAD - MLA_NOPE - MLA_ROPE), F32)
    ta = jnp.concatenate([one, cos, z64], axis=1) * scale
    tb = jnp.concatenate([zero, sin, z64], axis=1) * scale
    tc = jnp.concatenate([cos, z64], axis=1)
    ts = jnp.concatenate([sin, z64], axis=1)
    return ta, tb, tc, ts


def _rope_swap_cols(w):
    n_freq = MLA_ROPE // 4
    w4 = w.reshape(w.shape[0], 2, 2, n_freq)
    return jnp.stack([-w4[:, :, 1], w4[:, :, 0]], axis=2).reshape(w.shape[0], MLA_ROPE)


def _even_layer(x, ctx, mods, norm_mix_w, w_in, conv_w, conv_b, dt_bias, a_log, d_skip, ssd_norm_w,
                q_norm_w, w_uq, kv_norm_w, w_ukv, w_o):
    seq_len, d = x.shape
    sh1, sc1, g1 = (mods[0:1, i * d:(i + 1) * d] for i in range(3))
    csh1, csc1 = mods[1:2, 0:d], mods[1:2, d:2 * d]

    o1 = Q_SIDE + SSD_XBC
    o2 = o1 + SSD_DT
    o3 = o2 + MLA_KV_RANK
    w_kr = w_in[:, o3:]
    w_ext = jnp.concatenate([w_in[:, :Q_SIDE], w_in[:, Q_SIDE:o1], w_in[:, o2:o3], w_kr, _rope_swap_cols(w_kr),
                             w_in[:, o1:o2], jnp.zeros((d, P_COLS - P_DT - SSD_DT), F32)], axis=1).astype(BF16)
    zb = jnp.zeros((P_COLS,), F32)
    p_lat = normmm(x, norm_mix_w, sh1, sc1, w_ext, zb, tm=512, tn=768, out_dtype=F32)
    p_ctx = normmm(ctx, norm_mix_w, csh1, csc1, w_ext, zb, tm=CTX_LEN, tn=768, out_dtype=F32)

    scale = float(MLA_NOPE + MLA_ROPE) ** -0.5 * math.log2(math.e)
    ta, tb, tc, ts = _rope_tables(seq_len, scale)
    wq = w_uq.reshape(MLA_Q_RANK, MLA_HEADS, MLA_NOPE + MLA_ROPE)
    zpad = jnp.zeros((MLA_Q_RANK, MLA_HEADS, MLA_HEAD_PAD - MLA_NOPE - MLA_ROPE), F32)
    wa = jnp.concatenate([wq, zpad], axis=2).reshape(MLA_Q_RANK, -1).astype(BF16)
    wr = wq[:, :, MLA_NOPE:]
    n_freq = MLA_ROPE // 4
    wr4 = wr.reshape(MLA_Q_RANK, MLA_HEADS, 2, 2, n_freq)
    wsw = jnp.stack([-wr4[:, :, :, 1], wr4[:, :, :, 0]], axis=3).reshape(MLA_Q_RANK, MLA_HEADS, MLA_ROPE)
    wb = jnp.concatenate([jnp.zeros((MLA_Q_RANK, MLA_HEADS, MLA_NOPE), F32), wsw, zpad], axis=2)
    wb = wb.reshape(MLA_Q_RANK, -1).astype(BF16)
    q = qproj(p_lat, q_norm_w, wa, wb, ta, tb, tm=512)

    zk = jnp.zeros((MLA_KV_RANK,), F32)
    w_ukv_b = w_ukv.astype(BF16)
    zkb = jnp.zeros((w_ukv.shape[1],), F32)
    kv_lat = normmm(p_lat, kv_norm_w, zk, zk, w_ukv_b, zkb, tm=512, tn=512, out_dtype=BF16,
                    x_col=P_CKV // MLA_KV_RANK)
    kv_ctx = normmm(p_ctx, kv_norm_w, zk, zk, w_ukv_b, zkb, tm=CTX_LEN, tn=512, out_dtype=BF16,
                    x_col=P_CKV // MLA_KV_RANK)
    k_lat, v_lat = kv_assemble(kv_lat, p_lat, tc, ts, tm=512, rope=True)
    k_ctx, v_ctx = kv_assemble(kv_ctx, p_ctx, tc, ts, tm=CTX_LEN, rope=False)
    k_all = jnp.concatenate([k_lat, k_ctx], axis=0)
    v_all = jnp.concatenate([v_lat, v_ctx], axis=0)
    s_tot = seq_len + CTX_LEN
    o_att = attention(q, k_all, v_all, tq=_pick(seq_len, (2048, 1024, 512, 256)),
                      tk=_pick(s_tot, (1280, 1024, 768, 640, 512, 384, 256, 128)))

    xbc_lat = dwconv3(p_lat, conv_w, conv_b, tm=512, tc=SSD_XBC, col0=P_XBC, width=SSD_XBC, act=True)
    xbc_ctx = dwconv3(p_ctx, conv_w, conv_b, tm=CTX_LEN, tc=SSD_XBC, col0=P_XBC, width=SSD_XBC, act=True)
    xbc_all = jnp.concatenate([xbc_lat, xbc_ctx], axis=0)
    dt_all = jnp.concatenate([p_lat[:, P_DT:P_DT + SSD_DT], p_ctx[:, P_DT:P_DT + SSD_DT]], axis=0)
    nch = s_tot // SSD_CHUNK
    dt2 = dt_all.reshape(s_tot, 2, SSD_HEADS).transpose(1, 0, 2)
    dt2t = dt2.reshape(2, nch, SSD_CHUNK, SSD_HEADS).transpose(0, 1, 3, 2)
    bias2 = dt_bias.reshape(2, 1, SSD_HEADS)
    bias2t = dt_bias.reshape(2, SSD_HEADS, 1)
    a_neg = -jnp.exp(a_log.astype(F32))
    a2 = a_neg.reshape(2, 1, SSD_HEADS)
    a2t = a_neg.reshape(2, SSD_HEADS, 1)
    lower = jnp.tril(jnp.ones((SSD_CHUNK, SSD_CHUNK), F32))
    tri2 = jnp.stack([lower, lower.T]).astype(BF16)
    yf, yb = ssd_scan(xbc_all, dt2, dt2t, bias2, bias2t, a2, a2t, tri2, n_lat_chunks=seq_len // SSD_CHUNK)

    dsk = jnp.repeat(d_skip[0] + d_skip[1], SSD_HEAD_DIM).reshape(1, SSD_INNER)
    return merge_out(o_att, yf, yb, xbc_lat, p_lat, dsk, ssd_norm_w.reshape(1, SSD_INNER), w_o.astype(BF16), x, g1,
                     tm=512, tn=512)


def _odd_layer(x, mods, norm_mix_w, w_in, b_in, short_w, short_b, fw1, fb1, fw_mid, fb_mid, freq, fw_out,
               fbias, w_out, b_out):
    seq_len, d = x.shape
    sh1, sc1, g1 = (mods[0:1, i * d:(i + 1) * d] for i in range(3))
    pc = normmm_conv(x, norm_mix_w, sh1, sc1, w_in.astype(BF16), b_in, short_w, short_b, tm=512, tn=768)

    tabs = _fft_tables(seq_len)
    bands = (HY_EMB - 1) // 2
    fband = jnp.linspace(1e-4, bands - 1, bands, dtype=F32)
    fvec = jnp.concatenate([jnp.zeros((1,), F32), fband, fband, jnp.zeros((128 - HY_EMB,), F32)]).reshape(1, 128)
    w1p = jnp.concatenate([fw1.astype(F32), jnp.zeros((128 - HY_EMB, HY_HID), F32)], axis=0).astype(BF16)
    lo = math.log(HY_SLOW_DECAY) / HY_TARGET
    hi = math.log(HY_FAST_DECAY) / HY_TARGET
    delta = jnp.abs(jnp.linspace(lo, hi, d, dtype=F32)).reshape(1, d)
    n_ord = fw_out.shape[1]
    wo = jnp.transpose(fw_out, (2, 0, 1, 3)).reshape(2, HY_HID, n_ord * d).astype(BF16)
    hid = filter_hidden(fvec, w1p, fb1.reshape(1, HY_HID), fw_mid.astype(BF16), fb_mid.reshape(-1, 1, HY_HID),
                        freq.reshape(1, HY_HID), seq_len=seq_len, tr=1024)
    ak, kabs = filter_stage1(tabs["f1"], hid.reshape(tabs["n1"], FFT_N2, HY_HID), wo, jnp.tile(delta, (1, n_ord)),
                             seq_len=seq_len, tn=256)
    ak3 = ak.reshape(tabs["nsp"], 2 * FFT_N2, n_ord * d)
    y_cols = 2
    y = None
    for i in range(n_ord):
        scale = 1.0 / kabs[:, i * d:(i + 1) * d]
        fb = fbias[i].reshape(1, d)
        if y is None:
            y = long_conv_gate(pc, y_cols, pc, i, fb, ak3, i, scale, tabs, seq_len=seq_len, d=d)
        else:
            y = long_conv_gate(y, 0, pc, i, fb, ak3, i, scale, tabs, seq_len=seq_len, d=d)
    y = y.reshape(seq_len, d)
    return mm_res(y, w_out.astype(BF16), b_out, x, g1, tm=512, tn=512)


def kernel(x, c, ctx, c_ctx, mod_w, mod_b, norm_mix_w, norm_ffn_w, ffn_w1, ffn_w3, ffn_w2, ev_w_in, ev_conv_w, ev_conv_b, ev_dt_bias, ev_a_log, ev_d_skip, ev_ssd_norm_w, ev_q_norm_w, ev_w_uq, ev_kv_norm_w, ev_w_ukv, ev_w_o, hy_w_in, hy_b_in, hy_short_w, hy_short_b, hy_fw1, hy_fb1, hy_fw_mid, hy_fb_mid, hy_freq, hy_fw_out, hy_fbias, hy_w_out, hy_b_out, final_norm_w):
    assert x.shape[0] == 1 and mod_w.shape[0] == 2
    xs = x[0]
    xc = ctx[0]
    d = xs.shape[1]
    vecs = jnp.concatenate([c.reshape(1, d), c_ctx.reshape(1, d), jnp.zeros((6, d), F32)], axis=0)
    depth = mod_w.shape[0]
    w1_b, w3_b, w2_b = ffn_w1.astype(BF16), ffn_w3.astype(BF16), ffn_w2.astype(BF16)
    for i in range(depth):
        mods = adaln_vectors(vecs, mod_w, mod_b, i)
        sh2, sc2, g2 = (mods[0:1, j * d:(j + 1) * d] for j in range(3, 6))
        if i % 2 == 0:
            e = i // 2
            xs = _even_layer(xs, xc, mods, norm_mix_w[i], ev_w_in[e], ev_conv_w[e], ev_conv_b[e], ev_dt_bias[e],
                             ev_a_log[e], ev_d_skip[e], ev_ssd_norm_w[e], ev_q_norm_w[e], ev_w_uq[e],
                             ev_kv_norm_w[e], ev_w_ukv[e], ev_w_o[e])
        else:
            o = i // 2
            xs = _odd_layer(xs, mods, norm_mix_w[i], hy_w_in[o], hy_b_in[o], hy_short_w[o], hy_short_b[o],
                            hy_fw1[o], hy_fb1[o], hy_fw_mid[o], hy_fb_mid[o], hy_freq[o], hy_fw_out[o],
                            hy_fbias[o], hy_w_out[o], hy_b_out[o])
        xs = ffn(xs, norm_ffn_w[i], sh2, sc2, g2, w1_b, w3_b, w2_b, final_norm_w, layer=i, tm=512, tf=512,
                 final_norm=(i == depth - 1))
    return xs[None]
```

```python
import functools
import math

import jax
import jax.numpy as jnp
from jax import lax
from jax.experimental import pallas as pl
from jax.experimental.pallas import tpu as pltpu

F32 = jnp.float32
BF16 = jnp.bfloat16

EPS = 1e-6
GRID_W = 64
CTX_LEN = 256
SSD_HEADS = 16
SSD_HEAD_DIM = 64
SSD_INNER = 1024
SSD_GROUPS = 2
SSD_HPG = 8
SSD_STATE = 128
SSD_CHUNK = 128
SSD_XBC = 1536
SSD_DT = 32
MLA_HEADS = 8
MLA_NOPE = 128
MLA_ROPE = 64
MLA_V = 128
MLA_Q_RANK = 512
MLA_KV_RANK = 512
ROPE_THETA = 10000.0
MLA_HEAD_PAD = 256
Q_SIDE = SSD_INNER + MLA_Q_RANK
HY_EMB = 33
HY_HID = 64
HY_FAST_DECAY = 0.3
HY_SLOW_DECAY = 1.5
HY_TARGET = 1e-2
FFT_N2 = 128
SLAB_TILE = 8

P_Z, P_CQ, P_XBC, P_CKV, P_KR, P_DT, P_COLS = 0, 1024, 1536, 3072, 3584, 3712, 3840

VMEM_LIMIT = 56 * 1024 * 1024


def _cparams(*sem):
    return pltpu.CompilerParams(dimension_semantics=sem, vmem_limit_bytes=VMEM_LIMIT)


def _pick(n, cands):
    for c in cands:
        if n % c == 0:
            return c
    raise ValueError(f"no tile for {n}")


def _split(a):
    hi = a.astype(BF16)
    lo = (a - hi.astype(F32)).astype(BF16)
    return hi, lo


def _dot(a, b):
    return jnp.dot(a, b, preferred_element_type=F32)


def _silu(x):
    return x * (1.0 / (1.0 + jnp.exp(-x)))


def _modnorm(x, nw, sh, sc):
    ms = jnp.mean(x * x, axis=-1, keepdims=True)
    return (x * lax.rsqrt(ms + EPS) * nw) * (1.0 + sc) + sh


def _matvec_kernel(x_ref, w_ref, b_ref, o_ref):
    x = x_ref[...]
    o_ref[...] = _dot(_silu(x).astype(BF16), w_ref[...].astype(BF16)) + b_ref[...]


def adaln_vectors(vecs, w_all, b_all, layer):
    nl, k, n = w_all.shape
    tn = 1024
    return pl.pallas_call(
        _matvec_kernel,
        grid=(n // tn,),
        in_specs=[pl.BlockSpec((8, k), lambda j: (0, 0)),
                  pl.BlockSpec((None, k, tn), lambda j: (layer, 0, j)),
                  pl.BlockSpec((None, 1, tn), lambda j: (layer, 0, j))],
        out_specs=pl.BlockSpec((8, tn), lambda j: (0, j)),
        out_shape=jax.ShapeDtypeStruct((8, n), F32),
        compiler_params=_cparams("arbitrary"),
        name="adaln_vectors",
    )(vecs, w_all, b_all.reshape(nl, 1, n))


def _normmm_kernel(x_ref, nw_ref, sh_ref, sc_ref, w_ref, b_ref, o_ref, xn_ref):
    @pl.when(pl.program_id(1) == 0)
    def _():
        xn_ref[...] = _modnorm(x_ref[...].astype(F32), nw_ref[...], sh_ref[...], sc_ref[...]).astype(BF16)

    o_ref[...] = (_dot(xn_ref[...], w_ref[...]) + b_ref[...]).astype(o_ref.dtype)


def normmm(x, nw, sh, sc, w, b, *, tm, tn, out_dtype, x_col=0):
    m = x.shape[0]
    k, n = w.shape
    return pl.pallas_call(
        _normmm_kernel,
        grid=(m // tm, n // tn),
        in_specs=[pl.BlockSpec((tm, k), lambda i, j: (i, x_col)),
                  pl.BlockSpec((1, k), lambda i, j: (0, 0)),
                  pl.BlockSpec((1, k), lambda i, j: (0, 0)),
                  pl.BlockSpec((1, k), lambda i, j: (0, 0)),
                  pl.BlockSpec((k, tn), lambda i, j: (0, j)),
                  pl.BlockSpec((1, tn), lambda i, j: (0, j))],
        out_specs=pl.BlockSpec((tm, tn), lambda i, j: (i, j)),
        out_shape=jax.ShapeDtypeStruct((m, n), out_dtype),
        scratch_shapes=[pltpu.VMEM((tm, k), BF16)],
        compiler_params=_cparams("parallel", "arbitrary"),
        name="normmm",
    )(x, nw.reshape(1, k), sh.reshape(1, k), sc.reshape(1, k), w, b.reshape(1, n))


def _normmm_conv_kernel(x_ref, xprev_ref, xnext_ref, nw_ref, sh_ref, sc_ref, w_ref, b_ref, cw_ref, cb_ref,
                        o_ref, xn_ref, xh_ref):
    i = pl.program_id(0)

    @pl.when(pl.program_id(1) == 0)
    def _():
        nw, sh, sc = nw_ref[...], sh_ref[...], sc_ref[...]
        xn_ref[...] = _modnorm(x_ref[...], nw, sh, sc).astype(BF16)
        xh_ref[0:8, :] = _modnorm(xprev_ref[...], nw, sh, sc).astype(BF16)
        xh_ref[8:16, :] = _modnorm(xnext_ref[...], nw, sh, sc).astype(BF16)

    w = w_ref[...]
    b = b_ref[...]
    y = _dot(xn_ref[...], w) + b
    yh = _dot(xh_ref[...], w) + b
    prev_row = jnp.where(i == 0, 0.0, yh[7:8, :])
    next_row = jnp.where(i == pl.num_programs(0) - 1, 0.0, yh[8:9, :])
    yp, yn = _shift_rows(y, prev_row, next_row)
    o_ref[...] = yp * cw_ref[0:1, :] + y * cw_ref[1:2, :] + yn * cw_ref[2:3, :] + cb_ref[...]


def normmm_conv(x, nw, sh, sc, w, b, cw, cb, *, tm, tn):
    m, k = x.shape
    n = w.shape[1]
    nrb = m // 8
    tb = tm // 8
    vec = pl.BlockSpec((1, k), lambda i, j: (0, 0))
    return pl.pallas_call(
        _normmm_conv_kernel,
        grid=(m // tm, n // tn),
        in_specs=[pl.BlockSpec((tm, k), lambda i, j: (i, 0)),
                  pl.BlockSpec((8, k), lambda i, j: (jnp.maximum(i * tb - 1, 0), 0)),
                  pl.BlockSpec((8, k), lambda i, j: (jnp.minimum((i + 1) * tb, nrb - 1), 0)),
                  vec, vec, vec,
                  pl.BlockSpec((k, tn), lambda i, j: (0, j)),
                  pl.BlockSpec((1, tn), lambda i, j: (0, j)),
                  pl.BlockSpec((3, tn), lambda i, j: (0, j)),
                  pl.BlockSpec((1, tn), lambda i, j: (0, j))],
        out_specs=pl.BlockSpec((tm, tn), lambda i, j: (i, j)),
        out_shape=jax.ShapeDtypeStruct((m, n), F32),
        scratch_shapes=[pltpu.VMEM((tm, k), BF16), pltpu.VMEM((16, k), BF16)],
        compiler_params=_cparams("parallel", "arbitrary"),
        name="normmm_conv",
    )(x, x, x, nw.reshape(1, k), sh.reshape(1, k), sc.reshape(1, k), w, b.reshape(1, n), cw, cb.reshape(1, n))


def _qproj_kernel(x_ref, nw_ref, wa_ref, wb_ref, ta_ref, tb_ref, o_ref):
    x = x_ref[...]
    ms = jnp.mean(x * x, axis=-1, keepdims=True)
    xn = (x * lax.rsqrt(ms + EPS) * nw_ref[...]).astype(BF16)
    ta = ta_ref[...]
    tb = tb_ref[...]
    hp = MLA_HEAD_PAD
    for h in range(MLA_HEADS):
        cols = slice(h * hp, (h + 1) * hp)
        o_ref[:, cols] = (_dot(xn, wa_ref[:, cols]) * ta + _dot(xn, wb_ref[:, cols]) * tb).astype(o_ref.dtype)


def qproj(p, nw, wa, wb, ta, tb, *, tm):
    m = ta.shape[0]
    k = MLA_Q_RANK
    hp = MLA_HEAD_PAD
    n = MLA_HEADS * hp
    return pl.pallas_call(
        _qproj_kernel,
        grid=(m // tm,),
        in_specs=[pl.BlockSpec((tm, k), lambda i: (i, P_CQ // MLA_Q_RANK)),
                  pl.BlockSpec((1, k), lambda i: (0, 0)),
                  pl.BlockSpec((k, n), lambda i: (0, 0)),
                  pl.BlockSpec((k, n), lambda i: (0, 0)),
                  pl.BlockSpec((tm, hp), lambda i: (i, 0)),
                  pl.BlockSpec((tm, hp), lambda i: (i, 0))],
        out_specs=pl.BlockSpec((tm, n), lambda i: (i, 0)),
        out_shape=jax.ShapeDtypeStruct((m, n), BF16),
        compiler_params=_cparams("parallel"),
        name="mla_qproj",
    )(p, nw.reshape(1, k), wa, wb, ta, tb)


def _kv_assemble_kernel(kv_ref, kr_ref, tc_ref, ts_ref, k_ref, v_ref, *, rope):
    krr = kr_ref[...]
    kr = krr[:, :MLA_ROPE]
    if rope:
        kr = kr * tc_ref[...][:, :MLA_ROPE] + krr[:, MLA_ROPE:] * ts_ref[...][:, :MLA_ROPE]
    tail = jnp.concatenate([kr, jnp.zeros_like(kr)], axis=-1).astype(BF16)
    for h in range(MLA_HEADS):
        base = h * (MLA_NOPE + MLA_V)
        k_ref[:, h * MLA_HEAD_PAD:h * MLA_HEAD_PAD + MLA_NOPE] = kv_ref[:, base:base + MLA_NOPE]
        k_ref[:, h * MLA_HEAD_PAD + MLA_NOPE:(h + 1) * MLA_HEAD_PAD] = tail
        v_ref[:, h * MLA_V:(h + 1) * MLA_V] = kv_ref[:, base + MLA_NOPE:base + MLA_NOPE + MLA_V]


def kv_assemble(kvp, p, tc, ts, *, tm, rope):
    m = kvp.shape[0]
    return pl.pallas_call(
        functools.partial(_kv_assemble_kernel, rope=rope),
        grid=(m // tm,),
        in_specs=[pl.BlockSpec((tm, MLA_HEADS * (MLA_NOPE + MLA_V)), lambda i: (i, 0)),
                  pl.BlockSpec((tm, 128), lambda i: (i, P_KR // 128)),
                  pl.BlockSpec((tm, 128), lambda i: (i, 0)),
                  pl.BlockSpec((tm, 128), lambda i: (i, 0))],
        out_specs=[pl.BlockSpec((tm, MLA_HEADS * MLA_HEAD_PAD), lambda i: (i, 0)),
                   pl.BlockSpec((tm, MLA_HEADS * MLA_V), lambda i: (i, 0))],
        out_shape=[jax.ShapeDtypeStruct((m, MLA_HEADS * MLA_HEAD_PAD), BF16),
                   jax.ShapeDtypeStruct((m, MLA_HEADS * MLA_V), BF16)],
        compiler_params=_cparams("parallel"),
        name="mla_kv_assemble",
    )(kvp, p, tc, ts)


def _attn_kernel(q_ref, k_ref, v_ref, o_ref, s0_ref, s1_ref, m_ref, acc_ref, *, nk):
    j = pl.program_id(2)

    def qk(s_ref):
        s_ref[...] = lax.dot_general(q_ref[...], k_ref[...], (((1,), (1,)), ((), ())), preferred_element_type=F32)

    def softmax_pv(s_ref):
        s = s_ref[...]
        m_prev = m_ref[...]
        m_new = jnp.maximum(m_prev, jnp.max(s, axis=-1, keepdims=True))
        alpha = jnp.exp2(m_prev - m_new)
        p = jnp.exp2(s - m_new).astype(BF16)
        v = v_ref[...]
        lane = lax.broadcasted_iota(jnp.int32, v.shape, 1)
        ones_col = jnp.where(lane == 0, 1.0, 0.0).astype(BF16)
        v_ext = jnp.concatenate([v, ones_col], axis=1)
        acc_ref[...] = alpha * acc_ref[...] + _dot(p, v_ext)
        m_ref[...] = m_new

    @pl.when(j == 0)
    def _():
        m_ref[...] = jnp.full(m_ref.shape, -jnp.inf, F32)
        acc_ref[...] = jnp.zeros(acc_ref.shape, F32)
        qk(s0_ref)

    mid = jnp.logical_and(j > 0, j < nk)

    @pl.when(jnp.logical_and(mid, j % 2 == 1))
    def _():
        softmax_pv(s0_ref)
        qk(s1_ref)

    @pl.when(jnp.logical_and(mid, j % 2 == 0))
    def _():
        softmax_pv(s1_ref)
        qk(s0_ref)

    @pl.when(j == nk)
    def _():
        softmax_pv(s1_ref if (nk - 1) % 2 else s0_ref)
        acc = acc_ref[...]
        o_ref[...] = (acc[:, :MLA_V] / acc[:, MLA_V:MLA_V + 1]).astype(o_ref.dtype)


def attention(q, k, v, *, tq, tk):
    lq = q.shape[0]
    s = k.shape[0]
    nk = s // tk
    return pl.pallas_call(
        functools.partial(_attn_kernel, nk=nk),
        grid=(MLA_HEADS, lq // tq, nk + 1),
        in_specs=[pl.BlockSpec((tq, MLA_HEAD_PAD), lambda h, i, j: (i, h)),
                  pl.BlockSpec((tk, MLA_HEAD_PAD), lambda h, i, j: (jnp.minimum(j, nk - 1), h)),
                  pl.BlockSpec((tk, MLA_V), lambda h, i, j: (jnp.maximum(j - 1, 0), h))],
        out_specs=pl.BlockSpec((tq, MLA_V), lambda h, i, j: (i, h)),
        out_shape=jax.ShapeDtypeStruct((lq, MLA_HEADS * MLA_V), BF16),
        scratch_shapes=[pltpu.VMEM((tq, tk), F32), pltpu.VMEM((tq, tk), F32), pltpu.VMEM((tq, 1), F32),
                        pltpu.VMEM((tq, 2 * MLA_V), F32)],
        compiler_params=_cparams("parallel", "parallel", "arbitrary"),
        name="mla_attention",
    )(q, k, v)


def _shift_rows(x, prev_row, next_row):
    tm = x.shape[0]
    rows = lax.broadcasted_iota(jnp.int32, x.shape, 0)
    xp = jnp.where(rows == 0, prev_row, pltpu.roll(x, 1, 0))
    xn = jnp.where(rows == tm - 1, next_row, pltpu.roll(x, tm - 1, 0))
    return xp, xn


def _dwconv_kernel(x_ref, xprev_ref, xnext_ref, w_ref, b_ref, o_ref, *, act):
    i = pl.program_id(0)
    x = x_ref[...]
    prev_row = jnp.where(i == 0, 0.0, xprev_ref[7:8, :])
    next_row = jnp.where(i == pl.num_programs(0) - 1, 0.0, xnext_ref[0:1, :])
    xp, xn = _shift_rows(x, prev_row, next_row)
    y = xp * w_ref[0:1, :] + x * w_ref[1:2, :] + xn * w_ref[2:3, :] + b_ref[...]
    if act:
        y = _silu(y)
    o_ref[...] = y


def dwconv3(x, w, b, *, tm, tc, col0, width, act):
    m = x.shape[0]
    cb = col0 // tc
    nrb = m // 8
    tb = tm // 8
    return pl.pallas_call(
        functools.partial(_dwconv_kernel, act=act),
        grid=(m // tm, width // tc),
        in_specs=[pl.BlockSpec((tm, tc), lambda i, j: (i, cb + j)),
                  pl.BlockSpec((8, tc), lambda i, j: (jnp.maximum(i * tb - 1, 0), cb + j)),
                  pl.BlockSpec((8, tc), lambda i, j: (jnp.minimum((i + 1) * tb, nrb - 1), cb + j)),
                  pl.BlockSpec((3, tc), lambda i, j: (0, j)),
                  pl.BlockSpec((1, tc), lambda i, j: (0, j))],
        out_specs=pl.BlockSpec((tm, tc), lambda i, j: (i, j)),
        out_shape=jax.ShapeDtypeStruct((m, width), F32),
        compiler_params=_cparams("arbitrary", "arbitrary"),
        name="dwconv3",
    )(x, x, x, w, b.reshape(1, width))


def _softplus(x):
    return jnp.maximum(x, 0.0) + jnp.log(1.0 + jnp.exp(-jnp.abs(x)))


def _ssd_prepare(d, xbc, dt_ref, dtt_ref, bias_ref, biast_ref, a_ref, at_ref, tri_ref):
    q = SSD_CHUNK
    dt = _softplus(dt_ref[0] + bias_ref[d])
    dtt = _softplus(dtt_ref[0, 0] + biast_ref[d])
    dta_hi, dta_lo = _split(dt * a_ref[d])
    dtat_hi, dtat_lo = _split(dtt * at_ref[d])
    tri = tri_ref[d]
    trit = tri_ref[1 - d]
    acum = _dot(tri, dta_hi) + _dot(tri, dta_lo)
    acumt = _dot(dtat_hi, trit) + _dot(dtat_lo, trit)
    total = acum[q - 1:q, :] if d == 0 else acum[0:1, :]
    return dict(xbc=xbc, dt=dt, acum=acum, acumt=acumt, to_end=jnp.exp(total - acum),
                from_start=jnp.exp(acum), chunk_decay=jnp.exp(total), mask=tri > 0.5)


def _per_head_lanes(a, g):
    return jnp.concatenate([jnp.broadcast_to(a[:, h:h + 1], (a.shape[0], SSD_HEAD_DIM))
                            for h in range(g * SSD_HPG, (g + 1) * SSD_HPG)], axis=1)


def _ssd_group(d, p, g, h_ref):
    xbc = p["xbc"]
    gw = SSD_HPG * SSD_HEAD_DIM
    bm = xbc[:, SSD_INNER + g * SSD_STATE:SSD_INNER + (g + 1) * SSD_STATE]
    cm = xbc[:, SSD_INNER + (SSD_GROUPS + g) * SSD_STATE:SSD_INNER + (SSD_GROUPS + g + 1) * SSD_STATE]
    bm_b = bm.astype(BF16)
    cm_b = cm.astype(BF16)
    cb = lax.dot_general(cm_b, bm_b, (((1,), (1,)), ((), ())), preferred_element_type=F32)
    xdt = xbc[:, g * gw:(g + 1) * gw] * _per_head_lanes(p["dt"], g)
    h_prev = h_ref[d * SSD_GROUPS + g]
    y_off = _dot(cm_b, h_prev.astype(BF16)) * _per_head_lanes(p["from_start"], g)
    states = _dot(bm.T.astype(BF16), (xdt * _per_head_lanes(p["to_end"], g)).astype(BF16))
    h_ref[d * SSD_GROUPS + g] = h_prev * _per_head_lanes(p["chunk_decay"], g) + states
    return dict(cb=cb, xdt_b=xdt.astype(BF16), y_off=y_off)


def _ssd_head(p, grp, g, r, o_ref):
    h = g * SSD_HPG + r
    seg = p["acum"][:, h:h + 1] - p["acumt"][h:h + 1, :]
    decay = jnp.exp(jnp.where(p["mask"], seg, -jnp.inf))
    lanes = slice(r * SSD_HEAD_DIM, (r + 1) * SSD_HEAD_DIM)
    y_diag = _dot((grp["cb"] * decay).astype(BF16), grp["xdt_b"][:, lanes])
    o_ref[:, h * SSD_HEAD_DIM:(h + 1) * SSD_HEAD_DIM] = y_diag + grp["y_off"][:, lanes]


def _ssd_kernel(xf_ref, xfc_ref, xb_ref, xbc_ref, dtf_ref, dtb_ref, dttf_ref, dttb_ref, bias_ref, biast_ref, a_ref, at_ref, tri_ref,
                of_ref, ob_ref, h_ref):
    @pl.when(pl.program_id(0) == 0)
    def _():
        h_ref[...] = jnp.zeros(h_ref.shape, F32)

    s = pl.program_id(0)
    ncx = CTX_LEN // SSD_CHUNK
    x_fwd = jnp.where(s < ncx, xfc_ref[...], xf_ref[...])
    x_bwd = jnp.where(s < ncx, xbc_ref[...], xb_ref[...])
    prm = (_ssd_prepare(0, x_fwd, dtf_ref, dttf_ref, bias_ref, biast_ref, a_ref, at_ref, tri_ref),
           _ssd_prepare(1, x_bwd, dtb_ref, dttb_ref, bias_ref, biast_ref, a_ref, at_ref, tri_ref))
    outs = (of_ref, ob_ref)
    for g in range(SSD_GROUPS):
        grp = (_ssd_group(0, prm[0], g, h_ref), _ssd_group(1, prm[1], g, h_ref))
        for r in range(SSD_HPG):
            for d in range(2):
                _ssd_head(prm[d], grp[d], g, r, outs[d])


def ssd_scan(xbc_lat, xbc_ctx, dt2, dt2t, bias2, bias2t, a2, a2t, tri2, *, n_lat_chunks):
    nc = n_lat_chunks
    ncx = CTX_LEN // SSD_CHUNK
    tot = nc + ncx
    q = SSD_CHUNK

    def cf(s):
        return lax.rem(s + nc, tot)

    def cbk(s):
        return tot - 1 - s

    def full(shape):
        return pl.BlockSpec(shape, lambda s: (0,) * len(shape))

    return pl.pallas_call(
        _ssd_kernel,
        grid=(tot,),
        in_specs=[pl.BlockSpec((q, SSD_XBC), lambda s: (jnp.minimum(cf(s), nc - 1), 0)),
                  pl.BlockSpec((q, SSD_XBC), lambda s: (jnp.maximum(cf(s) - nc, 0), 0)),
                  pl.BlockSpec((q, SSD_XBC), lambda s: (jnp.minimum(cbk(s), nc - 1), 0)),
                  pl.BlockSpec((q, SSD_XBC), lambda s: (jnp.maximum(cbk(s) - nc, 0), 0)),
                  pl.BlockSpec((1, q, SSD_HEADS), lambda s: (0, cf(s), 0)),
                  pl.BlockSpec((1, q, SSD_HEADS), lambda s: (1, cbk(s), 0)),
                  pl.BlockSpec((1, 1, SSD_HEADS, q), lambda s: (0, cf(s), 0, 0)),
                  pl.BlockSpec((1, 1, SSD_HEADS, q), lambda s: (1, cbk(s), 0, 0)),
                  full((2, 1, SSD_HEADS)), full((2, SSD_HEADS, 1)), full((2, 1, SSD_HEADS)),
                  full((2, SSD_HEADS, 1)), full((2, q, q))],
        out_specs=[pl.BlockSpec((q, SSD_INNER), lambda s: (jnp.where(cf(s) >= nc, 0, cf(s)), 0)),
                   pl.BlockSpec((q, SSD_INNER), lambda s: (jnp.where(cbk(s) >= nc, nc - 1, cbk(s)), 0))],
        out_shape=[jax.ShapeDtypeStruct((nc * q, SSD_INNER), F32), jax.ShapeDtypeStruct((nc * q, SSD_INNER), F32)],
        scratch_shapes=[pltpu.VMEM((2 * SSD_GROUPS, SSD_STATE, SSD_HPG * SSD_HEAD_DIM), F32)],
        compiler_params=_cparams("arbitrary"),
        name="ssd_scan",
    )(xbc_lat, xbc_ctx, xbc_lat, xbc_ctx, dt2, dt2, dt2t, dt2t, bias2, bias2t, a2, a2t, tri2)


def _merge_kernel(o_ref, yf_ref, yb_ref, xs_ref, z_ref, dsk_ref, nw_ref, w_ref, x_ref, g_ref, out_ref, a_ref):
    @pl.when(pl.program_id(1) == 0)
    def _():
        a_ref[:, :MLA_HEADS * MLA_V] = o_ref[...]
        y = yf_ref[...] + yb_ref[...] + dsk_ref[...] * xs_ref[...]
        gy = y * _silu(z_ref[...])
        gw = SSD_INNER // SSD_GROUPS
        for g in range(SSD_GROUPS):
            part = gy[:, g * gw:(g + 1) * gw]
            ms = jnp.mean(part * part, axis=-1, keepdims=True)
            a_ref[:, MLA_HEADS * MLA_V + g * gw:MLA_HEADS * MLA_V + (g + 1) * gw] = (
                part * lax.rsqrt(ms + EPS) * nw_ref[:, g * gw:(g + 1) * gw]).astype(BF16)

    out_ref[...] = x_ref[...] + g_ref[...] * _dot(a_ref[...], w_ref[...])


def merge_out(o_att, yf, yb, xbc, p, dsk, nw, w_o, x, gate, *, tm, tn):
    m, n = x.shape
    kw = w_o.shape[0]
    return pl.pallas_call(
        _merge_kernel,
        grid=(m // tm, n // tn),
        in_specs=[pl.BlockSpec((tm, MLA_HEADS * MLA_V), lambda i, j: (i, 0)),
                  pl.BlockSpec((tm, SSD_INNER), lambda i, j: (i, 0)),
                  pl.BlockSpec((tm, SSD_INNER), lambda i, j: (i, 0)),
                  pl.BlockSpec((tm, SSD_INNER), lambda i, j: (i, 0)),
                  pl.BlockSpec((tm, SSD_INNER), lambda i, j: (i, 0)),
                  pl.BlockSpec((1, SSD_INNER), lambda i, j: (0, 0)),
                  pl.BlockSpec((1, SSD_INNER), lambda i, j: (0, 0)),
                  pl.BlockSpec((kw, tn), lambda i, j: (0, j)),
                  pl.BlockSpec((tm, tn), lambda i, j: (i, j)),
                  pl.BlockSpec((1, tn), lambda i, j: (0, j))],
        out_specs=pl.BlockSpec((tm, tn), lambda i, j: (i, j)),
        out_shape=jax.ShapeDtypeStruct((m, n), F32),
        scratch_shapes=[pltpu.VMEM((tm, kw), BF16)],
        compiler_params=_cparams("parallel", "arbitrary"),
        name="merge_out",
    )(o_att, yf, yb, xbc, p, dsk, nw, w_o, x, gate)


def _mmres_kernel(a_ref, w_ref, b_ref, x_ref, g_ref, o_ref):
    o_ref[...] = x_ref[...] + g_ref[...] * (_dot(a_ref[...].astype(BF16), w_ref[...]) + b_ref[...])


def mm_res(a, w, b, x, gate, *, tm, tn):
    m, k = a.shape
    n = w.shape[1]
    return pl.pallas_call(
        _mmres_kernel,
        grid=(m // tm, n // tn),
        in_specs=[pl.BlockSpec((tm, k), lambda i, j: (i, 0)),
                  pl.BlockSpec((k, tn), lambda i, j: (0, j)),
                  pl.BlockSpec((1, tn), lambda i, j: (0, j)),
                  pl.BlockSpec((tm, tn), lambda i, j: (i, j)),
                  pl.BlockSpec((1, tn), lambda i, j: (0, j))],
        out_specs=pl.BlockSpec((tm, tn), lambda i, j: (i, j)),
        out_shape=jax.ShapeDtypeStruct((m, n), F32),
        compiler_params=_cparams("parallel", "arbitrary"),
        name="mm_res",
    )(a, w, b.reshape(1, n), x, gate)


def _ffn_kernel(x_ref, nw_ref, sh_ref, sc_ref, g_ref, w1_ref, w3_ref, w2_ref, fw_ref, o_ref, xn_ref, acc_ref,
                *, final_norm):
    f = pl.program_id(1)

    @pl.when(f == 0)
    def _():
        xn_ref[...] = _modnorm(x_ref[...], nw_ref[...], sh_ref[...], sc_ref[...]).astype(BF16)
        acc_ref[...] = jnp.zeros(acc_ref.shape, F32)

    xn = xn_ref[...]
    a = _dot(xn, w1_ref[...])
    b = _dot(xn, w3_ref[...])
    acc_ref[...] += _dot((_silu(a) * b).astype(BF16), w2_ref[...])

    @pl.when(f == pl.num_programs(1) - 1)
    def _():
        y = x_ref[...] + g_ref[...] * acc_ref[...]
        if final_norm:
            ms = jnp.mean(y * y, axis=-1, keepdims=True)
            y = y * lax.rsqrt(ms + EPS) * fw_ref[...]
        o_ref[...] = y


def ffn(x, nw, sh, sc, gate, w1, w3, w2, fw, *, layer, tm, tf, final_norm):
    m, dm = x.shape
    dff = w1.shape[2]
    vec = pl.BlockSpec((1, dm), lambda i, f: (0, 0))
    return pl.pallas_call(
        functools.partial(_ffn_kernel, final_norm=final_norm),
        grid=(m // tm, dff // tf),
        in_specs=[pl.BlockSpec((tm, dm), lambda i, f: (i, 0)), vec, vec, vec, vec,
                  pl.BlockSpec((None, dm, tf), lambda i, f: (layer, 0, f)),
                  pl.BlockSpec((None, dm, tf), lambda i, f: (layer, 0, f)),
                  pl.BlockSpec((None, tf, dm), lambda i, f: (layer, f, 0)),
                  vec],
        out_specs=pl.BlockSpec((tm, dm), lambda i, f: (i, 0)),
        out_shape=jax.ShapeDtypeStruct((m, dm), F32),
        scratch_shapes=[pltpu.VMEM((tm, dm), BF16), pltpu.VMEM((tm, dm), F32)],
        compiler_params=_cparams("parallel", "arbitrary"),
        name="ffn",
    )(x, nw.reshape(1, dm), sh, sc, gate, w1, w3, w2, fw.reshape(1, dm))


def _filter_hidden_kernel(fvec_ref, w1_ref, b1_ref, wm_ref, bm_ref, freq_ref, h_ref, *, seq_len, tr):
    i = pl.program_id(0)
    n = i * tr + lax.broadcasted_iota(jnp.int32, (tr, 1), 0)
    lag = jnp.where(n < seq_len, n, 2 * seq_len - n).astype(F32)
    t = lag / float(seq_len - 1)
    ang = lag * (2.0 * math.pi / seq_len)
    lane = lax.broadcasted_iota(jnp.int32, (tr, 128), 1)
    bands = (HY_EMB - 1) // 2
    arg = ang * fvec_ref[...]
    emb = jnp.where(lane == 0, t, jnp.where(lane <= bands, jnp.cos(arg), jnp.where(lane <= 2 * bands, -jnp.sin(arg), 0.0)))
    fr = freq_ref[...]
    hid = jnp.sin(fr * (_dot(emb.astype(BF16), w1_ref[...]) + b1_ref[...]))
    for j in range(wm_ref.shape[0]):
        hid = jnp.sin(fr * (_dot(hid.astype(BF16), wm_ref[j]) + bm_ref[j]))
    h_ref[...] = hid


def filter_hidden(fvec, w1, b1, wm, bm, freq, *, seq_len, tr):
    full = lambda shp: pl.BlockSpec(shp, lambda i: (0,) * len(shp))
    return pl.pallas_call(
        functools.partial(_filter_hidden_kernel, seq_len=seq_len, tr=tr),
        grid=(2 * seq_len // tr,),
        in_specs=[full((1, 128)), full(w1.shape), full(b1.shape), full(wm.shape), full(bm.shape), full(freq.shape)],
        out_specs=pl.BlockSpec((tr, HY_HID), lambda i: (i, 0)),
        out_shape=jax.ShapeDtypeStruct((2 * seq_len, HY_HID), F32),
        compiler_params=_cparams("parallel"),
        name="hyena_filter_hidden",
    )(fvec, w1, b1, wm, bm, freq)


ROW_GROUP = 16


def _filter_stage1_kernel(f_ref, hid_ref, wo_ref, delta_ref, o_ref, abs_ref, hs_ref, os_ref, *, seq_len):
    i = pl.program_id(1)
    n1 = hid_ref.shape[0]
    half = n1 // 2
    n1_idx = lax.broadcasted_iota(jnp.int32, (n1, 1), 0)
    delta = delta_ref[...]
    f = f_ref[...]
    wo_past, wo_future = wo_ref[0], wo_ref[1]

    @pl.when(i == 0)
    def _():
        abs_ref[...] = jnp.zeros(abs_ref.shape, F32)

    asum = jnp.zeros(abs_ref.shape, F32)
    for r in range(ROW_GROUP):
        hs_ref[...] = hid_ref[:, r, :]
        hb = hs_ref[...].astype(BF16)
        k = jnp.concatenate([_dot(hb[:half], wo_past), _dot(hb[half:], wo_future)], axis=0)
        n = n1_idx * FFT_N2 + (i * ROW_GROUP + r)
        lag = jnp.where(n < seq_len, n, 2 * seq_len - n).astype(F32)
        k = k * jnp.exp(-(lag / float(seq_len - 1)) * delta)
        k = jnp.where(n == seq_len, 0.0, k)
        asum = asum + jnp.sum(jnp.abs(k), axis=0, keepdims=True)
        os_ref[:, r, :] = _dot(f, k.astype(BF16))
    abs_ref[...] += asum
    o_ref[...] = os_ref[...].astype(o_ref.dtype)


def filter_stage1(f1, hid3, wo, delta, *, seq_len, tn):
    mrows, n1 = f1.shape
    dd = delta.shape[1]
    return pl.pallas_call(
        functools.partial(_filter_stage1_kernel, seq_len=seq_len),
        grid=(dd // tn, FFT_N2 // ROW_GROUP),
        in_specs=[pl.BlockSpec((mrows, n1), lambda j, i: (0, 0)),
                  pl.BlockSpec((n1, ROW_GROUP, HY_HID), lambda j, i: (0, i, 0)),
                  pl.BlockSpec((2, HY_HID, tn), lambda j, i: (0, 0, j)),
                  pl.BlockSpec((1, tn), lambda j, i: (0, j))],
        out_specs=[pl.BlockSpec((mrows, ROW_GROUP, tn), lambda j, i: (0, i, j)),
                   pl.BlockSpec((1, tn), lambda j, i: (0, j))],
        out_shape=[jax.ShapeDtypeStruct((mrows, FFT_N2, dd), BF16), jax.ShapeDtypeStruct((1, dd), F32)],
        scratch_shapes=[pltpu.VMEM((n1, HY_HID), F32), pltpu.VMEM((mrows, ROW_GROUP, tn), F32)],
        compiler_params=_cparams("parallel", "arbitrary"),
        name="filter_stage1",
    )(f1, hid3, wo, delta)


def _fft1_kernel(f_ref, x_ref, o_ref, xs_ref, os_ref):
    for r in range(ROW_GROUP):
        xs_ref[...] = x_ref[:, r, :]
        os_ref[:, r, :] = _dot(f_ref[...], xs_ref[...].astype(BF16))
    o_ref[...] = os_ref[...].astype(o_ref.dtype)


def _fft1_inv_kernel(g_ref, b_ref, v_ref, gate_ref, fb_ref, o_ref, bf_ref, xs_ref):
    bf_ref[...] = b_ref[...].astype(F32)
    for r in range(ROW_GROUP):
        xs_ref[...] = bf_ref[:, r, :]
        o_ref[:, r, :] = _dot(g_ref[...], xs_ref[...].astype(BF16))
    o_ref[...] = gate_ref[...] * (o_ref[...] + fb_ref[...] * v_ref[...])


def fft_stage1(f1, x3, *, d, tn, col_ofs=0):
    mrows, k = f1.shape
    n2 = x3.shape[1]
    nd = d // tn
    return pl.pallas_call(
        _fft1_kernel,
        grid=(n2 // ROW_GROUP, nd),
        in_specs=[pl.BlockSpec((mrows, k), lambda i, j: (0, 0)),
                  pl.BlockSpec((k, ROW_GROUP, tn), lambda i, j: (0, i, col_ofs * nd + j))],
        out_specs=pl.BlockSpec((mrows, ROW_GROUP, tn), lambda i, j: (0, i, j)),
        out_shape=jax.ShapeDtypeStruct((mrows, n2, d), BF16),
        scratch_shapes=[pltpu.VMEM((k, tn), F32), pltpu.VMEM((mrows, ROW_GROUP, tn), F32)],
        compiler_params=_cparams("parallel", "arbitrary"),
        name="fft_stage1",
    )(f1, x3)


def fft_stage1_inv(g1, b3, v3, v_ofs, gate3, gate_ofs, fbias, *, d, tn):
    mrows, k = g1.shape
    n2 = b3.shape[1]
    nd = d // tn
    return pl.pallas_call(
        _fft1_inv_kernel,
        grid=(n2 // ROW_GROUP, nd),
        in_specs=[pl.BlockSpec((mrows, k), lambda i, j: (0, 0)),
                  pl.BlockSpec((k, ROW_GROUP, tn), lambda i, j: (0, i, j)),
                  pl.BlockSpec((mrows, ROW_GROUP, tn), lambda i, j: (0, i, v_ofs * nd + j)),
                  pl.BlockSpec((mrows, ROW_GROUP, tn), lambda i, j: (0, i, gate_ofs * nd + j)),
                  pl.BlockSpec((1, tn), lambda i, j: (0, j))],
        out_specs=pl.BlockSpec((mrows, ROW_GROUP, tn), lambda i, j: (0, i, j)),
        out_shape=jax.ShapeDtypeStruct((mrows, n2, d), F32),
        scratch_shapes=[pltpu.VMEM((k, ROW_GROUP, tn), F32), pltpu.VMEM((k, tn), F32)],
        compiler_params=_cparams("parallel", "arbitrary"),
        name="fft_stage1_inv",
    )(g1, b3, v3, gate3, fbias)


def _fftmid_kernel(a_ref, ak_ref, mf_ref, mi_ref, scale_ref, o_ref):
    half = FFT_N2
    scale = scale_ref[...]
    for s in range(SLAB_TILE):
        mf = mf_ref[s]
        y = _dot(mf, a_ref[s])
        kf = _dot(mf, ak_ref[s])
        yr, yi = y[:half], y[half:]
        kr, ki = kf[:half], kf[half:]
        pr = (yr * kr - yi * ki) * scale
        pi = (yr * ki + yi * kr) * scale
        o_ref[s] = _dot(mi_ref[s], jnp.concatenate([pr, pi], axis=0).astype(BF16)).astype(o_ref.dtype)


def fft_mid(a3, ak3, ak_ofs, mf, mi, scale, *, dc):
    nsp, rows, d = a3.shape
    nd = d // dc
    blk = pl.BlockSpec((SLAB_TILE, rows, dc), lambda s, j: (s, 0, j))
    mat = pl.BlockSpec((SLAB_TILE, rows, rows), lambda s, j: (s, 0, 0))
    return pl.pallas_call(
        _fftmid_kernel,
        grid=(nsp // SLAB_TILE, nd),
        in_specs=[blk, pl.BlockSpec((SLAB_TILE, rows, dc), lambda s, j: (s, 0, ak_ofs * nd + j)), mat, mat,
                  pl.BlockSpec((1, dc), lambda s, j: (0, j))],
        out_specs=blk,
        out_shape=jax.ShapeDtypeStruct((nsp, rows, d), BF16),
        compiler_params=_cparams("parallel", "arbitrary"),
        name="fft_mid",
    )(a3, ak3, mf, mi, scale)


def _fft_tables(seq_len):
    n = 2 * seq_len
    n2 = FFT_N2
    n1 = n // n2
    ns = n1 // 2 + 1
    nsp = -(-ns // SLAB_TILE) * SLAB_TILE
    k1 = jnp.arange(nsp, dtype=jnp.int32)
    valid = (k1 < ns)
    m1 = jnp.arange(n1, dtype=jnp.int32)
    ang1 = (2.0 * math.pi / n1) * lax.rem(k1[:, None] * m1[None, :], n1).astype(F32)
    vf = valid[:, None].astype(F32)
    f1 = jnp.stack([jnp.cos(ang1) * vf, -jnp.sin(ang1) * vf], axis=1).reshape(2 * nsp, n1)
    wgt = jnp.where((k1 == 0) | (k1 == n1 // 2), 1.0, 2.0) * valid.astype(F32) / n
    g1 = jnp.stack([jnp.cos(ang1) * wgt[:, None], -jnp.sin(ang1) * wgt[:, None]], axis=1)
    g1 = g1.reshape(2 * nsp, n1).T[: n1 // 2]
    k2 = jnp.arange(n2, dtype=jnp.int32)
    m2 = jnp.arange(n2, dtype=jnp.int32)
    f = k1[:, None, None] + n1 * k2[None, :, None]
    ang2 = (2.0 * math.pi / n) * lax.rem(f * m2[None, None, :], n).astype(F32)
    vm = valid[:, None, None].astype(F32)
    c2, s2 = jnp.cos(ang2) * vm, jnp.sin(ang2) * vm
    mf = jnp.concatenate([jnp.concatenate([c2, s2], axis=2), jnp.concatenate([-s2, c2], axis=2)], axis=1)
    c2t, s2t = jnp.swapaxes(c2, 1, 2), jnp.swapaxes(s2, 1, 2)
    mi = jnp.concatenate([jnp.concatenate([c2t, -s2t], axis=2), jnp.concatenate([s2t, c2t], axis=2)], axis=1)
    return dict(n1=n1, nsp=nsp, f1=f1.astype(BF16), g1=g1.astype(BF16), mf=mf.astype(BF16), mi=mi.astype(BF16))


def long_conv_gate(v, v_cols, gate, gate_cols, fbias, ak3, ak_cols, scale, tabs, *, seq_len, d):
    n1, nsp = tabs["n1"], tabs["nsp"]
    half = n1 // 2
    v3 = v.reshape(half, FFT_N2, v.shape[1])
    g3 = gate.reshape(half, FFT_N2, gate.shape[1])
    a = fft_stage1(tabs["f1"][:, :half], v3, d=d, tn=256, col_ofs=v_cols)
    b3 = fft_mid(a.reshape(nsp, 2 * FFT_N2, d), ak3, ak_cols, tabs["mf"], tabs["mi"], scale, dc=512)
    out = fft_stage1_inv(tabs["g1"], b3.reshape(2 * nsp, FFT_N2, d), v3, v_cols, g3, gate_cols, fbias, d=d, tn=256)
    return out.reshape(seq_len, d)


def _rope_tables(seq_len, scale):
    n_freq = MLA_ROPE // 4
    rows = seq_len // GRID_W
    row = jnp.repeat(jnp.arange(rows, dtype=F32), GRID_W)
    col = jnp.tile(jnp.arange(GRID_W, dtype=F32), rows)
    inv = ROPE_THETA ** (-jnp.arange(n_freq, dtype=F32) / n_freq)
    ang = jnp.stack([row[:, None] * inv, col[:, None] * inv], axis=1)
    cos = jnp.broadcast_to(jnp.cos(ang)[:, :, None, :], (seq_len, 2, 2, n_freq)).reshape(seq_len, MLA_ROPE)
    sin = jnp.broadcast_to(jnp.sin(ang)[:, :, None, :], (seq_len, 2, 2, n_freq)).reshape(seq_len, MLA_ROPE)
    one = jnp.ones((seq_len, MLA_NOPE), F32)
    zero = jnp.zeros((seq_len, MLA_NOPE), F32)
    z64 = jnp.zeros((seq_len, MLA_HEAD_PAD - MLA_NOPE - MLA_ROPE), F32)
    ta = jnp.concatenate([one, cos, z64], axis=1) * scale
    tb = jnp.concatenate([zero, sin, z64], axis=1) * scale
    tc = jnp.concatenate([cos, z64], axis=1)
    ts = jnp.concatenate([sin, z64], axis=1)
    return ta, tb, tc, ts


def _rope_swap_cols(w):
    n_freq = MLA_ROPE // 4
    w4 = w.reshape(w.shape[0], 2, 2, n_freq)
    return jnp.stack([-w4[:, :, 1], w4[:, :, 0]], axis=2).reshape(w.shape[0], MLA_ROPE)


def _even_layer(x, ctx, mods, norm_mix_w, w_in, conv_w, conv_b, dt_bias, a_log, d_skip, ssd_norm_w,
                q_norm_w, w_uq, kv_norm_w, w_ukv, w_o):
    seq_len, d = x.shape
    sh1, sc1, g1 = (mods[0:1, i * d:(i + 1) * d] for i in range(3))
    csh1, csc1 = mods[1:2, 0:d], mods[1:2, d:2 * d]

    o1 = Q_SIDE + SSD_XBC
    o2 = o1 + SSD_DT
    o3 = o2 + MLA_KV_RANK
    w_kr = w_in[:, o3:]
    w_ext = jnp.concatenate([w_in[:, :Q_SIDE], w_in[:, Q_SIDE:o1], w_in[:, o2:o3], w_kr, _rope_swap_cols(w_kr),
                             w_in[:, o1:o2], jnp.zeros((d, P_COLS - P_DT - SSD_DT), F32)], axis=1).astype(BF16)
    zb = jnp.zeros((P_COLS,), F32)
    p_lat = normmm(x, norm_mix_w, sh1, sc1, w_ext, zb, tm=_pick(seq_len, (1024, 512)), tn=768, out_dtype=F32)
    p_ctx = normmm(ctx, norm_mix_w, csh1, csc1, w_ext, zb, tm=CTX_LEN, tn=768, out_dtype=F32)

    scale = float(MLA_NOPE + MLA_ROPE) ** -0.5 * math.log2(math.e)
    ta, tb, tc, ts = _rope_tables(seq_len, scale)
    wq = w_uq.reshape(MLA_Q_RANK, MLA_HEADS, MLA_NOPE + MLA_ROPE)
    zpad = jnp.zeros((MLA_Q_RANK, MLA_HEADS, MLA_HEAD_PAD - MLA_NOPE - MLA_ROPE), F32)
    wa = jnp.concatenate([wq, zpad], axis=2).reshape(MLA_Q_RANK, -1).astype(BF16)
    wr = wq[:, :, MLA_NOPE:]
    n_freq = MLA_ROPE // 4
    wr4 = wr.reshape(MLA_Q_RANK, MLA_HEADS, 2, 2, n_freq)
    wsw = jnp.stack([-wr4[:, :, :, 1], wr4[:, :, :, 0]], axis=3).reshape(MLA_Q_RANK, MLA_HEADS, MLA_ROPE)
    wb = jnp.concatenate([jnp.zeros((MLA_Q_RANK, MLA_HEADS, MLA_NOPE), F32), wsw, zpad], axis=2)
    wb = wb.reshape(MLA_Q_RANK, -1).astype(BF16)
    q = qproj(p_lat, q_norm_w, wa, wb, ta, tb, tm=512)

    zk = jnp.zeros((MLA_KV_RANK,), F32)
    w_ukv_b = w_ukv.astype(BF16)
    zkb = jnp.zeros((w_ukv.shape[1],), F32)
    kv_lat = normmm(p_lat, kv_norm_w, zk, zk, w_ukv_b, zkb, tm=512, tn=w_ukv.shape[1], out_dtype=BF16,
                    x_col=P_CKV // MLA_KV_RANK)
    kv_ctx = normmm(p_ctx, kv_norm_w, zk, zk, w_ukv_b, zkb, tm=CTX_LEN, tn=w_ukv.shape[1], out_dtype=BF16,
                    x_col=P_CKV // MLA_KV_RANK)
    k_lat, v_lat = kv_assemble(kv_lat, p_lat, tc, ts, tm=512, rope=True)
    k_ctx, v_ctx = kv_assemble(kv_ctx, p_ctx, tc, ts, tm=CTX_LEN, rope=False)
    k_all = jnp.concatenate([k_lat, k_ctx], axis=0)
    v_all = jnp.concatenate([v_lat, v_ctx], axis=0)
    s_tot = seq_len + CTX_LEN
    o_att = attention(q, k_all, v_all, tq=_pick(seq_len, (2048, 1024, 512, 256)),
                      tk=_pick(s_tot, (1280, 1024, 768, 640, 512, 384, 256, 128)))

    xbc_lat = dwconv3(p_lat, conv_w, conv_b, tm=512, tc=SSD_XBC, col0=P_XBC, width=SSD_XBC, act=True)
    xbc_ctx = dwconv3(p_ctx, conv_w, conv_b, tm=CTX_LEN, tc=SSD_XBC, col0=P_XBC, width=SSD_XBC, act=True)
    dt_all = jnp.concatenate([p_lat[:, P_DT:P_DT + SSD_DT], p_ctx[:, P_DT:P_DT + SSD_DT]], axis=0)
    nch = s_tot // SSD_CHUNK
    dt2 = dt_all.reshape(s_tot, 2, SSD_HEADS).transpose(1, 0, 2)
    dt2t = dt2.reshape(2, nch, SSD_CHUNK, SSD_HEADS).transpose(0, 1, 3, 2)
    bias2 = dt_bias.reshape(2, 1, SSD_HEADS)
    bias2t = dt_bias.reshape(2, SSD_HEADS, 1)
    a_neg = -jnp.exp(a_log.astype(F32))
    a2 = a_neg.reshape(2, 1, SSD_HEADS)
    a2t = a_neg.reshape(2, SSD_HEADS, 1)
    lower = jnp.tril(jnp.ones((SSD_CHUNK, SSD_CHUNK), F32))
    tri2 = jnp.stack([lower, lower.T]).astype(BF16)
    yf, yb = ssd_scan(xbc_lat, xbc_ctx, dt2, dt2t, bias2, bias2t, a2, a2t, tri2, n_lat_chunks=seq_len // SSD_CHUNK)

    dsk = jnp.repeat(d_skip[0] + d_skip[1], SSD_HEAD_DIM).reshape(1, SSD_INNER)
    return merge_out(o_att, yf, yb, xbc_lat, p_lat, dsk, ssd_norm_w.reshape(1, SSD_INNER), w_o.astype(BF16), x, g1,
                     tm=256, tn=d)


def _odd_layer(x, mods, norm_mix_w, w_in, b_in, short_w, short_b, fw1, fb1, fw_mid, fb_mid, freq, fw_out,
               fbias, w_out, b_out):
    seq_len, d = x.shape
    sh1, sc1, g1 = (mods[0:1, i * d:(i + 1) * d] for i in range(3))
    pc = normmm_conv(x, norm_mix_w, sh1, sc1, w_in.astype(BF16), b_in, short_w, short_b,
                     tm=_pick(seq_len, (1024, 512)), tn=768)

    tabs = _fft_tables(seq_len)
    bands = (HY_EMB - 1) // 2
    fband = jnp.linspace(1e-4, bands - 1, bands, dtype=F32)
    fvec = jnp.concatenate([jnp.zeros((1,), F32), fband, fband, jnp.zeros((128 - HY_EMB,), F32)]).reshape(1, 128)
    w1p = jnp.concatenate([fw1.astype(F32), jnp.zeros((128 - HY_EMB, HY_HID), F32)], axis=0).astype(BF16)
    lo = math.log(HY_SLOW_DECAY) / HY_TARGET
    hi = math.log(HY_FAST_DECAY) / HY_TARGET
    delta = jnp.abs(jnp.linspace(lo, hi, d, dtype=F32)).reshape(1, d)
    n_ord = fw_out.shape[1]
    wo = jnp.transpose(fw_out, (2, 0, 1, 3)).reshape(2, HY_HID, n_ord * d).astype(BF16)
    hid = filter_hidden(fvec, w1p, fb1.reshape(1, HY_HID), fw_mid.astype(BF16), fb_mid.reshape(-1, 1, HY_HID),
                        freq.reshape(1, HY_HID), seq_len=seq_len, tr=1024)
    ak, kabs = filter_stage1(tabs["f1"], hid.reshape(tabs["n1"], FFT_N2, HY_HID), wo, jnp.tile(delta, (1, n_ord)),
                             seq_len=seq_len, tn=256)
    ak3 = ak.reshape(tabs["nsp"], 2 * FFT_N2, n_ord * d)
    y_cols = 2
    y = None
    for i in range(n_ord):
        scale = 1.0 / kabs[:, i * d:(i + 1) * d]
        fb = fbias[i].reshape(1, d)
        if y is None:
            y = long_conv_gate(pc, y_cols, pc, i, fb, ak3, i, scale, tabs, seq_len=seq_len, d=d)
        else:
            y = long_conv_gate(y, 0, pc, i, fb, ak3, i, scale, tabs, seq_len=seq_len, d=d)
    y = y.reshape(seq_len, d)
    return mm_res(y, w_out.astype(BF16), b_out, x, g1, tm=512, tn=d)


def kernel(x, c, ctx, c_ctx, mod_w, mod_b, norm_mix_w, norm_ffn_w, ffn_w1, ffn_w3, ffn_w2, ev_w_in, ev_conv_w, ev_conv_b, ev_dt_bias, ev_a_log, ev_d_skip, ev_ssd_norm_w, ev_q_norm_w, ev_w_uq, ev_kv_norm_w, ev_w_ukv, ev_w_o, hy_w_in, hy_b_in, hy_short_w, hy_short_b, hy_fw1, hy_fb1, hy_fw_mid, hy_fb_mid, hy_freq, hy_fw_out, hy_fbias, hy_w_out, hy_b_out, final_norm_w):
    assert x.shape[0] == 1 and mod_w.shape[0] == 2
    xs = x[0]
    xc = ctx[0]
    d = xs.shape[1]
    vecs = jnp.concatenate([c.reshape(1, d), c_ctx.reshape(1, d), jnp.zeros((6, d), F32)], axis=0)
    depth = mod_w.shape[0]
    w1_b, w3_b, w2_b = ffn_w1.astype(BF16), ffn_w3.astype(BF16), ffn_w2.astype(BF16)
    for i in range(depth):
        mods = adaln_vectors(vecs, mod_w, mod_b, i)
        sh2, sc2, g2 = (mods[0:1, j * d:(j + 1) * d] for j in range(3, 6))
        if i % 2 == 0:
            e = i // 2
            xs = _even_layer(xs, xc, mods, norm_mix_w[i], ev_w_in[e], ev_conv_w[e], ev_conv_b[e], ev_dt_bias[e],
                             ev_a_log[e], ev_d_skip[e], ev_ssd_norm_w[e], ev_q_norm_w[e], ev_w_uq[e],
                             ev_kv_norm_w[e], ev_w_ukv[e], ev_w_o[e])
        else:
            o = i // 2
            xs = _odd_layer(xs, mods, norm_mix_w[i], hy_w_in[o], hy_b_in[o], hy_short_w[o], hy_short_b[o],
                            hy_fw1[o], hy_fb1[o], hy_fw_mid[o], hy_fb_mid[o], hy_freq[o], hy_fw_out[o],
                            hy_fbias[o], hy_w_out[o], hy_b_out[o])
        xs = ffn(xs, norm_ffn_w[i], sh2, sc2, g2, w1_b, w3_b, w2_b, final_norm_w, layer=i, tm=512, tf=512,
                 final_norm=(i == depth - 1))
    return xs[None]
```

```python
import functools
import math

import jax
import jax.numpy as jnp
from jax import lax
from jax.experimental import pallas as pl
from jax.experimental.pallas import tpu as pltpu

F32 = jnp.float32
BF16 = jnp.bfloat16

EPS = 1e-6
GRID_W = 64
CTX_LEN = 256
SSD_HEADS = 16
SSD_HEAD_DIM = 64
SSD_INNER = 1024
SSD_GROUPS = 2
SSD_HPG = 8
SSD_STATE = 128
SSD_CHUNK = 256
SSD_XBC = 1536
SSD_DT = 32
MLA_HEADS = 8
MLA_NOPE = 128
MLA_ROPE = 64
MLA_V = 128
MLA_Q_RANK = 512
MLA_KV_RANK = 512
ROPE_THETA = 10000.0
MLA_HEAD_PAD = 256
Q_SIDE = SSD_INNER + MLA_Q_RANK
HY_EMB = 33
HY_HID = 64
HY_FAST_DECAY = 0.3
HY_SLOW_DECAY = 1.5
HY_TARGET = 1e-2
FFT_N2 = 128
SLAB_TILE = 8

P_Z, P_CQ, P_XBC, P_CKV, P_KR, P_DT, P_COLS = 0, 1024, 1536, 3072, 3584, 3712, 3840

VMEM_LIMIT = 56 * 1024 * 1024


def _cparams(*sem):
    return pltpu.CompilerParams(dimension_semantics=sem, vmem_limit_bytes=VMEM_LIMIT)


def _pick(n, cands):
    for c in cands:
        if n % c == 0:
            return c
    raise ValueError(f"no tile for {n}")


def _split(a):
    hi = a.astype(BF16)
    lo = (a - hi.astype(F32)).astype(BF16)
    return hi, lo


def _dot(a, b):
    return jnp.dot(a, b, preferred_element_type=F32)


def _silu(x):
    return x * (1.0 / (1.0 + jnp.exp(-x)))


def _modnorm(x, nw, sh, sc):
    ms = jnp.mean(x * x, axis=-1, keepdims=True)
    return (x * lax.rsqrt(ms + EPS) * nw) * (1.0 + sc) + sh


def _matvec_kernel(x_ref, w_ref, b_ref, o_ref):
    x = x_ref[...]
    o_ref[...] = _dot(_silu(x).astype(BF16), w_ref[...].astype(BF16)) + b_ref[...]


def adaln_vectors(vecs, w_all, b_all, layer):
    nl, k, n = w_all.shape
    tn = 1024
    return pl.pallas_call(
        _matvec_kernel,
        grid=(n // tn,),
        in_specs=[pl.BlockSpec((8, k), lambda j: (0, 0)),
                  pl.BlockSpec((None, k, tn), lambda j: (layer, 0, j)),
                  pl.BlockSpec((None, 1, tn), lambda j: (layer, 0, j))],
        out_specs=pl.BlockSpec((8, tn), lambda j: (0, j)),
        out_shape=jax.ShapeDtypeStruct((8, n), F32),
        compiler_params=_cparams("arbitrary"),
        name="adaln_vectors",
    )(vecs, w_all, b_all.reshape(nl, 1, n))


def _normmm_kernel(x_ref, nw_ref, sh_ref, sc_ref, w_ref, b_ref, o_ref, xn_ref):
    @pl.when(pl.program_id(1) == 0)
    def _():
        xn_ref[...] = _modnorm(x_ref[...].astype(F32), nw_ref[...], sh_ref[...], sc_ref[...]).astype(BF16)

    o_ref[...] = (_dot(xn_ref[...], w_ref[...]) + b_ref[...]).astype(o_ref.dtype)


def normmm(x, nw, sh, sc, w, b, *, tm, tn, out_dtype, x_col=0):
    m = x.shape[0]
    k, n = w.shape
    return pl.pallas_call(
        _normmm_kernel,
        grid=(m // tm, n // tn),
        in_specs=[pl.BlockSpec((tm, k), lambda i, j: (i, x_col)),
                  pl.BlockSpec((1, k), lambda i, j: (0, 0)),
                  pl.BlockSpec((1, k), lambda i, j: (0, 0)),
                  pl.BlockSpec((1, k), lambda i, j: (0, 0)),
                  pl.BlockSpec((k, tn), lambda i, j: (0, j)),
                  pl.BlockSpec((1, tn), lambda i, j: (0, j))],
        out_specs=pl.BlockSpec((tm, tn), lambda i, j: (i, j)),
        out_shape=jax.ShapeDtypeStruct((m, n), out_dtype),
        scratch_shapes=[pltpu.VMEM((tm, k), BF16)],
        compiler_params=_cparams("parallel", "arbitrary"),
        name="normmm",
    )(x, nw.reshape(1, k), sh.reshape(1, k), sc.reshape(1, k), w, b.reshape(1, n))


def _normmm_conv_kernel(x_ref, xprev_ref, xnext_ref, nw_ref, sh_ref, sc_ref, w_ref, b_ref, cw_ref, cb_ref,
                        o_ref, xn_ref, xh_ref):
    i = pl.program_id(0)

    @pl.when(pl.program_id(1) == 0)
    def _():
        nw, sh, sc = nw_ref[...], sh_ref[...], sc_ref[...]
        xn_ref[...] = _modnorm(x_ref[...], nw, sh, sc).astype(BF16)
        xh_ref[0:8, :] = _modnorm(xprev_ref[...], nw, sh, sc).astype(BF16)
        xh_ref[8:16, :] = _modnorm(xnext_ref[...], nw, sh, sc).astype(BF16)

    w = w_ref[...]
    b = b_ref[...]
    y = _dot(xn_ref[...], w) + b
    yh = _dot(xh_ref[...], w) + b
    prev_row = jnp.where(i == 0, 0.0, yh[7:8, :])
    next_row = jnp.where(i == pl.num_programs(0) - 1, 0.0, yh[8:9, :])
    yp, yn = _shift_rows(y, prev_row, next_row)
    o_ref[...] = yp * cw_ref[0:1, :] + y * cw_ref[1:2, :] + yn * cw_ref[2:3, :] + cb_ref[...]


def normmm_conv(x, nw, sh, sc, w, b, cw, cb, *, tm, tn):
    m, k = x.shape
    n = w.shape[1]
    nrb = m // 8
    tb = tm // 8
    vec = pl.BlockSpec((1, k), lambda i, j: (0, 0))
    return pl.pallas_call(
        _normmm_conv_kernel,
        grid=(m // tm, n // tn),
        in_specs=[pl.BlockSpec((tm, k), lambda i, j: (i, 0)),
                  pl.BlockSpec((8, k), lambda i, j: (jnp.maximum(i * tb - 1, 0), 0)),
                  pl.BlockSpec((8, k), lambda i, j: (jnp.minimum((i + 1) * tb, nrb - 1), 0)),
                  vec, vec, vec,
                  pl.BlockSpec((k, tn), lambda i, j: (0, j)),
                  pl.BlockSpec((1, tn), lambda i, j: (0, j)),
                  pl.BlockSpec((3, tn), lambda i, j: (0, j)),
                  pl.BlockSpec((1, tn), lambda i, j: (0, j))],
        out_specs=pl.BlockSpec((tm, tn), lambda i, j: (i, j)),
        out_shape=jax.ShapeDtypeStruct((m, n), F32),
        scratch_shapes=[pltpu.VMEM((tm, k), BF16), pltpu.VMEM((16, k), BF16)],
        compiler_params=_cparams("parallel", "arbitrary"),
        name="normmm_conv",
    )(x, x, x, nw.reshape(1, k), sh.reshape(1, k), sc.reshape(1, k), w, b.reshape(1, n), cw, cb.reshape(1, n))


def _qproj_kernel(x_ref, nw_ref, wa_ref, wb_ref, ta_ref, tb_ref, o_ref):
    x = x_ref[...]
    ms = jnp.mean(x * x, axis=-1, keepdims=True)
    xn = (x * lax.rsqrt(ms + EPS) * nw_ref[...]).astype(BF16)
    ta = ta_ref[...]
    tb = tb_ref[...]
    hp = MLA_HEAD_PAD
    for h in range(MLA_HEADS):
        cols = slice(h * hp, (h + 1) * hp)
        o_ref[:, cols] = (_dot(xn, wa_ref[:, cols]) * ta + _dot(xn, wb_ref[:, cols]) * tb).astype(o_ref.dtype)


def qproj(p, nw, wa, wb, ta, tb, *, tm):
    m = ta.shape[0]
    k = MLA_Q_RANK
    hp = MLA_HEAD_PAD
    n = MLA_HEADS * hp
    return pl.pallas_call(
        _qproj_kernel,
        grid=(m // tm,),
        in_specs=[pl.BlockSpec((tm, k), lambda i: (i, P_CQ // MLA_Q_RANK)),
                  pl.BlockSpec((1, k), lambda i: (0, 0)),
                  pl.BlockSpec((k, n), lambda i: (0, 0)),
                  pl.BlockSpec((k, n), lambda i: (0, 0)),
                  pl.BlockSpec((tm, hp), lambda i: (i, 0)),
                  pl.BlockSpec((tm, hp), lambda i: (i, 0))],
        out_specs=pl.BlockSpec((tm, n), lambda i: (i, 0)),
        out_shape=jax.ShapeDtypeStruct((m, n), BF16),
        compiler_params=_cparams("parallel"),
        name="mla_qproj",
    )(p, nw.reshape(1, k), wa, wb, ta, tb)


def _kv_assemble_kernel(kv_ref, kr_ref, tc_ref, ts_ref, k_ref, v_ref, *, rope):
    krr = kr_ref[...]
    kr = krr[:, :MLA_ROPE]
    if rope:
        kr = kr * tc_ref[...][:, :MLA_ROPE] + krr[:, MLA_ROPE:] * ts_ref[...][:, :MLA_ROPE]
    tail = jnp.concatenate([kr, jnp.zeros_like(kr)], axis=-1).astype(BF16)
    for h in range(MLA_HEADS):
        base = h * (MLA_NOPE + MLA_V)
        k_ref[:, h * MLA_HEAD_PAD:h * MLA_HEAD_PAD + MLA_NOPE] = kv_ref[:, base:base + MLA_NOPE]
        k_ref[:, h * MLA_HEAD_PAD + MLA_NOPE:(h + 1) * MLA_HEAD_PAD] = tail
        v_ref[:, h * MLA_V:(h + 1) * MLA_V] = kv_ref[:, base + MLA_NOPE:base + MLA_NOPE + MLA_V]


def kv_assemble(kvp, p, tc, ts, *, tm, rope):
    m = kvp.shape[0]
    return pl.pallas_call(
        functools.partial(_kv_assemble_kernel, rope=rope),
        grid=(m // tm,),
        in_specs=[pl.BlockSpec((tm, MLA_HEADS * (MLA_NOPE + MLA_V)), lambda i: (i, 0)),
                  pl.BlockSpec((tm, 128), lambda i: (i, P_KR // 128)),
                  pl.BlockSpec((tm, 128), lambda i: (i, 0)),
                  pl.BlockSpec((tm, 128), lambda i: (i, 0))],
        out_specs=[pl.BlockSpec((tm, MLA_HEADS * MLA_HEAD_PAD), lambda i: (i, 0)),
                   pl.BlockSpec((tm, MLA_HEADS * MLA_V), lambda i: (i, 0))],
        out_shape=[jax.ShapeDtypeStruct((m, MLA_HEADS * MLA_HEAD_PAD), BF16),
                   jax.ShapeDtypeStruct((m, MLA_HEADS * MLA_V), BF16)],
        compiler_params=_cparams("parallel"),
        name="mla_kv_assemble",
    )(kvp, p, tc, ts)


def _attn_kernel(q_ref, k_ref, v_ref, o_ref, s0_ref, s1_ref, m_ref, acc_ref, *, nk):
    j = pl.program_id(2)

    def qk(s_ref):
        s_ref[...] = lax.dot_general(q_ref[...], k_ref[...], (((1,), (1,)), ((), ())), preferred_element_type=F32)

    def softmax_pv(s_ref):
        s = s_ref[...]
        m_prev = m_ref[...]
        m_new = jnp.maximum(m_prev, jnp.max(s, axis=-1, keepdims=True))
        alpha = jnp.exp2(m_prev - m_new)
        p = jnp.exp2(s - m_new).astype(BF16)
        v = v_ref[...]
        lane = lax.broadcasted_iota(jnp.int32, v.shape, 1)
        ones_col = jnp.where(lane == 0, 1.0, 0.0).astype(BF16)
        v_ext = jnp.concatenate([v, ones_col], axis=1)
        acc_ref[...] = alpha * acc_ref[...] + _dot(p, v_ext)
        m_ref[...] = m_new

    @pl.when(j == 0)
    def _():
        m_ref[...] = jnp.full(m_ref.shape, -jnp.inf, F32)
        acc_ref[...] = jnp.zeros(acc_ref.shape, F32)
        qk(s0_ref)

    mid = jnp.logical_and(j > 0, j < nk)

    @pl.when(jnp.logical_and(mid, j % 2 == 1))
    def _():
        softmax_pv(s0_ref)
        qk(s1_ref)

    @pl.when(jnp.logical_and(mid, j % 2 == 0))
    def _():
        softmax_pv(s1_ref)
        qk(s0_ref)

    @pl.when(j == nk)
    def _():
        softmax_pv(s1_ref if (nk - 1) % 2 else s0_ref)
        acc = acc_ref[...]
        o_ref[...] = (acc[:, :MLA_V] / acc[:, MLA_V:MLA_V + 1]).astype(o_ref.dtype)


def attention(q, k, v, *, tq, tk):
    lq = q.shape[0]
    s = k.shape[0]
    nk = s // tk
    return pl.pallas_call(
        functools.partial(_attn_kernel, nk=nk),
        grid=(MLA_HEADS, lq // tq, nk + 1),
        in_specs=[pl.BlockSpec((tq, MLA_HEAD_PAD), lambda h, i, j: (i, h)),
                  pl.BlockSpec((tk, MLA_HEAD_PAD), lambda h, i, j: (jnp.minimum(j, nk - 1), h)),
                  pl.BlockSpec((tk, MLA_V), lambda h, i, j: (jnp.maximum(j - 1, 0), h))],
        out_specs=pl.BlockSpec((tq, MLA_V), lambda h, i, j: (i, h)),
        out_shape=jax.ShapeDtypeStruct((lq, MLA_HEADS * MLA_V), BF16),
        scratch_shapes=[pltpu.VMEM((tq, tk), F32), pltpu.VMEM((tq, tk), F32), pltpu.VMEM((tq, 1), F32),
                        pltpu.VMEM((tq, 2 * MLA_V), F32)],
        compiler_params=_cparams("parallel", "parallel", "arbitrary"),
        name="mla_attention",
    )(q, k, v)


def _shift_rows(x, prev_row, next_row):
    tm = x.shape[0]
    rows = lax.broadcasted_iota(jnp.int32, x.shape, 0)
    xp = jnp.where(rows == 0, prev_row, pltpu.roll(x, 1, 0))
    xn = jnp.where(rows == tm - 1, next_row, pltpu.roll(x, tm - 1, 0))
    return xp, xn


def _dwconv_kernel(x_ref, xprev_ref, xnext_ref, w_ref, b_ref, o_ref, *, act):
    i = pl.program_id(0)
    x = x_ref[...]
    prev_row = jnp.where(i == 0, 0.0, xprev_ref[7:8, :])
    next_row = jnp.where(i == pl.num_programs(0) - 1, 0.0, xnext_ref[0:1, :])
    xp, xn = _shift_rows(x, prev_row, next_row)
    y = xp * w_ref[0:1, :] + x * w_ref[1:2, :] + xn * w_ref[2:3, :] + b_ref[...]
    if act:
        y = _silu(y)
    o_ref[...] = y


def dwconv3(x, w, b, *, tm, tc, col0, width, act):
    m = x.shape[0]
    cb = col0 // tc
    nrb = m // 8
    tb = tm // 8
    return pl.pallas_call(
        functools.partial(_dwconv_kernel, act=act),
        grid=(m // tm, width // tc),
        in_specs=[pl.BlockSpec((tm, tc), lambda i, j: (i, cb + j)),
                  pl.BlockSpec((8, tc), lambda i, j: (jnp.maximum(i * tb - 1, 0), cb + j)),
                  pl.BlockSpec((8, tc), lambda i, j: (jnp.minimum((i + 1) * tb, nrb - 1), cb + j)),
                  pl.BlockSpec((3, tc), lambda i, j: (0, j)),
                  pl.BlockSpec((1, tc), lambda i, j: (0, j))],
        out_specs=pl.BlockSpec((tm, tc), lambda i, j: (i, j)),
        out_shape=jax.ShapeDtypeStruct((m, width), F32),
        compiler_params=_cparams("arbitrary", "arbitrary"),
        name="dwconv3",
    )(x, x, x, w, b.reshape(1, width))


def _softplus(x):
    return jnp.maximum(x, 0.0) + jnp.log(1.0 + jnp.exp(-jnp.abs(x)))


def _ssd_prepare(d, xbc, dt_ref, dtt_ref, bias_ref, biast_ref, a_ref, at_ref, tri_ref):
    q = SSD_CHUNK
    dt = _softplus(dt_ref[0] + bias_ref[d])
    dtt = _softplus(dtt_ref[0, 0] + biast_ref[d])
    dta_hi, dta_lo = _split(dt * a_ref[d])
    dtat_hi, dtat_lo = _split(dtt * at_ref[d])
    tri = tri_ref[d]
    trit = tri_ref[1 - d]
    acum = _dot(tri, dta_hi) + _dot(tri, dta_lo)
    acumt = _dot(dtat_hi, trit) + _dot(dtat_lo, trit)
    total = acum[q - 1:q, :] if d == 0 else acum[0:1, :]
    return dict(xbc=xbc, dt=dt, acum=acum, acumt=acumt, to_end=jnp.exp(total - acum),
                from_start=jnp.exp(acum), chunk_decay=jnp.exp(total), mask=tri > 0.5)


def _per_head_lanes(a, g):
    return jnp.concatenate([jnp.broadcast_to(a[:, h:h + 1], (a.shape[0], SSD_HEAD_DIM))
                            for h in range(g * SSD_HPG, (g + 1) * SSD_HPG)], axis=1)


def _ssd_group(d, p, g, h_ref):
    xbc = p["xbc"]
    gw = SSD_HPG * SSD_HEAD_DIM
    bm = xbc[:, SSD_INNER + g * SSD_STATE:SSD_INNER + (g + 1) * SSD_STATE]
    cm = xbc[:, SSD_INNER + (SSD_GROUPS + g) * SSD_STATE:SSD_INNER + (SSD_GROUPS + g + 1) * SSD_STATE]
    bm_b = bm.astype(BF16)
    cm_b = cm.astype(BF16)
    cb = lax.dot_general(cm_b, bm_b, (((1,), (1,)), ((), ())), preferred_element_type=F32)
    xdt = xbc[:, g * gw:(g + 1) * gw] * _per_head_lanes(p["dt"], g)
    h_prev = h_ref[d * SSD_GROUPS + g]
    y_off = _dot(cm_b, h_prev.astype(BF16)) * _per_head_lanes(p["from_start"], g)
    states = _dot(bm.T.astype(BF16), (xdt * _per_head_lanes(p["to_end"], g)).astype(BF16))
    h_ref[d * SSD_GROUPS + g] = h_prev * _per_head_lanes(p["chunk_decay"], g) + states
    return dict(cb=cb, xdt_b=xdt.astype(BF16), y_off=y_off)


def _ssd_head(p, grp, g, r, o_ref):
    h = g * SSD_HPG + r
    seg = p["acum"][:, h:h + 1] - p["acumt"][h:h + 1, :]
    decay = jnp.exp(jnp.where(p["mask"], seg, -jnp.inf))
    lanes = slice(r * SSD_HEAD_DIM, (r + 1) * SSD_HEAD_DIM)
    y_diag = _dot((grp["cb"] * decay).astype(BF16), grp["xdt_b"][:, lanes])
    o_ref[:, h * SSD_HEAD_DIM:(h + 1) * SSD_HEAD_DIM] = y_diag + grp["y_off"][:, lanes]


def _ssd_kernel(xf_ref, xfc_ref, xb_ref, xbc_ref, dtf_ref, dtb_ref, dttf_ref, dttb_ref, bias_ref, biast_ref, a_ref, at_ref, tri_ref,
                of_ref, ob_ref, h_ref):
    @pl.when(pl.program_id(0) == 0)
    def _():
        h_ref[...] = jnp.zeros(h_ref.shape, F32)

    s = pl.program_id(0)
    ncx = CTX_LEN // SSD_CHUNK
    x_fwd = jnp.where(s < ncx, xfc_ref[...], xf_ref[...])
    x_bwd = jnp.where(s < ncx, xbc_ref[...], xb_ref[...])
    prm = (_ssd_prepare(0, x_fwd, dtf_ref, dttf_ref, bias_ref, biast_ref, a_ref, at_ref, tri_ref),
           _ssd_prepare(1, x_bwd, dtb_ref, dttb_ref, bias_ref, biast_ref, a_ref, at_ref, tri_ref))
    outs = (of_ref, ob_ref)
    for g in range(SSD_GROUPS):
        grp = (_ssd_group(0, prm[0], g, h_ref), _ssd_group(1, prm[1], g, h_ref))
        for r in range(SSD_HPG):
            for d in range(2):
                _ssd_head(prm[d], grp[d], g, r, outs[d])


def ssd_scan(xbc_lat, xbc_ctx, dt2, dt2t, bias2, bias2t, a2, a2t, tri2, *, n_lat_chunks):
    nc = n_lat_chunks
    ncx = CTX_LEN // SSD_CHUNK
    tot = nc + ncx
    q = SSD_CHUNK

    def cf(s):
        return lax.rem(s + nc, tot)

    def cbk(s):
        return tot - 1 - s

    def full(shape):
        return pl.BlockSpec(shape, lambda s: (0,) * len(shape))

    return pl.pallas_call(
        _ssd_kernel,
        grid=(tot,),
        in_specs=[pl.BlockSpec((q, SSD_XBC), lambda s: (jnp.minimum(cf(s), nc - 1), 0)),
                  pl.BlockSpec((q, SSD_XBC), lambda s: (jnp.maximum(cf(s) - nc, 0), 0)),
                  pl.BlockSpec((q, SSD_XBC), lambda s: (jnp.minimum(cbk(s), nc - 1), 0)),
                  pl.BlockSpec((q, SSD_XBC), lambda s: (jnp.maximum(cbk(s) - nc, 0), 0)),
                  pl.BlockSpec((1, q, SSD_HEADS), lambda s: (0, cf(s), 0)),
                  pl.BlockSpec((1, q, SSD_HEADS), lambda s: (1, cbk(s), 0)),
                  pl.BlockSpec((1, 1, SSD_HEADS, q), lambda s: (0, cf(s), 0, 0)),
                  pl.BlockSpec((1, 1, SSD_HEADS, q), lambda s: (1, cbk(s), 0, 0)),
                  full((2, 1, SSD_HEADS)), full((2, SSD_HEADS, 1)), full((2, 1, SSD_HEADS)),
                  full((2, SSD_HEADS, 1)), full((2, q, q))],
        out_specs=[pl.BlockSpec((q, SSD_INNER), lambda s: (jnp.where(cf(s) >= nc, 0, cf(s)), 0)),
                   pl.BlockSpec((q, SSD_INNER), lambda s: (jnp.where(cbk(s) >= nc, nc - 1, cbk(s)), 0))],
        out_shape=[jax.ShapeDtypeStruct((nc * q, SSD_INNER), F32), jax.ShapeDtypeStruct((nc * q, SSD_INNER), F32)],
        scratch_shapes=[pltpu.VMEM((2 * SSD_GROUPS, SSD_STATE, SSD_HPG * SSD_HEAD_DIM), F32)],
        compiler_params=_cparams("arbitrary"),
        name="ssd_scan",
    )(xbc_lat, xbc_ctx, xbc_lat, xbc_ctx, dt2, dt2, dt2t, dt2t, bias2, bias2t, a2, a2t, tri2)


def _merge_kernel(o_ref, yf_ref, yb_ref, xs_ref, z_ref, dsk_ref, nw_ref, w_ref, x_ref, g_ref, out_ref, a_ref):
    @pl.when(pl.program_id(1) == 0)
    def _():
        a_ref[:, :MLA_HEADS * MLA_V] = o_ref[...]
        y = yf_ref[...] + yb_ref[...] + dsk_ref[...] * xs_ref[...]
        gy = y * _silu(z_ref[...])
        gw = SSD_INNER // SSD_GROUPS
        for g in range(SSD_GROUPS):
            part = gy[:, g * gw:(g + 1) * gw]
            ms = jnp.mean(part * part, axis=-1, keepdims=True)
            a_ref[:, MLA_HEADS * MLA_V + g * gw:MLA_HEADS * MLA_V + (g + 1) * gw] = (
                part * lax.rsqrt(ms + EPS) * nw_ref[:, g * gw:(g + 1) * gw]).astype(BF16)

    out_ref[...] = x_ref[...] + g_ref[...] * _dot(a_ref[...], w_ref[...])


def merge_out(o_att, yf, yb, xbc, p, dsk, nw, w_o, x, gate, *, tm, tn):
    m, n = x.shape
    kw = w_o.shape[0]
    return pl.pallas_call(
        _merge_kernel,
        grid=(m // tm, n // tn),
        in_specs=[pl.BlockSpec((tm, MLA_HEADS * MLA_V), lambda i, j: (i, 0)),
                  pl.BlockSpec((tm, SSD_INNER), lambda i, j: (i, 0)),
                  pl.BlockSpec((tm, SSD_INNER), lambda i, j: (i, 0)),
                  pl.BlockSpec((tm, SSD_INNER), lambda i, j: (i, 0)),
                  pl.BlockSpec((tm, SSD_INNER), lambda i, j: (i, 0)),
                  pl.BlockSpec((1, SSD_INNER), lambda i, j: (0, 0)),
                  pl.BlockSpec((1, SSD_INNER), lambda i, j: (0, 0)),
                  pl.BlockSpec((kw, tn), lambda i, j: (0, j)),
                  pl.BlockSpec((tm, tn), lambda i, j: (i, j)),
                  pl.BlockSpec((1, tn), lambda i, j: (0, j))],
        out_specs=pl.BlockSpec((tm, tn), lambda i, j: (i, j)),
        out_shape=jax.ShapeDtypeStruct((m, n), F32),
        scratch_shapes=[pltpu.VMEM((tm, kw), BF16)],
        compiler_params=_cparams("parallel", "arbitrary"),
        name="merge_out",
    )(o_att, yf, yb, xbc, p, dsk, nw, w_o, x, gate)


def _mmres_kernel(a_ref, w_ref, b_ref, x_ref, g_ref, o_ref):
    o_ref[...] = x_ref[...] + g_ref[...] * (_dot(a_ref[...].astype(BF16), w_ref[...]) + b_ref[...])


def mm_res(a, w, b, x, gate, *, tm, tn):
    m, k = a.shape
    n = w.shape[1]
    return pl.pallas_call(
        _mmres_kernel,
        grid=(m // tm, n // tn),
        in_specs=[pl.BlockSpec((tm, k), lambda i, j: (i, 0)),
                  pl.BlockSpec((k, tn), lambda i, j: (0, j)),
                  pl.BlockSpec((1, tn), lambda i, j: (0, j)),
                  pl.BlockSpec((tm, tn), lambda i, j: (i, j)),
                  pl.BlockSpec((1, tn), lambda i, j: (0, j))],
        out_specs=pl.BlockSpec((tm, tn), lambda i, j: (i, j)),
        out_shape=jax.ShapeDtypeStruct((m, n), F32),
        compiler_params=_cparams("parallel", "arbitrary"),
        name="mm_res",
    )(a, w, b.reshape(1, n), x, gate)


def _ffn_kernel(x_ref, nw_ref, sh_ref, sc_ref, g_ref, w1_ref, w3_ref, w2_ref, fw_ref, o_ref, xn_ref, acc_ref,
                *, final_norm):
    f = pl.program_id(1)

    @pl.when(f == 0)
    def _():
        xn_ref[...] = _modnorm(x_ref[...], nw_ref[...], sh_ref[...], sc_ref[...]).astype(BF16)
        acc_ref[...] = jnp.zeros(acc_ref.shape, F32)

    xn = xn_ref[...]
    a = _dot(xn, w1_ref[...])
    b = _dot(xn, w3_ref[...])
    acc_ref[...] += _dot((_silu(a) * b).astype(BF16), w2_ref[...])

    @pl.when(f == pl.num_programs(1) - 1)
    def _():
        y = x_ref[...] + g_ref[...] * acc_ref[...]
        if final_norm:
            ms = jnp.mean(y * y, axis=-1, keepdims=True)
            y = y * lax.rsqrt(ms + EPS) * fw_ref[...]
        o_ref[...] = y


def ffn(x, nw, sh, sc, gate, w1, w3, w2, fw, *, layer, tm, tf, final_norm):
    m, dm = x.shape
    dff = w1.shape[2]
    vec = pl.BlockSpec((1, dm), lambda i, f: (0, 0))
    return pl.pallas_call(
        functools.partial(_ffn_kernel, final_norm=final_norm),
        grid=(m // tm, dff // tf),
        in_specs=[pl.BlockSpec((tm, dm), lambda i, f: (i, 0)), vec, vec, vec, vec,
                  pl.BlockSpec((None, dm, tf), lambda i, f: (layer, 0, f)),
                  pl.BlockSpec((None, dm, tf), lambda i, f: (layer, 0, f)),
                  pl.BlockSpec((None, tf, dm), lambda i, f: (layer, f, 0)),
                  vec],
        out_specs=pl.BlockSpec((tm, dm), lambda i, f: (i, 0)),
        out_shape=jax.ShapeDtypeStruct((m, dm), F32),
        scratch_shapes=[pltpu.VMEM((tm, dm), BF16), pltpu.VMEM((tm, dm), F32)],
        compiler_params=_cparams("parallel", "arbitrary"),
        name="ffn",
    )(x, nw.reshape(1, dm), sh, sc, gate, w1, w3, w2, fw.reshape(1, dm))


def _filter_hidden_kernel(fvec_ref, w1_ref, b1_ref, wm_ref, bm_ref, freq_ref, h_ref, *, seq_len, tr):
    i = pl.program_id(0)
    n1_total = 2 * seq_len // FFT_N2
    shift = n1_total.bit_length() - 1
    assert 1 << shift == n1_total
    p = i * tr + lax.broadcasted_iota(jnp.int32, (tr, 1), 0)
    n = ((p & (n1_total - 1)) << (FFT_N2.bit_length() - 1)) + (p >> shift)
    lag = jnp.where(n < seq_len, n, 2 * seq_len - n).astype(F32)
    t = lag / float(seq_len - 1)
    ang = lag * (2.0 * math.pi / seq_len)
    lane = lax.broadcasted_iota(jnp.int32, (tr, 128), 1)
    feat = jnp.sin(ang * fvec_ref[0:1, :] + fvec_ref[1:2, :])
    emb = jnp.where(lane == 0, t, jnp.where(lane < HY_EMB, feat, 0.0))
    fr = freq_ref[...]
    h2 = tr // 2
    e = emb.astype(BF16)
    z = jnp.concatenate([_dot(e[:h2], w1_ref[...]), _dot(e[h2:], w1_ref[...])], axis=1)
    hid = jnp.sin(fr * (z + b1_ref[...]))
    for j in range(wm_ref.shape[0]):
        hid = jnp.sin(fr * (_dot(hid.astype(BF16), wm_ref[j]) + bm_ref[j]))
    h_ref[0:h2, :] = hid[:, :HY_HID]
    h_ref[h2:tr, :] = hid[:, HY_HID:]


def filter_hidden(fvec, w1, b1, wm, bm, freq, *, seq_len, tr):
    full = lambda shp: pl.BlockSpec(shp, lambda i: (0,) * len(shp))
    return pl.pallas_call(
        functools.partial(_filter_hidden_kernel, seq_len=seq_len, tr=tr),
        grid=(2 * seq_len // tr,),
        in_specs=[full((2, 128)), full(w1.shape), full(b1.shape), full(wm.shape), full(bm.shape), full(freq.shape)],
        out_specs=pl.BlockSpec((tr, HY_HID), lambda i: (i, 0)),
        out_shape=jax.ShapeDtypeStruct((2 * seq_len, HY_HID), F32),
        compiler_params=_cparams("parallel"),
        name="hyena_filter_hidden",
    )(fvec, w1, b1, wm, bm, freq)


ROW_GROUP = 16


def _filter_stage1_kernel(f_ref, hid_ref, wo_ref, delta_ref, o_ref, abs_ref, os_ref, *, seq_len):
    i = pl.program_id(1)
    n1 = hid_ref.shape[1]
    half = n1 // 2
    n1_idx = lax.broadcasted_iota(jnp.int32, (n1, 1), 0)
    delta = delta_ref[...]
    f = f_ref[...]
    wo_past, wo_future = wo_ref[0], wo_ref[1]

    @pl.when(i == 0)
    def _():
        abs_ref[...] = jnp.zeros(abs_ref.shape, F32)

    asum = jnp.zeros(abs_ref.shape, F32)
    for r in range(ROW_GROUP):
        hb = hid_ref[r].astype(BF16)
        k = jnp.concatenate([_dot(hb[:half], wo_past), _dot(hb[half:], wo_future)], axis=0)
        n = n1_idx * FFT_N2 + (i * ROW_GROUP + r)
        lag = jnp.where(n < seq_len, n, 2 * seq_len - n).astype(F32)
        k = k * jnp.exp(-(lag / float(seq_len - 1)) * delta)
        k = jnp.where(n == seq_len, 0.0, k)
        asum = asum + jnp.sum(jnp.abs(k), axis=0, keepdims=True)
        os_ref[:, r, :] = _dot(f, k.astype(BF16))
    abs_ref[...] += asum
    o_ref[...] = os_ref[...].astype(o_ref.dtype)


def filter_stage1(f1, hid3, wo, delta, *, seq_len, tn):
    mrows, n1 = f1.shape
    dd = delta.shape[1]
    return pl.pallas_call(
        functools.partial(_filter_stage1_kernel, seq_len=seq_len),
        grid=(dd // tn, FFT_N2 // ROW_GROUP),
        in_specs=[pl.BlockSpec((mrows, n1), lambda j, i: (0, 0)),
                  pl.BlockSpec((ROW_GROUP, n1, HY_HID), lambda j, i: (i, 0, 0)),
                  pl.BlockSpec((2, HY_HID, tn), lambda j, i: (0, 0, j)),
                  pl.BlockSpec((1, tn), lambda j, i: (0, j))],
        out_specs=[pl.BlockSpec((mrows, ROW_GROUP, tn), lambda j, i: (0, i, j)),
                   pl.BlockSpec((1, tn), lambda j, i: (0, j))],
        out_shape=[jax.ShapeDtypeStruct((mrows, FFT_N2, dd), BF16), jax.ShapeDtypeStruct((1, dd), F32)],
        scratch_shapes=[pltpu.VMEM((mrows, ROW_GROUP, tn), F32)],
        compiler_params=_cparams("parallel", "arbitrary"),
        name="filter_stage1",
    )(f1, hid3, wo, delta)


def _fft1_kernel(f_ref, x_ref, o_ref, xs_ref, os_ref):
    for r in range(ROW_GROUP):
        xs_ref[...] = x_ref[:, r, :]
        os_ref[:, r, :] = _dot(f_ref[...], xs_ref[...].astype(BF16))
    o_ref[...] = os_ref[...].astype(o_ref.dtype)


def _fft1_inv_kernel(g_ref, b_ref, v_ref, gate_ref, fb_ref, o_ref, bf_ref, xs_ref):
    bf_ref[...] = b_ref[...].astype(F32)
    for r in range(ROW_GROUP):
        xs_ref[...] = bf_ref[:, r, :]
        o_ref[:, r, :] = _dot(g_ref[...], xs_ref[...].astype(BF16))
    o_ref[...] = gate_ref[...] * (o_ref[...] + fb_ref[...] * v_ref[...])


def fft_stage1(f1, x3, *, d, tn, col_ofs=0):
    mrows, k = f1.shape
    n2 = x3.shape[1]
    nd = d // tn
    return pl.pallas_call(
        _fft1_kernel,
        grid=(n2 // ROW_GROUP, nd),
        in_specs=[pl.BlockSpec((mrows, k), lambda i, j: (0, 0)),
                  pl.BlockSpec((k, ROW_GROUP, tn), lambda i, j: (0, i, col_ofs * nd + j))],
        out_specs=pl.BlockSpec((mrows, ROW_GROUP, tn), lambda i, j: (0, i, j)),
        out_shape=jax.ShapeDtypeStruct((mrows, n2, d), BF16),
        scratch_shapes=[pltpu.VMEM((k, tn), F32), pltpu.VMEM((mrows, ROW_GROUP, tn), F32)],
        compiler_params=_cparams("parallel", "arbitrary"),
        name="fft_stage1",
    )(f1, x3)


def fft_stage1_inv(g1, b3, v3, v_ofs, gate3, gate_ofs, fbias, *, d, tn):
    mrows, k = g1.shape
    n2 = b3.shape[1]
    nd = d // tn
    return pl.pallas_call(
        _fft1_inv_kernel,
        grid=(n2 // ROW_GROUP, nd),
        in_specs=[pl.BlockSpec((mrows, k), lambda i, j: (0, 0)),
                  pl.BlockSpec((k, ROW_GROUP, tn), lambda i, j: (0, i, j)),
                  pl.BlockSpec((mrows, ROW_GROUP, tn), lambda i, j: (0, i, v_ofs * nd + j)),
                  pl.BlockSpec((mrows, ROW_GROUP, tn), lambda i, j: (0, i, gate_ofs * nd + j)),
                  pl.BlockSpec((1, tn), lambda i, j: (0, j))],
        out_specs=pl.BlockSpec((mrows, ROW_GROUP, tn), lambda i, j: (0, i, j)),
        out_shape=jax.ShapeDtypeStruct((mrows, n2, d), F32),
        scratch_shapes=[pltpu.VMEM((k, ROW_GROUP, tn), F32), pltpu.VMEM((k, tn), F32)],
        compiler_params=_cparams("parallel", "arbitrary"),
        name="fft_stage1_inv",
    )(g1, b3, v3, gate3, fbias)


def _fftmid_kernel(a_ref, ak_ref, mf_ref, mi_ref, scale_ref, o_ref):
    half = FFT_N2
    scale = scale_ref[...]
    for s in range(SLAB_TILE):
        mf = mf_ref[s]
        y = _dot(mf, a_ref[s])
        kf = _dot(mf, ak_ref[s])
        yr, yi = y[:half], y[half:]
        kr, ki = kf[:half], kf[half:]
        pr = (yr * kr - yi * ki) * scale
        pi = (yr * ki + yi * kr) * scale
        o_ref[s] = _dot(mi_ref[s], jnp.concatenate([pr, pi], axis=0).astype(BF16)).astype(o_ref.dtype)


def fft_mid(a3, ak3, ak_ofs, mf, mi, scale, *, dc):
    nsp, rows, d = a3.shape
    nd = d // dc
    blk = pl.BlockSpec((SLAB_TILE, rows, dc), lambda s, j: (s, 0, j))
    mat = pl.BlockSpec((SLAB_TILE, rows, rows), lambda s, j: (s, 0, 0))
    return pl.pallas_call(
        _fftmid_kernel,
        grid=(nsp // SLAB_TILE, nd),
        in_specs=[blk, pl.BlockSpec((SLAB_TILE, rows, dc), lambda s, j: (s, 0, ak_ofs * nd + j)), mat, mat,
                  pl.BlockSpec((1, dc), lambda s, j: (0, j))],
        out_specs=blk,
        out_shape=jax.ShapeDtypeStruct((nsp, rows, d), BF16),
        compiler_params=_cparams("parallel", "arbitrary"),
        name="fft_mid",
    )(a3, ak3, mf, mi, scale)


def _fft_tables(seq_len):
    n = 2 * seq_len
    n2 = FFT_N2
    n1 = n // n2
    ns = n1 // 2 + 1
    nsp = -(-ns // SLAB_TILE) * SLAB_TILE
    k1 = jnp.arange(nsp, dtype=jnp.int32)
    valid = (k1 < ns)
    m1 = jnp.arange(n1, dtype=jnp.int32)
    ang1 = (2.0 * math.pi / n1) * lax.rem(k1[:, None] * m1[None, :], n1).astype(F32)
    vf = valid[:, None].astype(F32)
    f1 = jnp.stack([jnp.cos(ang1) * vf, -jnp.sin(ang1) * vf], axis=1).reshape(2 * nsp, n1)
    wgt = jnp.where((k1 == 0) | (k1 == n1 // 2), 1.0, 2.0) * valid.astype(F32) / n
    g1 = jnp.stack([jnp.cos(ang1) * wgt[:, None], -jnp.sin(ang1) * wgt[:, None]], axis=1)
    g1 = g1.reshape(2 * nsp, n1).T[: n1 // 2]
    k2 = jnp.arange(n2, dtype=jnp.int32)
    m2 = jnp.arange(n2, dtype=jnp.int32)
    f = k1[:, None, None] + n1 * k2[None, :, None]
    ang2 = (2.0 * math.pi / n) * lax.rem(f * m2[None, None, :], n).astype(F32)
    vm = valid[:, None, None].astype(F32)
    c2, s2 = jnp.cos(ang2) * vm, jnp.sin(ang2) * vm
    mf = jnp.concatenate([jnp.concatenate([c2, s2], axis=2), jnp.concatenate([-s2, c2], axis=2)], axis=1)
    c2t, s2t = jnp.swapaxes(c2, 1, 2), jnp.swapaxes(s2, 1, 2)
    mi = jnp.concatenate([jnp.concatenate([c2t, -s2t], axis=2), jnp.concatenate([s2t, c2t], axis=2)], axis=1)
    return dict(n1=n1, nsp=nsp, f1=f1.astype(BF16), g1=g1.astype(BF16), mf=mf.astype(BF16), mi=mi.astype(BF16))


def long_conv_gate(v, v_cols, gate, gate_cols, fbias, ak3, ak_cols, scale, tabs, *, seq_len, d):
    n1, nsp = tabs["n1"], tabs["nsp"]
    half = n1 // 2
    v3 = v.reshape(half, FFT_N2, v.shape[1])
    g3 = gate.reshape(half, FFT_N2, gate.shape[1])
    a = fft_stage1(tabs["f1"][:, :half], v3, d=d, tn=256, col_ofs=v_cols)
    b3 = fft_mid(a.reshape(nsp, 2 * FFT_N2, d), ak3, ak_cols, tabs["mf"], tabs["mi"], scale, dc=512)
    out = fft_stage1_inv(tabs["g1"], b3.reshape(2 * nsp, FFT_N2, d), v3, v_cols, g3, gate_cols, fbias, d=d, tn=256)
    return out.reshape(seq_len, d)


def _rope_tables(seq_len, scale):
    n_freq = MLA_ROPE // 4
    rows = seq_len // GRID_W
    row = jnp.repeat(jnp.arange(rows, dtype=F32), GRID_W)
    col = jnp.tile(jnp.arange(GRID_W, dtype=F32), rows)
    inv = ROPE_THETA ** (-jnp.arange(n_freq, dtype=F32) / n_freq)
    ang = jnp.stack([row[:, None] * inv, col[:, None] * inv], axis=1)
    cos = jnp.broadcast_to(jnp.cos(ang)[:, :, None, :], (seq_len, 2, 2, n_freq)).reshape(seq_len, MLA_ROPE)
    sin = jnp.broadcast_to(jnp.sin(ang)[:, :, None, :], (seq_len, 2, 2, n_freq)).reshape(seq_len, MLA_ROPE)
    one = jnp.ones((seq_len, MLA_NOPE), F32)
    zero = jnp.zeros((seq_len, MLA_NOPE), F32)
    z64 = jnp.zeros((seq_len, MLA_HEAD_PAD - MLA_NOPE - MLA_ROPE), F32)
    ta = jnp.concatenate([one, cos, z64], axis=1) * scale
    tb = jnp.concatenate([zero, sin, z64], axis=1) * scale
    tc = jnp.concatenate([cos, z64], axis=1)
    ts = jnp.concatenate([sin, z64], axis=1)
    return ta, tb, tc, ts


def _rope_swap_cols(w):
    n_freq = MLA_ROPE // 4
    w4 = w.reshape(w.shape[0], 2, 2, n_freq)
    return jnp.stack([-w4[:, :, 1], w4[:, :, 0]], axis=2).reshape(w.shape[0], MLA_ROPE)


def _even_layer(x, ctx, mods, norm_mix_w, w_in, conv_w, conv_b, dt_bias, a_log, d_skip, ssd_norm_w,
                q_norm_w, w_uq, kv_norm_w, w_ukv, w_o):
    seq_len, d = x.shape
    sh1, sc1, g1 = (mods[0:1, i * d:(i + 1) * d] for i in range(3))
    csh1, csc1 = mods[1:2, 0:d], mods[1:2, d:2 * d]

    o1 = Q_SIDE + SSD_XBC
    o2 = o1 + SSD_DT
    o3 = o2 + MLA_KV_RANK
    w_kr = w_in[:, o3:]
    w_ext = jnp.concatenate([w_in[:, :Q_SIDE], w_in[:, Q_SIDE:o1], w_in[:, o2:o3], w_kr, _rope_swap_cols(w_kr),
                             w_in[:, o1:o2], jnp.zeros((d, P_COLS - P_DT - SSD_DT), F32)], axis=1).astype(BF16)
    zb = jnp.zeros((P_COLS,), F32)
    p_lat = normmm(x, norm_mix_w, sh1, sc1, w_ext, zb, tm=_pick(seq_len, (1024, 512)), tn=768, out_dtype=F32)
    p_ctx = normmm(ctx, norm_mix_w, csh1, csc1, w_ext, zb, tm=CTX_LEN, tn=768, out_dtype=F32)

    scale = float(MLA_NOPE + MLA_ROPE) ** -0.5 * math.log2(math.e)
    ta, tb, tc, ts = _rope_tables(seq_len, scale)
    wq = w_uq.reshape(MLA_Q_RANK, MLA_HEADS, MLA_NOPE + MLA_ROPE)
    zpad = jnp.zeros((MLA_Q_RANK, MLA_HEADS, MLA_HEAD_PAD - MLA_NOPE - MLA_ROPE), F32)
    wa = jnp.concatenate([wq, zpad], axis=2).reshape(MLA_Q_RANK, -1).astype(BF16)
    wr = wq[:, :, MLA_NOPE:]
    n_freq = MLA_ROPE // 4
    wr4 = wr.reshape(MLA_Q_RANK, MLA_HEADS, 2, 2, n_freq)
    wsw = jnp.stack([-wr4[:, :, :, 1], wr4[:, :, :, 0]], axis=3).reshape(MLA_Q_RANK, MLA_HEADS, MLA_ROPE)
    wb = jnp.concatenate([jnp.zeros((MLA_Q_RANK, MLA_HEADS, MLA_NOPE), F32), wsw, zpad], axis=2)
    wb = wb.reshape(MLA_Q_RANK, -1).astype(BF16)
    q = qproj(p_lat, q_norm_w, wa, wb, ta, tb, tm=512)

    zk = jnp.zeros((MLA_KV_RANK,), F32)
    w_ukv_b = w_ukv.astype(BF16)
    zkb = jnp.zeros((w_ukv.shape[1],), F32)
    kv_lat = normmm(p_lat, kv_norm_w, zk, zk, w_ukv_b, zkb, tm=512, tn=w_ukv.shape[1], out_dtype=BF16,
                    x_col=P_CKV // MLA_KV_RANK)
    kv_ctx = normmm(p_ctx, kv_norm_w, zk, zk, w_ukv_b, zkb, tm=CTX_LEN, tn=w_ukv.shape[1], out_dtype=BF16,
                    x_col=P_CKV // MLA_KV_RANK)
    k_lat, v_lat = kv_assemble(kv_lat, p_lat, tc, ts, tm=512, rope=True)
    k_ctx, v_ctx = kv_assemble(kv_ctx, p_ctx, tc, ts, tm=CTX_LEN, rope=False)
    k_all = jnp.concatenate([k_lat, k_ctx], axis=0)
    v_all = jnp.concatenate([v_lat, v_ctx], axis=0)
    s_tot = seq_len + CTX_LEN
    o_att = attention(q, k_all, v_all, tq=_pick(seq_len, (2048, 1024, 512, 256)),
                      tk=_pick(s_tot, (1280, 1024, 768, 640, 512, 384, 256, 128)))

    xbc_lat = dwconv3(p_lat, conv_w, conv_b, tm=512, tc=SSD_XBC, col0=P_XBC, width=SSD_XBC, act=True)
    xbc_ctx = dwconv3(p_ctx, conv_w, conv_b, tm=CTX_LEN, tc=SSD_XBC, col0=P_XBC, width=SSD_XBC, act=True)
    dt_all = jnp.concatenate([p_lat[:, P_DT:P_DT + SSD_DT], p_ctx[:, P_DT:P_DT + SSD_DT]], axis=0)
    nch = s_tot // SSD_CHUNK
    dt2 = dt_all.reshape(s_tot, 2, SSD_HEADS).transpose(1, 0, 2)
    dt2t = dt2.reshape(2, nch, SSD_CHUNK, SSD_HEADS).transpose(0, 1, 3, 2)
    bias2 = dt_bias.reshape(2, 1, SSD_HEADS)
    bias2t = dt_bias.reshape(2, SSD_HEADS, 1)
    a_neg = -jnp.exp(a_log.astype(F32))
    a2 = a_neg.reshape(2, 1, SSD_HEADS)
    a2t = a_neg.reshape(2, SSD_HEADS, 1)
    lower = jnp.tril(jnp.ones((SSD_CHUNK, SSD_CHUNK), F32))
    tri2 = jnp.stack([lower, lower.T]).astype(BF16)
    yf, yb = ssd_scan(xbc_lat, xbc_ctx, dt2, dt2t, bias2, bias2t, a2, a2t, tri2, n_lat_chunks=seq_len // SSD_CHUNK)

    dsk = jnp.repeat(d_skip[0] + d_skip[1], SSD_HEAD_DIM).reshape(1, SSD_INNER)
    return merge_out(o_att, yf, yb, xbc_lat, p_lat, dsk, ssd_norm_w.reshape(1, SSD_INNER), w_o.astype(BF16), x, g1,
                     tm=256, tn=d)


def _odd_layer(x, mods, norm_mix_w, w_in, b_in, short_w, short_b, fw1, fb1, fw_mid, fb_mid, freq, fw_out,
               fbias, w_out, b_out):
    seq_len, d = x.shape
    sh1, sc1, g1 = (mods[0:1, i * d:(i + 1) * d] for i in range(3))
    pc = normmm_conv(x, norm_mix_w, sh1, sc1, w_in.astype(BF16), b_in, short_w, short_b,
                     tm=_pick(seq_len, (1024, 512)), tn=768)

    tabs = _fft_tables(seq_len)
    bands = (HY_EMB - 1) // 2
    fband = jnp.linspace(1e-4, bands - 1, bands, dtype=F32)
    zpad = jnp.zeros((128 - HY_EMB,), F32)
    fvec = jnp.stack([jnp.concatenate([jnp.zeros((1,), F32), fband, fband, zpad]),
                      jnp.concatenate([jnp.zeros((1,), F32), jnp.full((bands,), 0.5 * math.pi, F32),
                                       jnp.full((bands,), math.pi, F32), zpad])])
    w1p = jnp.concatenate([fw1.astype(F32), jnp.zeros((128 - HY_EMB, HY_HID), F32)], axis=0).astype(BF16)
    lo = math.log(HY_SLOW_DECAY) / HY_TARGET
    hi = math.log(HY_FAST_DECAY) / HY_TARGET
    delta = jnp.abs(jnp.linspace(lo, hi, d, dtype=F32)).reshape(1, d)
    n_ord = fw_out.shape[1]
    wo = jnp.transpose(fw_out, (2, 0, 1, 3)).reshape(2, HY_HID, n_ord * d).astype(BF16)
    two = lambda a: jnp.concatenate([a, a], axis=-1)
    zmid = jnp.zeros_like(fw_mid)
    wm_bd = jnp.concatenate([jnp.concatenate([fw_mid, zmid], axis=2), jnp.concatenate([zmid, fw_mid], axis=2)],
                            axis=1).astype(BF16)
    hid = filter_hidden(fvec, w1p, two(fb1.reshape(1, HY_HID)), wm_bd, two(fb_mid.reshape(-1, 1, HY_HID)),
                        two(freq.reshape(1, HY_HID)), seq_len=seq_len, tr=1024)
    ak, kabs = filter_stage1(tabs["f1"], hid.reshape(FFT_N2, tabs["n1"], HY_HID), wo, jnp.tile(delta, (1, n_ord)),
                             seq_len=seq_len, tn=256)
    ak3 = ak.reshape(tabs["nsp"], 2 * FFT_N2, n_ord * d)
    y_cols = 2
    y = None
    for i in range(n_ord):
        scale = 1.0 / kabs[:, i * d:(i + 1) * d]
        fb = fbias[i].reshape(1, d)
        if y is None:
            y = long_conv_gate(pc, y_cols, pc, i, fb, ak3, i, scale, tabs, seq_len=seq_len, d=d)
        else:
            y = long_conv_gate(y, 0, pc, i, fb, ak3, i, scale, tabs, seq_len=seq_len, d=d)
    y = y.reshape(seq_len, d)
    return mm_res(y, w_out.astype(BF16), b_out, x, g1, tm=512, tn=d)


def kernel(x, c, ctx, c_ctx, mod_w, mod_b, norm_mix_w, norm_ffn_w, ffn_w1, ffn_w3, ffn_w2, ev_w_in, ev_conv_w, ev_conv_b, ev_dt_bias, ev_a_log, ev_d_skip, ev_ssd_norm_w, ev_q_norm_w, ev_w_uq, ev_kv_norm_w, ev_w_ukv, ev_w_o, hy_w_in, hy_b_in, hy_short_w, hy_short_b, hy_fw1, hy_fb1, hy_fw_mid, hy_fb_mid, hy_freq, hy_fw_out, hy_fbias, hy_w_out, hy_b_out, final_norm_w):
    assert x.shape[0] == 1 and mod_w.shape[0] == 2
    xs = x[0]
    xc = ctx[0]
    d = xs.shape[1]
    vecs = jnp.concatenate([c.reshape(1, d), c_ctx.reshape(1, d), jnp.zeros((6, d), F32)], axis=0)
    depth = mod_w.shape[0]
    w1_b, w3_b, w2_b = ffn_w1.astype(BF16), ffn_w3.astype(BF16), ffn_w2.astype(BF16)
    for i in range(depth):
        mods = adaln_vectors(vecs, mod_w, mod_b, i)
        sh2, sc2, g2 = (mods[0:1, j * d:(j + 1) * d] for j in range(3, 6))
        if i % 2 == 0:
            e = i // 2
            xs = _even_layer(xs, xc, mods, norm_mix_w[i], ev_w_in[e], ev_conv_w[e], ev_conv_b[e], ev_dt_bias[e],
                             ev_a_log[e], ev_d_skip[e], ev_ssd_norm_w[e], ev_q_norm_w[e], ev_w_uq[e],
                             ev_kv_norm_w[e], ev_w_ukv[e], ev_w_o[e])
        else:
            o = i // 2
            xs = _odd_layer(xs, mods, norm_mix_w[i], hy_w_in[o], hy_b_in[o], hy_short_w[o], hy_short_b[o],
                            hy_fw1[o], hy_fb1[o], hy_fw_mid[o], hy_fb_mid[o], hy_freq[o], hy_fw_out[o],
                            hy_fbias[o], hy_w_out[o], hy_b_out[o])
        xs = ffn(xs, norm_ffn_w[i], sh2, sc2, g2, w1_b, w3_b, w2_b, final_norm_w, layer=i, tm=512, tf=512,
                 final_norm=(i == depth - 1))
    return xs[None]
```

```python
import functools
import math

import jax
import jax.numpy as jnp
from jax import lax
from jax.experimental import pallas as pl
from jax.experimental.pallas import tpu as pltpu

F32 = jnp.float32
BF16 = jnp.bfloat16

EPS = 1e-6
GRID_W = 64
CTX_LEN = 256
SSD_HEADS = 16
SSD_HEAD_DIM = 64
SSD_INNER = 1024
SSD_GROUPS = 2
SSD_HPG = 8
SSD_STATE = 128
SSD_CHUNK = 256
SSD_XBC = 1536
SSD_DT = 32
MLA_HEADS = 8
MLA_NOPE = 128
MLA_ROPE = 64
MLA_V = 128
MLA_Q_RANK = 512
MLA_KV_RANK = 512
ROPE_THETA = 10000.0
MLA_HEAD_PAD = 256
Q_SIDE = SSD_INNER + MLA_Q_RANK
HY_EMB = 33
HY_HID = 64
HY_FAST_DECAY = 0.3
HY_SLOW_DECAY = 1.5
HY_TARGET = 1e-2
FFT_N2 = 128
SLAB_TILE = 8

P_Z, P_CQ, P_XBC, P_CKV, P_KR, P_DT, P_COLS = 0, 1024, 1536, 3072, 3584, 3712, 3840

VMEM_LIMIT = 56 * 1024 * 1024


def _cparams(*sem):
    return pltpu.CompilerParams(dimension_semantics=sem, vmem_limit_bytes=VMEM_LIMIT)


def _pick(n, cands):
    for c in cands:
        if n % c == 0:
            return c
    raise ValueError(f"no tile for {n}")


def _split(a):
    hi = a.astype(BF16)
    lo = (a - hi.astype(F32)).astype(BF16)
    return hi, lo


def _dot(a, b):
    return jnp.dot(a, b, preferred_element_type=F32)


def _silu(x):
    return x * (1.0 / (1.0 + jnp.exp(-x)))


def _modnorm(x, nw, sh, sc):
    ms = jnp.mean(x * x, axis=-1, keepdims=True)
    return (x * lax.rsqrt(ms + EPS) * nw) * (1.0 + sc) + sh


def _matvec_kernel(x_ref, w_ref, b_ref, o_ref):
    x = x_ref[...]
    o_ref[...] = _dot(_silu(x).astype(BF16), w_ref[...].astype(BF16)) + b_ref[...]


def adaln_vectors(vecs, w_all, b_all, layer):
    nl, k, n = w_all.shape
    tn = 1024
    return pl.pallas_call(
        _matvec_kernel,
        grid=(n // tn,),
        in_specs=[pl.BlockSpec((8, k), lambda j: (0, 0)),
                  pl.BlockSpec((None, k, tn), lambda j: (layer, 0, j)),
                  pl.BlockSpec((None, 1, tn), lambda j: (layer, 0, j))],
        out_specs=pl.BlockSpec((8, tn), lambda j: (0, j)),
        out_shape=jax.ShapeDtypeStruct((8, n), F32),
        compiler_params=_cparams("arbitrary"),
        name="adaln_vectors",
    )(vecs, w_all, b_all.reshape(nl, 1, n))


def _normmm_kernel(x_ref, nw_ref, sh_ref, sc_ref, w_ref, b_ref, o_ref, xn_ref):
    @pl.when(pl.program_id(1) == 0)
    def _():
        xn_ref[...] = _modnorm(x_ref[...].astype(F32), nw_ref[...], sh_ref[...], sc_ref[...]).astype(BF16)

    o_ref[...] = (_dot(xn_ref[...], w_ref[...]) + b_ref[...]).astype(o_ref.dtype)


def normmm(x, nw, sh, sc, w, b, *, tm, tn, out_dtype, x_col=0):
    m = x.shape[0]
    k, n = w.shape
    return pl.pallas_call(
        _normmm_kernel,
        grid=(m // tm, n // tn),
        in_specs=[pl.BlockSpec((tm, k), lambda i, j: (i, x_col)),
                  pl.BlockSpec((1, k), lambda i, j: (0, 0)),
                  pl.BlockSpec((1, k), lambda i, j: (0, 0)),
                  pl.BlockSpec((1, k), lambda i, j: (0, 0)),
                  pl.BlockSpec((k, tn), lambda i, j: (0, j)),
                  pl.BlockSpec((1, tn), lambda i, j: (0, j))],
        out_specs=pl.BlockSpec((tm, tn), lambda i, j: (i, j)),
        out_shape=jax.ShapeDtypeStruct((m, n), out_dtype),
        scratch_shapes=[pltpu.VMEM((tm, k), BF16)],
        compiler_params=_cparams("parallel", "arbitrary"),
        name="normmm",
    )(x, nw.reshape(1, k), sh.reshape(1, k), sc.reshape(1, k), w, b.reshape(1, n))


def _normmm_conv_kernel(x_ref, xprev_ref, xnext_ref, nw_ref, sh_ref, sc_ref, w_ref, b_ref, cw_ref, cb_ref,
                        o_ref, xn_ref, xh_ref):
    i = pl.program_id(0)

    @pl.when(pl.program_id(1) == 0)
    def _():
        nw, sh, sc = nw_ref[...], sh_ref[...], sc_ref[...]
        xn_ref[...] = _modnorm(x_ref[...], nw, sh, sc).astype(BF16)
        xh_ref[0:8, :] = _modnorm(xprev_ref[...], nw, sh, sc).astype(BF16)
        xh_ref[8:16, :] = _modnorm(xnext_ref[...], nw, sh, sc).astype(BF16)

    w = w_ref[...]
    b = b_ref[...]
    y = _dot(xn_ref[...], w) + b
    yh = _dot(xh_ref[...], w) + b
    prev_row = jnp.where(i == 0, 0.0, yh[7:8, :])
    next_row = jnp.where(i == pl.num_programs(0) - 1, 0.0, yh[8:9, :])
    yp, yn = _shift_rows(y, prev_row, next_row)
    o_ref[...] = yp * cw_ref[0:1, :] + y * cw_ref[1:2, :] + yn * cw_ref[2:3, :] + cb_ref[...]


def normmm_conv(x, nw, sh, sc, w, b, cw, cb, *, tm, tn):
    m, k = x.shape
    n = w.shape[1]
    nrb = m // 8
    tb = tm // 8
    vec = pl.BlockSpec((1, k), lambda i, j: (0, 0))
    return pl.pallas_call(
        _normmm_conv_kernel,
        grid=(m // tm, n // tn),
        in_specs=[pl.BlockSpec((tm, k), lambda i, j: (i, 0)),
                  pl.BlockSpec((8, k), lambda i, j: (jnp.maximum(i * tb - 1, 0), 0)),
                  pl.BlockSpec((8, k), lambda i, j: (jnp.minimum((i + 1) * tb, nrb - 1), 0)),
                  vec, vec, vec,
                  pl.BlockSpec((k, tn), lambda i, j: (0, j)),
                  pl.BlockSpec((1, tn), lambda i, j: (0, j)),
                  pl.BlockSpec((3, tn), lambda i, j: (0, j)),
                  pl.BlockSpec((1, tn), lambda i, j: (0, j))],
        out_specs=pl.BlockSpec((tm, tn), lambda i, j: (i, j)),
        out_shape=jax.ShapeDtypeStruct((m, n), F32),
        scratch_shapes=[pltpu.VMEM((tm, k), BF16), pltpu.VMEM((16, k), BF16)],
        compiler_params=_cparams("parallel", "arbitrary"),
        name="normmm_conv",
    )(x, x, x, nw.reshape(1, k), sh.reshape(1, k), sc.reshape(1, k), w, b.reshape(1, n), cw, cb.reshape(1, n))


def _qproj_kernel(x_ref, nw_ref, wa_ref, wb_ref, ta_ref, tb_ref, o_ref):
    x = x_ref[...]
    ms = jnp.mean(x * x, axis=-1, keepdims=True)
    xn = (x * lax.rsqrt(ms + EPS) * nw_ref[...]).astype(BF16)
    ta = ta_ref[...]
    tb = tb_ref[...]
    hp = MLA_HEAD_PAD
    for h in range(MLA_HEADS):
        cols = slice(h * hp, (h + 1) * hp)
        o_ref[:, cols] = (_dot(xn, wa_ref[:, cols]) * ta + _dot(xn, wb_ref[:, cols]) * tb).astype(o_ref.dtype)


def qproj(p, nw, wa, wb, ta, tb, *, tm):
    m = ta.shape[0]
    k = MLA_Q_RANK
    hp = MLA_HEAD_PAD
    n = MLA_HEADS * hp
    return pl.pallas_call(
        _qproj_kernel,
        grid=(m // tm,),
        in_specs=[pl.BlockSpec((tm, k), lambda i: (i, P_CQ // MLA_Q_RANK)),
                  pl.BlockSpec((1, k), lambda i: (0, 0)),
                  pl.BlockSpec((k, n), lambda i: (0, 0)),
                  pl.BlockSpec((k, n), lambda i: (0, 0)),
                  pl.BlockSpec((tm, hp), lambda i: (i, 0)),
                  pl.BlockSpec((tm, hp), lambda i: (i, 0))],
        out_specs=pl.BlockSpec((tm, n), lambda i: (i, 0)),
        out_shape=jax.ShapeDtypeStruct((m, n), BF16),
        compiler_params=_cparams("parallel"),
        name="mla_qproj",
    )(p, nw.reshape(1, k), wa, wb, ta, tb)


def _kv_assemble_kernel(kv_ref, kr_ref, tc_ref, ts_ref, k_ref, v_ref, *, rope):
    krr = kr_ref[...]
    kr = krr[:, :MLA_ROPE]
    if rope:
        kr = kr * tc_ref[...][:, :MLA_ROPE] + krr[:, MLA_ROPE:] * ts_ref[...][:, :MLA_ROPE]
    tail = jnp.concatenate([kr, jnp.zeros_like(kr)], axis=-1).astype(BF16)
    for h in range(MLA_HEADS):
        base = h * (MLA_NOPE + MLA_V)
        k_ref[:, h * MLA_HEAD_PAD:h * MLA_HEAD_PAD + MLA_NOPE] = kv_ref[:, base:base + MLA_NOPE]
        k_ref[:, h * MLA_HEAD_PAD + MLA_NOPE:(h + 1) * MLA_HEAD_PAD] = tail
        v_ref[:, h * MLA_V:(h + 1) * MLA_V] = kv_ref[:, base + MLA_NOPE:base + MLA_NOPE + MLA_V]


def kv_assemble(kvp, p, tc, ts, *, tm, rope):
    m = kvp.shape[0]
    return pl.pallas_call(
        functools.partial(_kv_assemble_kernel, rope=rope),
        grid=(m // tm,),
        in_specs=[pl.BlockSpec((tm, MLA_HEADS * (MLA_NOPE + MLA_V)), lambda i: (i, 0)),
                  pl.BlockSpec((tm, 128), lambda i: (i, P_KR // 128)),
                  pl.BlockSpec((tm, 128), lambda i: (i, 0)),
                  pl.BlockSpec((tm, 128), lambda i: (i, 0))],
        out_specs=[pl.BlockSpec((tm, MLA_HEADS * MLA_HEAD_PAD), lambda i: (i, 0)),
                   pl.BlockSpec((tm, MLA_HEADS * MLA_V), lambda i: (i, 0))],
        out_shape=[jax.ShapeDtypeStruct((m, MLA_HEADS * MLA_HEAD_PAD), BF16),
                   jax.ShapeDtypeStruct((m, MLA_HEADS * MLA_V), BF16)],
        compiler_params=_cparams("parallel"),
        name="mla_kv_assemble",
    )(kvp, p, tc, ts)


def _attn_kernel(q_ref, k_ref, v_ref, o_ref, s0_ref, s1_ref, m_ref, acc_ref, *, nk):
    j = pl.program_id(2)

    def qk(s_ref):
        s_ref[...] = lax.dot_general(q_ref[...], k_ref[...], (((1,), (1,)), ((), ())), preferred_element_type=F32)

    def softmax_pv(s_ref):
        s = s_ref[...]
        m_prev = m_ref[...]
        m_new = jnp.maximum(m_prev, jnp.max(s, axis=-1, keepdims=True))
        alpha = jnp.exp2(m_prev - m_new)
        p = jnp.exp2(s - m_new).astype(BF16)
        v = v_ref[...]
        lane = lax.broadcasted_iota(jnp.int32, v.shape, 1)
        ones_col = jnp.where(lane == 0, 1.0, 0.0).astype(BF16)
        v_ext = jnp.concatenate([v, ones_col], axis=1)
        acc_ref[...] = alpha * acc_ref[...] + _dot(p, v_ext)
        m_ref[...] = m_new

    @pl.when(j == 0)
    def _():
        m_ref[...] = jnp.full(m_ref.shape, -jnp.inf, F32)
        acc_ref[...] = jnp.zeros(acc_ref.shape, F32)
        qk(s0_ref)

    mid = jnp.logical_and(j > 0, j < nk)

    @pl.when(jnp.logical_and(mid, j % 2 == 1))
    def _():
        softmax_pv(s0_ref)
        qk(s1_ref)

    @pl.when(jnp.logical_and(mid, j % 2 == 0))
    def _():
        softmax_pv(s1_ref)
        qk(s0_ref)

    @pl.when(j == nk)
    def _():
        softmax_pv(s1_ref if (nk - 1) % 2 else s0_ref)
        acc = acc_ref[...]
        o_ref[...] = (acc[:, :MLA_V] / acc[:, MLA_V:MLA_V + 1]).astype(o_ref.dtype)


def attention(q, k, v, *, tq, tk):
    lq = q.shape[0]
    s = k.shape[0]
    nk = s // tk
    return pl.pallas_call(
        functools.partial(_attn_kernel, nk=nk),
        grid=(MLA_HEADS, lq // tq, nk + 1),
        in_specs=[pl.BlockSpec((tq, MLA_HEAD_PAD), lambda h, i, j: (i, h)),
                  pl.BlockSpec((tk, MLA_HEAD_PAD), lambda h, i, j: (jnp.minimum(j, nk - 1), h)),
                  pl.BlockSpec((tk, MLA_V), lambda h, i, j: (jnp.maximum(j - 1, 0), h))],
        out_specs=pl.BlockSpec((tq, MLA_V), lambda h, i, j: (i, h)),
        out_shape=jax.ShapeDtypeStruct((lq, MLA_HEADS * MLA_V), BF16),
        scratch_shapes=[pltpu.VMEM((tq, tk), F32), pltpu.VMEM((tq, tk), F32), pltpu.VMEM((tq, 1), F32),
                        pltpu.VMEM((tq, 2 * MLA_V), F32)],
        compiler_params=_cparams("parallel", "parallel", "arbitrary"),
        name="mla_attention",
    )(q, k, v)


def _shift_rows(x, prev_row, next_row):
    tm = x.shape[0]
    rows = lax.broadcasted_iota(jnp.int32, x.shape, 0)
    xp = jnp.where(rows == 0, prev_row, pltpu.roll(x, 1, 0))
    xn = jnp.where(rows == tm - 1, next_row, pltpu.roll(x, tm - 1, 0))
    return xp, xn


def _dwconv_kernel(x_ref, xprev_ref, xnext_ref, w_ref, b_ref, o_ref, *, act):
    i = pl.program_id(0)
    x = x_ref[...]
    prev_row = jnp.where(i == 0, 0.0, xprev_ref[7:8, :])
    next_row = jnp.where(i == pl.num_programs(0) - 1, 0.0, xnext_ref[0:1, :])
    xp, xn = _shift_rows(x, prev_row, next_row)
    y = xp * w_ref[0:1, :] + x * w_ref[1:2, :] + xn * w_ref[2:3, :] + b_ref[...]
    if act:
        y = _silu(y)
    o_ref[...] = y


def dwconv3(x, w, b, *, tm, tc, col0, width, act):
    m = x.shape[0]
    cb = col0 // tc
    nrb = m // 8
    tb = tm // 8
    return pl.pallas_call(
        functools.partial(_dwconv_kernel, act=act),
        grid=(m // tm, width // tc),
        in_specs=[pl.BlockSpec((tm, tc), lambda i, j: (i, cb + j)),
                  pl.BlockSpec((8, tc), lambda i, j: (jnp.maximum(i * tb - 1, 0), cb + j)),
                  pl.BlockSpec((8, tc), lambda i, j: (jnp.minimum((i + 1) * tb, nrb - 1), cb + j)),
                  pl.BlockSpec((3, tc), lambda i, j: (0, j)),
                  pl.BlockSpec((1, tc), lambda i, j: (0, j))],
        out_specs=pl.BlockSpec((tm, tc), lambda i, j: (i, j)),
        out_shape=jax.ShapeDtypeStruct((m, width), F32),
        compiler_params=_cparams("arbitrary", "arbitrary"),
        name="dwconv3",
    )(x, x, x, w, b.reshape(1, width))


def _softplus(x):
    return jnp.maximum(x, 0.0) + jnp.log(1.0 + jnp.exp(-jnp.abs(x)))


def _ssd_prepare(d, xbc, dt_ref, dtt_ref, bias_ref, biast_ref, a_ref, at_ref, tri_ref):
    q = SSD_CHUNK
    dt = _softplus(dt_ref[0] + bias_ref[d])
    dtt = _softplus(dtt_ref[0, 0] + biast_ref[d])
    dta_hi, dta_lo = _split(dt * a_ref[d])
    dtat_hi, dtat_lo = _split(dtt * at_ref[d])
    tri = tri_ref[d]
    trit = tri_ref[1 - d]
    acum = _dot(tri, dta_hi) + _dot(tri, dta_lo)
    acumt = _dot(dtat_hi, trit) + _dot(dtat_lo, trit)
    total = acum[q - 1:q, :] if d == 0 else acum[0:1, :]
    return dict(xbc=xbc, dt=dt, acum=acum, acumt=acumt, to_end=jnp.exp(total - acum),
                from_start=jnp.exp(acum), chunk_decay=jnp.exp(total), mask=tri > 0.5)


def _per_head_lanes(a, g):
    return jnp.concatenate([jnp.broadcast_to(a[:, h:h + 1], (a.shape[0], SSD_HEAD_DIM))
                            for h in range(g * SSD_HPG, (g + 1) * SSD_HPG)], axis=1)


def _ssd_group(d, p, g, h_ref):
    xbc = p["xbc"]
    gw = SSD_HPG * SSD_HEAD_DIM
    bm = xbc[:, SSD_INNER + g * SSD_STATE:SSD_INNER + (g + 1) * SSD_STATE]
    cm = xbc[:, SSD_INNER + (SSD_GROUPS + g) * SSD_STATE:SSD_INNER + (SSD_GROUPS + g + 1) * SSD_STATE]
    bm_b = bm.astype(BF16)
    cm_b = cm.astype(BF16)
    cb = lax.dot_general(cm_b, bm_b, (((1,), (1,)), ((), ())), preferred_element_type=F32)
    xdt = xbc[:, g * gw:(g + 1) * gw] * _per_head_lanes(p["dt"], g)
    h_prev = h_ref[d * SSD_GROUPS + g]
    y_off = _dot(cm_b, h_prev.astype(BF16)) * _per_head_lanes(p["from_start"], g)
    states = _dot(bm.T.astype(BF16), (xdt * _per_head_lanes(p["to_end"], g)).astype(BF16))
    h_ref[d * SSD_GROUPS + g] = h_prev * _per_head_lanes(p["chunk_decay"], g) + states
    return dict(cb=cb, xdt_b=xdt.astype(BF16), y_off=y_off)


def _ssd_head(p, grp, g, r, o_ref):
    h = g * SSD_HPG + r
    seg = p["acum"][:, h:h + 1] - p["acumt"][h:h + 1, :]
    decay = jnp.exp(jnp.where(p["mask"], seg, -jnp.inf))
    lanes = slice(r * SSD_HEAD_DIM, (r + 1) * SSD_HEAD_DIM)
    y_diag = _dot((grp["cb"] * decay).astype(BF16), grp["xdt_b"][:, lanes])
    o_ref[:, h * SSD_HEAD_DIM:(h + 1) * SSD_HEAD_DIM] = y_diag + grp["y_off"][:, lanes]


def _ssd_kernel(xf_ref, xfc_ref, xb_ref, xbc_ref, dtf_ref, dtb_ref, dttf_ref, dttb_ref, bias_ref, biast_ref, a_ref, at_ref, tri_ref,
                of_ref, ob_ref, h_ref):
    @pl.when(pl.program_id(0) == 0)
    def _():
        h_ref[...] = jnp.zeros(h_ref.shape, F32)

    s = pl.program_id(0)
    ncx = CTX_LEN // SSD_CHUNK
    x_fwd = jnp.where(s < ncx, xfc_ref[...], xf_ref[...])
    x_bwd = jnp.where(s < ncx, xbc_ref[...], xb_ref[...])
    prm = (_ssd_prepare(0, x_fwd, dtf_ref, dttf_ref, bias_ref, biast_ref, a_ref, at_ref, tri_ref),
           _ssd_prepare(1, x_bwd, dtb_ref, dttb_ref, bias_ref, biast_ref, a_ref, at_ref, tri_ref))
    outs = (of_ref, ob_ref)
    for g in range(SSD_GROUPS):
        grp = (_ssd_group(0, prm[0], g, h_ref), _ssd_group(1, prm[1], g, h_ref))
        for r in range(SSD_HPG):
            for d in range(2):
                _ssd_head(prm[d], grp[d], g, r, outs[d])


def ssd_scan(xbc_lat, xbc_ctx, dt2, dt2t, bias2, bias2t, a2, a2t, tri2, *, n_lat_chunks):
    nc = n_lat_chunks
    ncx = CTX_LEN // SSD_CHUNK
    tot = nc + ncx
    q = SSD_CHUNK

    def cf(s):
        return lax.rem(s + nc, tot)

    def cbk(s):
        return tot - 1 - s

    def full(shape):
        return pl.BlockSpec(shape, lambda s: (0,) * len(shape))

    return pl.pallas_call(
        _ssd_kernel,
        grid=(tot,),
        in_specs=[pl.BlockSpec((q, SSD_XBC), lambda s: (jnp.minimum(cf(s), nc - 1), 0)),
                  pl.BlockSpec((q, SSD_XBC), lambda s: (jnp.maximum(cf(s) - nc, 0), 0)),
                  pl.BlockSpec((q, SSD_XBC), lambda s: (jnp.minimum(cbk(s), nc - 1), 0)),
                  pl.BlockSpec((q, SSD_XBC), lambda s: (jnp.maximum(cbk(s) - nc, 0), 0)),
                  pl.BlockSpec((1, q, SSD_HEADS), lambda s: (0, cf(s), 0)),
                  pl.BlockSpec((1, q, SSD_HEADS), lambda s: (1, cbk(s), 0)),
                  pl.BlockSpec((1, 1, SSD_HEADS, q), lambda s: (0, cf(s), 0, 0)),
                  pl.BlockSpec((1, 1, SSD_HEADS, q), lambda s: (1, cbk(s), 0, 0)),
                  full((2, 1, SSD_HEADS)), full((2, SSD_HEADS, 1)), full((2, 1, SSD_HEADS)),
                  full((2, SSD_HEADS, 1)), full((2, q, q))],
        out_specs=[pl.BlockSpec((q, SSD_INNER), lambda s: (jnp.where(cf(s) >= nc, 0, cf(s)), 0)),
                   pl.BlockSpec((q, SSD_INNER), lambda s: (jnp.where(cbk(s) >= nc, nc - 1, cbk(s)), 0))],
        out_shape=[jax.ShapeDtypeStruct((nc * q, SSD_INNER), F32), jax.ShapeDtypeStruct((nc * q, SSD_INNER), F32)],
        scratch_shapes=[pltpu.VMEM((2 * SSD_GROUPS, SSD_STATE, SSD_HPG * SSD_HEAD_DIM), F32)],
        compiler_params=_cparams("arbitrary"),
        name="ssd_scan",
    )(xbc_lat, xbc_ctx, xbc_lat, xbc_ctx, dt2, dt2, dt2t, dt2t, bias2, bias2t, a2, a2t, tri2)


def _merge_kernel(o_ref, yf_ref, yb_ref, xs_ref, z_ref, dsk_ref, nw_ref, w_ref, x_ref, g_ref, out_ref, a_ref):
    @pl.when(pl.program_id(1) == 0)
    def _():
        a_ref[:, :MLA_HEADS * MLA_V] = o_ref[...]
        y = yf_ref[...] + yb_ref[...] + dsk_ref[...] * xs_ref[...]
        gy = y * _silu(z_ref[...])
        gw = SSD_INNER // SSD_GROUPS
        for g in range(SSD_GROUPS):
            part = gy[:, g * gw:(g + 1) * gw]
            ms = jnp.mean(part * part, axis=-1, keepdims=True)
            a_ref[:, MLA_HEADS * MLA_V + g * gw:MLA_HEADS * MLA_V + (g + 1) * gw] = (
                part * lax.rsqrt(ms + EPS) * nw_ref[:, g * gw:(g + 1) * gw]).astype(BF16)

    out_ref[...] = x_ref[...] + g_ref[...] * _dot(a_ref[...], w_ref[...])


def merge_out(o_att, yf, yb, xbc, p, dsk, nw, w_o, x, gate, *, tm, tn):
    m, n = x.shape
    kw = w_o.shape[0]
    return pl.pallas_call(
        _merge_kernel,
        grid=(m // tm, n // tn),
        in_specs=[pl.BlockSpec((tm, MLA_HEADS * MLA_V), lambda i, j: (i, 0)),
                  pl.BlockSpec((tm, SSD_INNER), lambda i, j: (i, 0)),
                  pl.BlockSpec((tm, SSD_INNER), lambda i, j: (i, 0)),
                  pl.BlockSpec((tm, SSD_INNER), lambda i, j: (i, 0)),
                  pl.BlockSpec((tm, SSD_INNER), lambda i, j: (i, 0)),
                  pl.BlockSpec((1, SSD_INNER), lambda i, j: (0, 0)),
                  pl.BlockSpec((1, SSD_INNER), lambda i, j: (0, 0)),
                  pl.BlockSpec((kw, tn), lambda i, j: (0, j)),
                  pl.BlockSpec((tm, tn), lambda i, j: (i, j)),
                  pl.BlockSpec((1, tn), lambda i, j: (0, j))],
        out_specs=pl.BlockSpec((tm, tn), lambda i, j: (i, j)),
        out_shape=jax.ShapeDtypeStruct((m, n), F32),
        scratch_shapes=[pltpu.VMEM((tm, kw), BF16)],
        compiler_params=_cparams("parallel", "arbitrary"),
        name="merge_out",
    )(o_att, yf, yb, xbc, p, dsk, nw, w_o, x, gate)


def _mmres_kernel(a_ref, w_ref, b_ref, x_ref, g_ref, o_ref):
    o_ref[...] = x_ref[...] + g_ref[...] * (_dot(a_ref[...].astype(BF16), w_ref[...]) + b_ref[...])


def mm_res(a, w, b, x, gate, *, tm, tn):
    m, k = a.shape
    n = w.shape[1]
    return pl.pallas_call(
        _mmres_kernel,
        grid=(m // tm, n // tn),
        in_specs=[pl.BlockSpec((tm, k), lambda i, j: (i, 0)),
                  pl.BlockSpec((k, tn), lambda i, j: (0, j)),
                  pl.BlockSpec((1, tn), lambda i, j: (0, j)),
                  pl.BlockSpec((tm, tn), lambda i, j: (i, j)),
                  pl.BlockSpec((1, tn), lambda i, j: (0, j))],
        out_specs=pl.BlockSpec((tm, tn), lambda i, j: (i, j)),
        out_shape=jax.ShapeDtypeStruct((m, n), F32),
        compiler_params=_cparams("parallel", "arbitrary"),
        name="mm_res",
    )(a, w, b.reshape(1, n), x, gate)


def _ffn_kernel(x_ref, nw_ref, sh_ref, sc_ref, g_ref, w1_ref, w3_ref, w2_ref, fw_ref, o_ref, xn_ref, acc_ref,
                *, final_norm):
    f = pl.program_id(1)

    @pl.when(f == 0)
    def _():
        xn_ref[...] = _modnorm(x_ref[...], nw_ref[...], sh_ref[...], sc_ref[...]).astype(BF16)
        acc_ref[...] = jnp.zeros(acc_ref.shape, F32)

    xn = xn_ref[...]
    a = _dot(xn, w1_ref[...])
    b = _dot(xn, w3_ref[...])
    acc_ref[...] += _dot((_silu(a) * b).astype(BF16), w2_ref[...])

    @pl.when(f == pl.num_programs(1) - 1)
    def _():
        y = x_ref[...] + g_ref[...] * acc_ref[...]
        if final_norm:
            ms = jnp.mean(y * y, axis=-1, keepdims=True)
            y = y * lax.rsqrt(ms + EPS) * fw_ref[...]
        o_ref[...] = y


def ffn(x, nw, sh, sc, gate, w1, w3, w2, fw, *, layer, tm, tf, final_norm):
    m, dm = x.shape
    dff = w1.shape[2]
    vec = pl.BlockSpec((1, dm), lambda i, f: (0, 0))
    return pl.pallas_call(
        functools.partial(_ffn_kernel, final_norm=final_norm),
        grid=(m // tm, dff // tf),
        in_specs=[pl.BlockSpec((tm, dm), lambda i, f: (i, 0)), vec, vec, vec, vec,
                  pl.BlockSpec((None, dm, tf), lambda i, f: (layer, 0, f)),
                  pl.BlockSpec((None, dm, tf), lambda i, f: (layer, 0, f)),
                  pl.BlockSpec((None, tf, dm), lambda i, f: (layer, f, 0)),
                  vec],
        out_specs=pl.BlockSpec((tm, dm), lambda i, f: (i, 0)),
        out_shape=jax.ShapeDtypeStruct((m, dm), F32),
        scratch_shapes=[pltpu.VMEM((tm, dm), BF16), pltpu.VMEM((tm, dm), F32)],
        compiler_params=_cparams("parallel", "arbitrary"),
        name="ffn",
    )(x, nw.reshape(1, dm), sh, sc, gate, w1, w3, w2, fw.reshape(1, dm))


def _filter_hidden_kernel(fvec_ref, w1_ref, b1_ref, wm_ref, bm_ref, freq_ref, h_ref, *, seq_len, tr):
    i = pl.program_id(0)
    n1_total = 2 * seq_len // FFT_N2
    shift = n1_total.bit_length() - 1
    assert 1 << shift == n1_total
    p = i * tr + lax.broadcasted_iota(jnp.int32, (tr, 1), 0)
    n = ((p & (n1_total - 1)) << (FFT_N2.bit_length() - 1)) + (p >> shift)
    lag = jnp.where(n < seq_len, n, 2 * seq_len - n).astype(F32)
    t = lag / float(seq_len - 1)
    ang = lag * (2.0 * math.pi / seq_len)
    lane = lax.broadcasted_iota(jnp.int32, (tr, 128), 1)
    feat = jnp.sin(ang * fvec_ref[0:1, :] + fvec_ref[1:2, :])
    emb = jnp.where(lane == 0, t, jnp.where(lane < HY_EMB, feat, 0.0))
    fr = freq_ref[...]
    h2 = tr // 2
    e = emb.astype(BF16)
    z = jnp.concatenate([_dot(e[:h2], w1_ref[...]), _dot(e[h2:], w1_ref[...])], axis=1)
    hid = jnp.sin(fr * (z + b1_ref[...]))
    for j in range(wm_ref.shape[0]):
        hid = jnp.sin(fr * (_dot(hid.astype(BF16), wm_ref[j]) + bm_ref[j]))
    h_ref[0:h2, :] = hid[:, :HY_HID]
    h_ref[h2:tr, :] = hid[:, HY_HID:]


def filter_hidden(fvec, w1, b1, wm, bm, freq, *, seq_len, tr):
    full = lambda shp: pl.BlockSpec(shp, lambda i: (0,) * len(shp))
    return pl.pallas_call(
        functools.partial(_filter_hidden_kernel, seq_len=seq_len, tr=tr),
        grid=(2 * seq_len // tr,),
        in_specs=[full((2, 128)), full(w1.shape), full(b1.shape), full(wm.shape), full(bm.shape), full(freq.shape)],
        out_specs=pl.BlockSpec((tr, HY_HID), lambda i: (i, 0)),
        out_shape=jax.ShapeDtypeStruct((2 * seq_len, HY_HID), F32),
        compiler_params=_cparams("parallel"),
        name="hyena_filter_hidden",
    )(fvec, w1, b1, wm, bm, freq)


ROW_GROUP = 16


def _filter_stage1_kernel(f_ref, hid_ref, wo_ref, delta_ref, o_ref, abs_ref, os_ref, *, seq_len):
    i = pl.program_id(1)
    n1 = hid_ref.shape[1]
    half = n1 // 2
    n1_idx = lax.broadcasted_iota(jnp.int32, (n1, 1), 0)
    delta = delta_ref[...]
    f = f_ref[...]
    wo_past, wo_future = wo_ref[0], wo_ref[1]

    @pl.when(i == 0)
    def _():
        abs_ref[...] = jnp.zeros(abs_ref.shape, F32)

    asum = jnp.zeros(abs_ref.shape, F32)
    for r in range(ROW_GROUP):
        hb = hid_ref[r].astype(BF16)
        k = jnp.concatenate([_dot(hb[:half], wo_past), _dot(hb[half:], wo_future)], axis=0)
        n = n1_idx * FFT_N2 + (i * ROW_GROUP + r)
        lag = jnp.where(n < seq_len, n, 2 * seq_len - n).astype(F32)
        k = k * jnp.exp(-(lag / float(seq_len - 1)) * delta)
        k = jnp.where(n == seq_len, 0.0, k)
        asum = asum + jnp.sum(jnp.abs(k), axis=0, keepdims=True)
        os_ref[r] = _dot(f, k.astype(BF16))
    abs_ref[...] += asum
    o_ref[...] = jnp.swapaxes(os_ref[...], 0, 1).astype(o_ref.dtype)


def filter_stage1(f1, hid3, wo, delta, *, seq_len, tn):
    mrows, n1 = f1.shape
    dd = delta.shape[1]
    return pl.pallas_call(
        functools.partial(_filter_stage1_kernel, seq_len=seq_len),
        grid=(dd // tn, FFT_N2 // ROW_GROUP),
        in_specs=[pl.BlockSpec((mrows, n1), lambda j, i: (0, 0)),
                  pl.BlockSpec((ROW_GROUP, n1, HY_HID), lambda j, i: (i, 0, 0)),
                  pl.BlockSpec((2, HY_HID, tn), lambda j, i: (0, 0, j)),
                  pl.BlockSpec((1, tn), lambda j, i: (0, j))],
        out_specs=[pl.BlockSpec((mrows, ROW_GROUP, tn), lambda j, i: (0, i, j)),
                   pl.BlockSpec((1, tn), lambda j, i: (0, j))],
        out_shape=[jax.ShapeDtypeStruct((mrows, FFT_N2, dd), BF16), jax.ShapeDtypeStruct((1, dd), F32)],
        scratch_shapes=[pltpu.VMEM((ROW_GROUP, mrows, tn), F32)],
        compiler_params=_cparams("parallel", "arbitrary"),
        name="filter_stage1",
    )(f1, hid3, wo, delta)


def _fft1_kernel(f_ref, x_ref, o_ref, os_ref):
    xt = jnp.swapaxes(x_ref[...], 0, 1)
    for r in range(ROW_GROUP):
        os_ref[r] = _dot(f_ref[...], xt[r].astype(BF16))
    o_ref[...] = jnp.swapaxes(os_ref[...], 0, 1).astype(o_ref.dtype)


def _fft1_inv_kernel(g_ref, b_ref, v_ref, gate_ref, fb_ref, o_ref, os_ref):
    bt = jnp.swapaxes(b_ref[...].astype(F32), 0, 1)
    for r in range(ROW_GROUP):
        os_ref[r] = _dot(g_ref[...], bt[r].astype(BF16))
    conv = jnp.swapaxes(os_ref[...], 0, 1)
    o_ref[...] = gate_ref[...] * (conv + fb_ref[...] * v_ref[...])


def fft_stage1(f1, x3, *, d, tn, col_ofs=0):
    mrows, k = f1.shape
    n2 = x3.shape[1]
    nd = d // tn
    return pl.pallas_call(
        _fft1_kernel,
        grid=(n2 // ROW_GROUP, nd),
        in_specs=[pl.BlockSpec((mrows, k), lambda i, j: (0, 0)),
                  pl.BlockSpec((k, ROW_GROUP, tn), lambda i, j: (0, i, col_ofs * nd + j))],
        out_specs=pl.BlockSpec((mrows, ROW_GROUP, tn), lambda i, j: (0, i, j)),
        out_shape=jax.ShapeDtypeStruct((mrows, n2, d), BF16),
        scratch_shapes=[pltpu.VMEM((ROW_GROUP, mrows, tn), F32)],
        compiler_params=_cparams("parallel", "arbitrary"),
        name="fft_stage1",
    )(f1, x3)


def fft_stage1_inv(g1, b3, v3, v_ofs, gate3, gate_ofs, fbias, *, d, tn):
    mrows, k = g1.shape
    n2 = b3.shape[1]
    nd = d // tn
    return pl.pallas_call(
        _fft1_inv_kernel,
        grid=(n2 // ROW_GROUP, nd),
        in_specs=[pl.BlockSpec((mrows, k), lambda i, j: (0, 0)),
                  pl.BlockSpec((k, ROW_GROUP, tn), lambda i, j: (0, i, j)),
                  pl.BlockSpec((mrows, ROW_GROUP, tn), lambda i, j: (0, i, v_ofs * nd + j)),
                  pl.BlockSpec((mrows, ROW_GROUP, tn), lambda i, j: (0, i, gate_ofs * nd + j)),
                  pl.BlockSpec((1, tn), lambda i, j: (0, j))],
        out_specs=pl.BlockSpec((mrows, ROW_GROUP, tn), lambda i, j: (0, i, j)),
        out_shape=jax.ShapeDtypeStruct((mrows, n2, d), F32),
        scratch_shapes=[pltpu.VMEM((ROW_GROUP, mrows, tn), F32)],
        compiler_params=_cparams("parallel", "arbitrary"),
        name="fft_stage1_inv",
    )(g1, b3, v3, gate3, fbias)


def _fftmid_kernel(a_ref, ak_ref, mf_ref, mi_ref, scale_ref, o_ref):
    half = FFT_N2
    scale = scale_ref[...]
    for s in range(SLAB_TILE):
        mf = mf_ref[s]
        y = _dot(mf, a_ref[s])
        kf = _dot(mf, ak_ref[s])
        yr, yi = y[:half], y[half:]
        kr, ki = kf[:half], kf[half:]
        pr = (yr * kr - yi * ki) * scale
        pi = (yr * ki + yi * kr) * scale
        o_ref[s] = _dot(mi_ref[s], jnp.concatenate([pr, pi], axis=0).astype(BF16)).astype(o_ref.dtype)


def fft_mid(a3, ak3, ak_ofs, mf, mi, scale, *, dc):
    nsp, rows, d = a3.shape
    nd = d // dc
    blk = pl.BlockSpec((SLAB_TILE, rows, dc), lambda s, j: (s, 0, j))
    mat = pl.BlockSpec((SLAB_TILE, rows, rows), lambda s, j: (s, 0, 0))
    return pl.pallas_call(
        _fftmid_kernel,
        grid=(nsp // SLAB_TILE, nd),
        in_specs=[blk, pl.BlockSpec((SLAB_TILE, rows, dc), lambda s, j: (s, 0, ak_ofs * nd + j)), mat, mat,
                  pl.BlockSpec((1, dc), lambda s, j: (0, j))],
        out_specs=blk,
        out_shape=jax.ShapeDtypeStruct((nsp, rows, d), BF16),
        compiler_params=_cparams("parallel", "arbitrary"),
        name="fft_mid",
    )(a3, ak3, mf, mi, scale)


def _fft_tables(seq_len):
    n = 2 * seq_len
    n2 = FFT_N2
    n1 = n // n2
    ns = n1 // 2 + 1
    nsp = -(-ns // SLAB_TILE) * SLAB_TILE
    k1 = jnp.arange(nsp, dtype=jnp.int32)
    valid = (k1 < ns)
    m1 = jnp.arange(n1, dtype=jnp.int32)
    ang1 = (2.0 * math.pi / n1) * lax.rem(k1[:, None] * m1[None, :], n1).astype(F32)
    vf = valid[:, None].astype(F32)
    f1 = jnp.stack([jnp.cos(ang1) * vf, -jnp.sin(ang1) * vf], axis=1).reshape(2 * nsp, n1)
    wgt = jnp.where((k1 == 0) | (k1 == n1 // 2), 1.0, 2.0) * valid.astype(F32) / n
    g1 = jnp.stack([jnp.cos(ang1) * wgt[:, None], -jnp.sin(ang1) * wgt[:, None]], axis=1)
    g1 = g1.reshape(2 * nsp, n1).T[: n1 // 2]
    k2 = jnp.arange(n2, dtype=jnp.int32)
    m2 = jnp.arange(n2, dtype=jnp.int32)
    f = k1[:, None, None] + n1 * k2[None, :, None]
    ang2 = (2.0 * math.pi / n) * lax.rem(f * m2[None, None, :], n).astype(F32)
    vm = valid[:, None, None].astype(F32)
    c2, s2 = jnp.cos(ang2) * vm, jnp.sin(ang2) * vm
    mf = jnp.concatenate([jnp.concatenate([c2, s2], axis=2), jnp.concatenate([-s2, c2], axis=2)], axis=1)
    c2t, s2t = jnp.swapaxes(c2, 1, 2), jnp.swapaxes(s2, 1, 2)
    mi = jnp.concatenate([jnp.concatenate([c2t, -s2t], axis=2), jnp.concatenate([s2t, c2t], axis=2)], axis=1)
    return dict(n1=n1, nsp=nsp, f1=f1.astype(BF16), g1=g1.astype(BF16), mf=mf.astype(BF16), mi=mi.astype(BF16))


def long_conv_gate(v, v_cols, gate, gate_cols, fbias, ak3, ak_cols, scale, tabs, *, seq_len, d):
    n1, nsp = tabs["n1"], tabs["nsp"]
    half = n1 // 2
    v3 = v.reshape(half, FFT_N2, v.shape[1])
    g3 = gate.reshape(half, FFT_N2, gate.shape[1])
    a = fft_stage1(tabs["f1"][:, :half], v3, d=d, tn=256, col_ofs=v_cols)
    b3 = fft_mid(a.reshape(nsp, 2 * FFT_N2, d), ak3, ak_cols, tabs["mf"], tabs["mi"], scale, dc=512)
    out = fft_stage1_inv(tabs["g1"], b3.reshape(2 * nsp, FFT_N2, d), v3, v_cols, g3, gate_cols, fbias, d=d, tn=256)
    return out.reshape(seq_len, d)


def _rope_tables(seq_len, scale):
    n_freq = MLA_ROPE // 4
    rows = seq_len // GRID_W
    row = jnp.repeat(jnp.arange(rows, dtype=F32), GRID_W)
    col = jnp.tile(jnp.arange(GRID_W, dtype=F32), rows)
    inv = ROPE_THETA ** (-jnp.arange(n_freq, dtype=F32) / n_freq)
    ang = jnp.stack([row[:, None] * inv, col[:, None] * inv], axis=1)
    cos = jnp.broadcast_to(jnp.cos(ang)[:, :, None, :], (seq_len, 2, 2, n_freq)).reshape(seq_len, MLA_ROPE)
    sin = jnp.broadcast_to(jnp.sin(ang)[:, :, None, :], (seq_len, 2, 2, n_freq)).reshape(seq_len, MLA_ROPE)
    one = jnp.ones((seq_len, MLA_NOPE), F32)
    zero = jnp.zeros((seq_len, MLA_NOPE), F32)
    z64 = jnp.zeros((seq_len, MLA_HEAD_PAD - MLA_NOPE - MLA_ROPE), F32)
    ta = jnp.concatenate([one, cos, z64], axis=1) * scale
    tb = jnp.concatenate([zero, sin, z64], axis=1) * scale
    tc = jnp.concatenate([cos, z64], axis=1)
    ts = jnp.concatenate([sin, z64], axis=1)
    return ta, tb, tc, ts


def _rope_swap_cols(w):
    n_freq = MLA_ROPE // 4
    w4 = w.reshape(w.shape[0], 2, 2, n_freq)
    return jnp.stack([-w4[:, :, 1], w4[:, :, 0]], axis=2).reshape(w.shape[0], MLA_ROPE)


def _even_layer(x, ctx, mods, norm_mix_w, w_in, conv_w, conv_b, dt_bias, a_log, d_skip, ssd_norm_w,
                q_norm_w, w_uq, kv_norm_w, w_ukv, w_o):
    seq_len, d = x.shape
    sh1, sc1, g1 = (mods[0:1, i * d:(i + 1) * d] for i in range(3))
    csh1, csc1 = mods[1:2, 0:d], mods[1:2, d:2 * d]

    o1 = Q_SIDE + SSD_XBC
    o2 = o1 + SSD_DT
    o3 = o2 + MLA_KV_RANK
    w_kr = w_in[:, o3:]
    w_ext = jnp.concatenate([w_in[:, :Q_SIDE], w_in[:, Q_SIDE:o1], w_in[:, o2:o3], w_kr, _rope_swap_cols(w_kr),
                             w_in[:, o1:o2], jnp.zeros((d, P_COLS - P_DT - SSD_DT), F32)], axis=1).astype(BF16)
    zb = jnp.zeros((P_COLS,), F32)
    p_lat = normmm(x, norm_mix_w, sh1, sc1, w_ext, zb, tm=_pick(seq_len, (1024, 512)), tn=768, out_dtype=F32)
    p_ctx = normmm(ctx, norm_mix_w, csh1, csc1, w_ext, zb, tm=CTX_LEN, tn=768, out_dtype=F32)

    scale = float(MLA_NOPE + MLA_ROPE) ** -0.5 * math.log2(math.e)
    ta, tb, tc, ts = _rope_tables(seq_len, scale)
    wq = w_uq.reshape(MLA_Q_RANK, MLA_HEADS, MLA_NOPE + MLA_ROPE)
    zpad = jnp.zeros((MLA_Q_RANK, MLA_HEADS, MLA_HEAD_PAD - MLA_NOPE - MLA_ROPE), F32)
    wa = jnp.concatenate([wq, zpad], axis=2).reshape(MLA_Q_RANK, -1).astype(BF16)
    wr = wq[:, :, MLA_NOPE:]
    n_freq = MLA_ROPE // 4
    wr4 = wr.reshape(MLA_Q_RANK, MLA_HEADS, 2, 2, n_freq)
    wsw = jnp.stack([-wr4[:, :, :, 1], wr4[:, :, :, 0]], axis=3).reshape(MLA_Q_RANK, MLA_HEADS, MLA_ROPE)
    wb = jnp.concatenate([jnp.zeros((MLA_Q_RANK, MLA_HEADS, MLA_NOPE), F32), wsw, zpad], axis=2)
    wb = wb.reshape(MLA_Q_RANK, -1).astype(BF16)
    q = qproj(p_lat, q_norm_w, wa, wb, ta, tb, tm=512)

    zk = jnp.zeros((MLA_KV_RANK,), F32)
    w_ukv_b = w_ukv.astype(BF16)
    zkb = jnp.zeros((w_ukv.shape[1],), F32)
    kv_lat = normmm(p_lat, kv_norm_w, zk, zk, w_ukv_b, zkb, tm=512, tn=w_ukv.shape[1], out_dtype=BF16,
                    x_col=P_CKV // MLA_KV_RANK)
    kv_ctx = normmm(p_ctx, kv_norm_w, zk, zk, w_ukv_b, zkb, tm=CTX_LEN, tn=w_ukv.shape[1], out_dtype=BF16,
                    x_col=P_CKV // MLA_KV_RANK)
    k_lat, v_lat = kv_assemble(kv_lat, p_lat, tc, ts, tm=512, rope=True)
    k_ctx, v_ctx = kv_assemble(kv_ctx, p_ctx, tc, ts, tm=CTX_LEN, rope=False)
    k_all = jnp.concatenate([k_lat, k_ctx], axis=0)
    v_all = jnp.concatenate([v_lat, v_ctx], axis=0)
    s_tot = seq_len + CTX_LEN
    o_att = attention(q, k_all, v_all, tq=_pick(seq_len, (2048, 1024, 512, 256)),
                      tk=_pick(s_tot, (1280, 1024, 768, 640, 512, 384, 256, 128)))

    xbc_lat = dwconv3(p_lat, conv_w, conv_b, tm=512, tc=SSD_XBC, col0=P_XBC, width=SSD_XBC, act=True)
    xbc_ctx = dwconv3(p_ctx, conv_w, conv_b, tm=CTX_LEN, tc=SSD_XBC, col0=P_XBC, width=SSD_XBC, act=True)
    dt_all = jnp.concatenate([p_lat[:, P_DT:P_DT + SSD_DT], p_ctx[:, P_DT:P_DT + SSD_DT]], axis=0)
    nch = s_tot // SSD_CHUNK
    dt2 = dt_all.reshape(s_tot, 2, SSD_HEADS).transpose(1, 0, 2)
    dt2t = dt2.reshape(2, nch, SSD_CHUNK, SSD_HEADS).transpose(0, 1, 3, 2)
    bias2 = dt_bias.reshape(2, 1, SSD_HEADS)
    bias2t = dt_bias.reshape(2, SSD_HEADS, 1)
    a_neg = -jnp.exp(a_log.astype(F32))
    a2 = a_neg.reshape(2, 1, SSD_HEADS)
    a2t = a_neg.reshape(2, SSD_HEADS, 1)
    lower = jnp.tril(jnp.ones((SSD_CHUNK, SSD_CHUNK), F32))
    tri2 = jnp.stack([lower, lower.T]).astype(BF16)
    yf, yb = ssd_scan(xbc_lat, xbc_ctx, dt2, dt2t, bias2, bias2t, a2, a2t, tri2, n_lat_chunks=seq_len // SSD_CHUNK)

    dsk = jnp.repeat(d_skip[0] + d_skip[1], SSD_HEAD_DIM).reshape(1, SSD_INNER)
    return merge_out(o_att, yf, yb, xbc_lat, p_lat, dsk, ssd_norm_w.reshape(1, SSD_INNER), w_o.astype(BF16), x, g1,
                     tm=256, tn=d)


def _odd_layer(x, mods, norm_mix_w, w_in, b_in, short_w, short_b, fw1, fb1, fw_mid, fb_mid, freq, fw_out,
               fbias, w_out, b_out):
    seq_len, d = x.shape
    sh1, sc1, g1 = (mods[0:1, i * d:(i + 1) * d] for i in range(3))
    pc = normmm_conv(x, norm_mix_w, sh1, sc1, w_in.astype(BF16), b_in, short_w, short_b,
                     tm=_pick(seq_len, (1024, 512)), tn=768)

    tabs = _fft_tables(seq_len)
    bands = (HY_EMB - 1) // 2
    fband = jnp.linspace(1e-4, bands - 1, bands, dtype=F32)
    zpad = jnp.zeros((128 - HY_EMB,), F32)
    fvec = jnp.stack([jnp.concatenate([jnp.zeros((1,), F32), fband, fband, zpad]),
                      jnp.concatenate([jnp.zeros((1,), F32), jnp.full((bands,), 0.5 * math.pi, F32),
                                       jnp.full((bands,), math.pi, F32), zpad])])
    w1p = jnp.concatenate([fw1.astype(F32), jnp.zeros((128 - HY_EMB, HY_HID), F32)], axis=0).astype(BF16)
    lo = math.log(HY_SLOW_DECAY) / HY_TARGET
    hi = math.log(HY_FAST_DECAY) / HY_TARGET
    delta = jnp.abs(jnp.linspace(lo, hi, d, dtype=F32)).reshape(1, d)
    n_ord = fw_out.shape[1]
    wo = jnp.transpose(fw_out, (2, 0, 1, 3)).reshape(2, HY_HID, n_ord * d).astype(BF16)
    two = lambda a: jnp.concatenate([a, a], axis=-1)
    zmid = jnp.zeros_like(fw_mid)
    wm_bd = jnp.concatenate([jnp.concatenate([fw_mid, zmid], axis=2), jnp.concatenate([zmid, fw_mid], axis=2)],
                            axis=1).astype(BF16)
    hid = filter_hidden(fvec, w1p, two(fb1.reshape(1, HY_HID)), wm_bd, two(fb_mid.reshape(-1, 1, HY_HID)),
                        two(freq.reshape(1, HY_HID)), seq_len=seq_len, tr=1024)
    ak, kabs = filter_stage1(tabs["f1"], hid.reshape(FFT_N2, tabs["n1"], HY_HID), wo, jnp.tile(delta, (1, n_ord)),
                             seq_len=seq_len, tn=256)
    ak3 = ak.reshape(tabs["nsp"], 2 * FFT_N2, n_ord * d)
    y_cols = 2
    y = None
    for i in range(n_ord):
        scale = 1.0 / kabs[:, i * d:(i + 1) * d]
        fb = fbias[i].reshape(1, d)
        if y is None:
            y = long_conv_gate(pc, y_cols, pc, i, fb, ak3, i, scale, tabs, seq_len=seq_len, d=d)
        else:
            y = long_conv_gate(y, 0, pc, i, fb, ak3, i, scale, tabs, seq_len=seq_len, d=d)
    y = y.reshape(seq_len, d)
    return mm_res(y, w_out.astype(BF16), b_out, x, g1, tm=512, tn=d)


def kernel(x, c, ctx, c_ctx, mod_w, mod_b, norm_mix_w, norm_ffn_w, ffn_w1, ffn_w3, ffn_w2, ev_w_in, ev_conv_w, ev_conv_b, ev_dt_bias, ev_a_log, ev_d_skip, ev_ssd_norm_w, ev_q_norm_w, ev_w_uq, ev_kv_norm_w, ev_w_ukv, ev_w_o, hy_w_in, hy_b_in, hy_short_w, hy_short_b, hy_fw1, hy_fb1, hy_fw_mid, hy_fb_mid, hy_freq, hy_fw_out, hy_fbias, hy_w_out, hy_b_out, final_norm_w):
    assert x.shape[0] == 1 and mod_w.shape[0] == 2
    xs = x[0]
    xc = ctx[0]
    d = xs.shape[1]
    vecs = jnp.concatenate([c.reshape(1, d), c_ctx.reshape(1, d), jnp.zeros((6, d), F32)], axis=0)
    depth = mod_w.shape[0]
    w1_b, w3_b, w2_b = ffn_w1.astype(BF16), ffn_w3.astype(BF16), ffn_w2.astype(BF16)
    for i in range(depth):
        mods = adaln_vectors(vecs, mod_w, mod_b, i)
        sh2, sc2, g2 = (mods[0:1, j * d:(j + 1) * d] for j in range(3, 6))
        if i % 2 == 0:
            e = i // 2
            xs = _even_layer(xs, xc, mods, norm_mix_w[i], ev_w_in[e], ev_conv_w[e], ev_conv_b[e], ev_dt_bias[e],
                             ev_a_log[e], ev_d_skip[e], ev_ssd_norm_w[e], ev_q_norm_w[e], ev_w_uq[e],
                             ev_kv_norm_w[e], ev_w_ukv[e], ev_w_o[e])
        else:
            o = i // 2
            xs = _odd_layer(xs, mods, norm_mix_w[i], hy_w_in[o], hy_b_in[o], hy_short_w[o], hy_short_b[o],
                            hy_fw1[o], hy_fb1[o], hy_fw_mid[o], hy_fb_mid[o], hy_freq[o], hy_fw_out[o],
                            hy_fbias[o], hy_w_out[o], hy_b_out[o])
        xs = ffn(xs, norm_ffn_w[i], sh2, sc2, g2, w1_b, w3_b, w2_b, final_norm_w, layer=i, tm=512, tf=512,
                 final_norm=(i == depth - 1))
    return xs[None]
```

```python
import functools
import math

import jax
import jax.numpy as jnp
from jax import lax
from jax.experimental import pallas as pl
from jax.experimental.pallas import tpu as pltpu

F32 = jnp.float32
BF16 = jnp.bfloat16

EPS = 1e-6
GRID_W = 64
CTX_LEN = 256
SSD_HEADS = 16
SSD_HEAD_DIM = 64
SSD_INNER = 1024
SSD_GROUPS = 2
SSD_HPG = 8
SSD_STATE = 128
SSD_CHUNK = 256
SSD_XBC = 1536
SSD_DT = 32
MLA_HEADS = 8
MLA_NOPE = 128
MLA_ROPE = 64
MLA_V = 128
MLA_Q_RANK = 512
MLA_KV_RANK = 512
ROPE_THETA = 10000.0
MLA_HEAD_PAD = 256
Q_SIDE = SSD_INNER + MLA_Q_RANK
HY_EMB = 33
HY_HID = 64
HY_FAST_DECAY = 0.3
HY_SLOW_DECAY = 1.5
HY_TARGET = 1e-2
FFT_N2 = 128
SLAB_TILE = 8

P_Z, P_CQ, P_XBC, P_CKV, P_KR, P_DT, P_COLS = 0, 1024, 1536, 3072, 3584, 3712, 3840

VMEM_LIMIT = 56 * 1024 * 1024


def _cparams(*sem):
    return pltpu.CompilerParams(dimension_semantics=sem, vmem_limit_bytes=VMEM_LIMIT)


def _pick(n, cands):
    for c in cands:
        if n % c == 0:
            return c
    raise ValueError(f"no tile for {n}")


def _split(a):
    hi = a.astype(BF16)
    lo = (a - hi.astype(F32)).astype(BF16)
    return hi, lo


def _dot(a, b):
    return jnp.dot(a, b, preferred_element_type=F32)


def _silu(x):
    return x * (1.0 / (1.0 + jnp.exp(-x)))


def _modnorm(x, nw, sh, sc):
    ms = jnp.mean(x * x, axis=-1, keepdims=True)
    return (x * lax.rsqrt(ms + EPS) * nw) * (1.0 + sc) + sh


def _matvec_kernel(x_ref, w_ref, b_ref, o_ref):
    x = x_ref[...]
    o_ref[...] = _dot(_silu(x).astype(BF16), w_ref[...].astype(BF16)) + b_ref[...]


def adaln_vectors(vecs, w_all, b_all, layer):
    nl, k, n = w_all.shape
    tn = 1024
    return pl.pallas_call(
        _matvec_kernel,
        grid=(n // tn,),
        in_specs=[pl.BlockSpec((8, k), lambda j: (0, 0)),
                  pl.BlockSpec((None, k, tn), lambda j: (layer, 0, j)),
                  pl.BlockSpec((None, 1, tn), lambda j: (layer, 0, j))],
        out_specs=pl.BlockSpec((8, tn), lambda j: (0, j)),
        out_shape=jax.ShapeDtypeStruct((8, n), F32),
        compiler_params=_cparams("arbitrary"),
        name="adaln_vectors",
    )(vecs, w_all, b_all.reshape(nl, 1, n))


def _normmm_kernel(x_ref, nw_ref, sh_ref, sc_ref, w_ref, b_ref, o_ref, xn_ref):
    @pl.when(pl.program_id(1) == 0)
    def _():
        xn_ref[...] = _modnorm(x_ref[...].astype(F32), nw_ref[...], sh_ref[...], sc_ref[...]).astype(BF16)

    o_ref[...] = (_dot(xn_ref[...], w_ref[...]) + b_ref[...]).astype(o_ref.dtype)


def normmm(x, nw, sh, sc, w, b, *, tm, tn, out_dtype, x_col=0):
    m = x.shape[0]
    k, n = w.shape
    return pl.pallas_call(
        _normmm_kernel,
        grid=(m // tm, n // tn),
        in_specs=[pl.BlockSpec((tm, k), lambda i, j: (i, x_col)),
                  pl.BlockSpec((1, k), lambda i, j: (0, 0)),
                  pl.BlockSpec((1, k), lambda i, j: (0, 0)),
                  pl.BlockSpec((1, k), lambda i, j: (0, 0)),
                  pl.BlockSpec((k, tn), lambda i, j: (0, j)),
                  pl.BlockSpec((1, tn), lambda i, j: (0, j))],
        out_specs=pl.BlockSpec((tm, tn), lambda i, j: (i, j)),
        out_shape=jax.ShapeDtypeStruct((m, n), out_dtype),
        scratch_shapes=[pltpu.VMEM((tm, k), BF16)],
        compiler_params=_cparams("parallel", "arbitrary"),
        name="normmm",
    )(x, nw.reshape(1, k), sh.reshape(1, k), sc.reshape(1, k), w, b.reshape(1, n))


def _normmm_conv_kernel(x_ref, xprev_ref, xnext_ref, nw_ref, sh_ref, sc_ref, w_ref, b_ref, cw_ref, cb_ref,
                        o_ref, xn_ref, xh_ref):
    i = pl.program_id(0)

    @pl.when(pl.program_id(1) == 0)
    def _():
        nw, sh, sc = nw_ref[...], sh_ref[...], sc_ref[...]
        xn_ref[...] = _modnorm(x_ref[...], nw, sh, sc).astype(BF16)
        xh_ref[0:8, :] = _modnorm(xprev_ref[...], nw, sh, sc).astype(BF16)
        xh_ref[8:16, :] = _modnorm(xnext_ref[...], nw, sh, sc).astype(BF16)

    w = w_ref[...]
    b = b_ref[...]
    y = _dot(xn_ref[...], w) + b
    yh = _dot(xh_ref[...], w) + b
    prev_row = jnp.where(i == 0, 0.0, yh[7:8, :])
    next_row = jnp.where(i == pl.num_programs(0) - 1, 0.0, yh[8:9, :])
    yp, yn = _shift_rows(y, prev_row, next_row)
    o_ref[...] = yp * cw_ref[0:1, :] + y * cw_ref[1:2, :] + yn * cw_ref[2:3, :] + cb_ref[...]


def normmm_conv(x, nw, sh, sc, w, b, cw, cb, *, tm, tn):
    m, k = x.shape
    n = w.shape[1]
    nrb = m // 8
    tb = tm // 8
    vec = pl.BlockSpec((1, k), lambda i, j: (0, 0))
    return pl.pallas_call(
        _normmm_conv_kernel,
        grid=(m // tm, n // tn),
        in_specs=[pl.BlockSpec((tm, k), lambda i, j: (i, 0)),
                  pl.BlockSpec((8, k), lambda i, j: (jnp.maximum(i * tb - 1, 0), 0)),
                  pl.BlockSpec((8, k), lambda i, j: (jnp.minimum((i + 1) * tb, nrb - 1), 0)),
                  vec, vec, vec,
                  pl.BlockSpec((k, tn), lambda i, j: (0, j)),
                  pl.BlockSpec((1, tn), lambda i, j: (0, j)),
                  pl.BlockSpec((3, tn), lambda i, j: (0, j)),
                  pl.BlockSpec((1, tn), lambda i, j: (0, j))],
        out_specs=pl.BlockSpec((tm, tn), lambda i, j: (i, j)),
        out_shape=jax.ShapeDtypeStruct((m, n), F32),
        scratch_shapes=[pltpu.VMEM((tm, k), BF16), pltpu.VMEM((16, k), BF16)],
        compiler_params=_cparams("parallel", "arbitrary"),
        name="normmm_conv",
    )(x, x, x, nw.reshape(1, k), sh.reshape(1, k), sc.reshape(1, k), w, b.reshape(1, n), cw, cb.reshape(1, n))


def _qproj_kernel(x_ref, nw_ref, wa_ref, wb_ref, ta_ref, tb_ref, o_ref):
    x = x_ref[...]
    ms = jnp.mean(x * x, axis=-1, keepdims=True)
    xn = (x * lax.rsqrt(ms + EPS) * nw_ref[...]).astype(BF16)
    ta = ta_ref[...]
    tb = tb_ref[...]
    hp = MLA_HEAD_PAD
    for h in range(MLA_HEADS):
        cols = slice(h * hp, (h + 1) * hp)
        o_ref[:, cols] = (_dot(xn, wa_ref[:, cols]) * ta + _dot(xn, wb_ref[:, cols]) * tb).astype(o_ref.dtype)


def qproj(p, nw, wa, wb, ta, tb, *, tm):
    m = ta.shape[0]
    k = MLA_Q_RANK
    hp = MLA_HEAD_PAD
    n = MLA_HEADS * hp
    return pl.pallas_call(
        _qproj_kernel,
        grid=(m // tm,),
        in_specs=[pl.BlockSpec((tm, k), lambda i: (i, P_CQ // MLA_Q_RANK)),
                  pl.BlockSpec((1, k), lambda i: (0, 0)),
                  pl.BlockSpec((k, n), lambda i: (0, 0)),
                  pl.BlockSpec((k, n), lambda i: (0, 0)),
                  pl.BlockSpec((tm, hp), lambda i: (i, 0)),
                  pl.BlockSpec((tm, hp), lambda i: (i, 0))],
        out_specs=pl.BlockSpec((tm, n), lambda i: (i, 0)),
        out_shape=jax.ShapeDtypeStruct((m, n), BF16),
        compiler_params=_cparams("parallel"),
        name="mla_qproj",
    )(p, nw.reshape(1, k), wa, wb, ta, tb)


def _kv_kernel(cl_ref, cc_ref, rl_ref, rc_ref, tc_ref, ts_ref, nw_ref, w_ref, k_ref, v_ref, *, n_lat_tiles):
    is_ctx = pl.program_id(0) >= n_lat_tiles
    x = jnp.where(is_ctx, cc_ref[...], cl_ref[...])
    ms = jnp.mean(x * x, axis=-1, keepdims=True)
    kv = _dot((x * lax.rsqrt(ms + EPS) * nw_ref[...]).astype(BF16), w_ref[...]).astype(BF16)
    krr = jnp.where(is_ctx, rc_ref[...], rl_ref[...])
    kr = krr[:, :MLA_ROPE]
    kr_rot = kr * tc_ref[...][:, :MLA_ROPE] + krr[:, MLA_ROPE:] * ts_ref[...][:, :MLA_ROPE]
    kr = jnp.where(is_ctx, kr, kr_rot)
    tail = jnp.concatenate([kr, jnp.zeros_like(kr)], axis=-1).astype(BF16)
    for h in range(MLA_HEADS):
        base = h * (MLA_NOPE + MLA_V)
        k_ref[:, h * MLA_HEAD_PAD:h * MLA_HEAD_PAD + MLA_NOPE] = kv[:, base:base + MLA_NOPE]
        k_ref[:, h * MLA_HEAD_PAD + MLA_NOPE:(h + 1) * MLA_HEAD_PAD] = tail
        v_ref[:, h * MLA_V:(h + 1) * MLA_V] = kv[:, base + MLA_NOPE:base + MLA_NOPE + MLA_V]


def kv_project(p_lat, p_ctx, tc, ts, nw, w):
    tm = CTX_LEN
    nl = p_lat.shape[0] // tm
    rows = p_lat.shape[0] + p_ctx.shape[0]
    k = MLA_KV_RANK
    lat = lambda i: jnp.minimum(i, nl - 1)
    return pl.pallas_call(
        functools.partial(_kv_kernel, n_lat_tiles=nl),
        grid=(rows // tm,),
        in_specs=[pl.BlockSpec((tm, k), lambda i: (lat(i), P_CKV // MLA_KV_RANK)),
                  pl.BlockSpec((tm, k), lambda i: (0, P_CKV // MLA_KV_RANK)),
                  pl.BlockSpec((tm, 128), lambda i: (lat(i), P_KR // 128)),
                  pl.BlockSpec((tm, 128), lambda i: (0, P_KR // 128)),
                  pl.BlockSpec((tm, 128), lambda i: (lat(i), 0)),
                  pl.BlockSpec((tm, 128), lambda i: (lat(i), 0)),
                  pl.BlockSpec((1, k), lambda i: (0, 0)),
                  pl.BlockSpec(w.shape, lambda i: (0, 0))],
        out_specs=[pl.BlockSpec((tm, MLA_HEADS * MLA_HEAD_PAD), lambda i: (i, 0)),
                   pl.BlockSpec((tm, MLA_HEADS * MLA_V), lambda i: (i, 0))],
        out_shape=[jax.ShapeDtypeStruct((rows, MLA_HEADS * MLA_HEAD_PAD), BF16),
                   jax.ShapeDtypeStruct((rows, MLA_HEADS * MLA_V), BF16)],
        compiler_params=_cparams("parallel"),
        name="mla_kv_project",
    )(p_lat, p_ctx, p_lat, p_ctx, tc, ts, nw.reshape(1, k), w)


def _attn_kernel(q_ref, k_ref, v_ref, o_ref, s0_ref, s1_ref, m_ref, acc_ref, *, nk):
    j = pl.program_id(2)

    def qk(s_ref):
        s_ref[...] = lax.dot_general(q_ref[...], k_ref[...], (((1,), (1,)), ((), ())), preferred_element_type=F32)

    def softmax_pv(s_ref):
        s = s_ref[...]
        m_prev = m_ref[...]
        m_new = jnp.maximum(m_prev, jnp.max(s, axis=-1, keepdims=True))
        alpha = jnp.exp2(m_prev - m_new)
        p = jnp.exp2(s - m_new).astype(BF16)
        v = v_ref[...]
        lane = lax.broadcasted_iota(jnp.int32, v.shape, 1)
        ones_col = jnp.where(lane == 0, 1.0, 0.0).astype(BF16)
        v_ext = jnp.concatenate([v, ones_col], axis=1)
        acc_ref[...] = alpha * acc_ref[...] + _dot(p, v_ext)
        m_ref[...] = m_new

    @pl.when(j == 0)
    def _():
        m_ref[...] = jnp.full(m_ref.shape, -jnp.inf, F32)
        acc_ref[...] = jnp.zeros(acc_ref.shape, F32)
        qk(s0_ref)

    mid = jnp.logical_and(j > 0, j < nk)

    @pl.when(jnp.logical_and(mid, j % 2 == 1))
    def _():
        softmax_pv(s0_ref)
        qk(s1_ref)

    @pl.when(jnp.logical_and(mid, j % 2 == 0))
    def _():
        softmax_pv(s1_ref)
        qk(s0_ref)

    @pl.when(j == nk)
    def _():
        softmax_pv(s1_ref if (nk - 1) % 2 else s0_ref)
        acc = acc_ref[...]
        o_ref[...] = (acc[:, :MLA_V] / acc[:, MLA_V:MLA_V + 1]).astype(o_ref.dtype)


def attention(q, k, v, *, tq, tk):
    lq = q.shape[0]
    s = k.shape[0]
    nk = s // tk
    return pl.pallas_call(
        functools.partial(_attn_kernel, nk=nk),
        grid=(MLA_HEADS, lq // tq, nk + 1),
        in_specs=[pl.BlockSpec((tq, MLA_HEAD_PAD), lambda h, i, j: (i, h)),
                  pl.BlockSpec((tk, MLA_HEAD_PAD), lambda h, i, j: (jnp.minimum(j, nk - 1), h)),
                  pl.BlockSpec((tk, MLA_V), lambda h, i, j: (jnp.maximum(j - 1, 0), h))],
        out_specs=pl.BlockSpec((tq, MLA_V), lambda h, i, j: (i, h)),
        out_shape=jax.ShapeDtypeStruct((lq, MLA_HEADS * MLA_V), BF16),
        scratch_shapes=[pltpu.VMEM((tq, tk), F32), pltpu.VMEM((tq, tk), F32), pltpu.VMEM((tq, 1), F32),
                        pltpu.VMEM((tq, 2 * MLA_V), F32)],
        compiler_params=_cparams("parallel", "parallel", "arbitrary"),
        name="mla_attention",
    )(q, k, v)


def _shift_rows(x, prev_row, next_row):
    tm = x.shape[0]
    rows = lax.broadcasted_iota(jnp.int32, x.shape, 0)
    xp = jnp.where(rows == 0, prev_row, pltpu.roll(x, 1, 0))
    xn = jnp.where(rows == tm - 1, next_row, pltpu.roll(x, tm - 1, 0))
    return xp, xn


def _dwconv_kernel(x_ref, xprev_ref, xnext_ref, w_ref, b_ref, o_ref, *, act):
    i = pl.program_id(0)
    x = x_ref[...]
    prev_row = jnp.where(i == 0, 0.0, xprev_ref[7:8, :])
    next_row = jnp.where(i == pl.num_programs(0) - 1, 0.0, xnext_ref[0:1, :])
    xp, xn = _shift_rows(x, prev_row, next_row)
    y = xp * w_ref[0:1, :] + x * w_ref[1:2, :] + xn * w_ref[2:3, :] + b_ref[...]
    if act:
        y = _silu(y)
    o_ref[...] = y


def dwconv3(x, w, b, *, tm, tc, col0, width, act):
    m = x.shape[0]
    cb = col0 // tc
    nrb = m // 8
    tb = tm // 8
    return pl.pallas_call(
        functools.partial(_dwconv_kernel, act=act),
        grid=(m // tm, width // tc),
        in_specs=[pl.BlockSpec((tm, tc), lambda i, j: (i, cb + j)),
                  pl.BlockSpec((8, tc), lambda i, j: (jnp.maximum(i * tb - 1, 0), cb + j)),
                  pl.BlockSpec((8, tc), lambda i, j: (jnp.minimum((i + 1) * tb, nrb - 1), cb + j)),
                  pl.BlockSpec((3, tc), lambda i, j: (0, j)),
                  pl.BlockSpec((1, tc), lambda i, j: (0, j))],
        out_specs=pl.BlockSpec((tm, tc), lambda i, j: (i, j)),
        out_shape=jax.ShapeDtypeStruct((m, width), F32),
        compiler_params=_cparams("arbitrary", "arbitrary"),
        name="dwconv3",
    )(x, x, x, w, b.reshape(1, width))


def _softplus(x):
    return jnp.maximum(x, 0.0) + jnp.log(1.0 + jnp.exp(-jnp.abs(x)))


def _ssd_prepare(d, xbc, dt_ref, dtt_ref, bias_ref, biast_ref, a_ref, at_ref, tri_ref):
    q = SSD_CHUNK
    dt = _softplus(dt_ref[0] + bias_ref[d])
    dtt = _softplus(dtt_ref[0, 0] + biast_ref[d])
    dta_hi, dta_lo = _split(dt * a_ref[d])
    dtat_hi, dtat_lo = _split(dtt * at_ref[d])
    tri = tri_ref[d]
    trit = tri_ref[1 - d]
    acum = _dot(tri, dta_hi) + _dot(tri, dta_lo)
    acumt = _dot(dtat_hi, trit) + _dot(dtat_lo, trit)
    total = acum[q - 1:q, :] if d == 0 else acum[0:1, :]
    return dict(xbc=xbc, dt=dt, acum=acum, acumt=acumt, to_end=jnp.exp(total - acum),
                from_start=jnp.exp(acum), chunk_decay=jnp.exp(total), mask=tri > 0.5)


def _per_head_lanes(a, g):
    return jnp.concatenate([jnp.broadcast_to(a[:, h:h + 1], (a.shape[0], SSD_HEAD_DIM))
                            for h in range(g * SSD_HPG, (g + 1) * SSD_HPG)], axis=1)


def _ssd_group(d, p, g, h_ref):
    xbc = p["xbc"]
    gw = SSD_HPG * SSD_HEAD_DIM
    bm = xbc[:, SSD_INNER + g * SSD_STATE:SSD_INNER + (g + 1) * SSD_STATE]
    cm = xbc[:, SSD_INNER + (SSD_GROUPS + g) * SSD_STATE:SSD_INNER + (SSD_GROUPS + g + 1) * SSD_STATE]
    bm_b = bm.astype(BF16)
    cm_b = cm.astype(BF16)
    cb = lax.dot_general(cm_b, bm_b, (((1,), (1,)), ((), ())), preferred_element_type=F32)
    xdt = xbc[:, g * gw:(g + 1) * gw] * _per_head_lanes(p["dt"], g)
    h_prev = h_ref[d * SSD_GROUPS + g]
    y_off = _dot(cm_b, h_prev.astype(BF16)) * _per_head_lanes(p["from_start"], g)
    states = _dot(bm.T.astype(BF16), (xdt * _per_head_lanes(p["to_end"], g)).astype(BF16))
    h_ref[d * SSD_GROUPS + g] = h_prev * _per_head_lanes(p["chunk_decay"], g) + states
    return dict(cb=cb, xdt_b=xdt.astype(BF16), y_off=y_off)


def _ssd_head(p, grp, g, r, o_ref):
    h = g * SSD_HPG + r
    seg = p["acum"][:, h:h + 1] - p["acumt"][h:h + 1, :]
    decay = jnp.exp(jnp.where(p["mask"], seg, -jnp.inf))
    lanes = slice(r * SSD_HEAD_DIM, (r + 1) * SSD_HEAD_DIM)
    y_diag = _dot((grp["cb"] * decay).astype(BF16), grp["xdt_b"][:, lanes])
    o_ref[:, h * SSD_HEAD_DIM:(h + 1) * SSD_HEAD_DIM] = y_diag + grp["y_off"][:, lanes]


def _ssd_kernel(xf_ref, xfc_ref, xb_ref, xbc_ref, dtf_ref, dtb_ref, dttf_ref, dttb_ref, bias_ref, biast_ref, a_ref, at_ref, tri_ref,
                of_ref, ob_ref, h_ref):
    @pl.when(pl.program_id(0) == 0)
    def _():
        h_ref[...] = jnp.zeros(h_ref.shape, F32)

    s = pl.program_id(0)
    ncx = CTX_LEN // SSD_CHUNK
    x_fwd = jnp.where(s < ncx, xfc_ref[...], xf_ref[...])
    x_bwd = jnp.where(s < ncx, xbc_ref[...], xb_ref[...])
    prm = (_ssd_prepare(0, x_fwd, dtf_ref, dttf_ref, bias_ref, biast_ref, a_ref, at_ref, tri_ref),
           _ssd_prepare(1, x_bwd, dtb_ref, dttb_ref, bias_ref, biast_ref, a_ref, at_ref, tri_ref))
    outs = (of_ref, ob_ref)
    for g in range(SSD_GROUPS):
        grp = (_ssd_group(0, prm[0], g, h_ref), _ssd_group(1, prm[1], g, h_ref))
        for r in range(SSD_HPG):
            for d in range(2):
                _ssd_head(prm[d], grp[d], g, r, outs[d])


def ssd_scan(xbc_lat, xbc_ctx, dt2, dt2t, bias2, bias2t, a2, a2t, tri2, *, n_lat_chunks):
    nc = n_lat_chunks
    ncx = CTX_LEN // SSD_CHUNK
    tot = nc + ncx
    q = SSD_CHUNK

    def cf(s):
        return lax.rem(s + nc, tot)

    def cbk(s):
        return tot - 1 - s

    def full(shape):
        return pl.BlockSpec(shape, lambda s: (0,) * len(shape))

    return pl.pallas_call(
        _ssd_kernel,
        grid=(tot,),
        in_specs=[pl.BlockSpec((q, SSD_XBC), lambda s: (jnp.minimum(cf(s), nc - 1), 0)),
                  pl.BlockSpec((q, SSD_XBC), lambda s: (jnp.maximum(cf(s) - nc, 0), 0)),
                  pl.BlockSpec((q, SSD_XBC), lambda s: (jnp.minimum(cbk(s), nc - 1), 0)),
                  pl.BlockSpec((q, SSD_XBC), lambda s: (jnp.maximum(cbk(s) - nc, 0), 0)),
                  pl.BlockSpec((1, q, SSD_HEADS), lambda s: (0, cf(s), 0)),
                  pl.BlockSpec((1, q, SSD_HEADS), lambda s: (1, cbk(s), 0)),
                  pl.BlockSpec((1, 1, SSD_HEADS, q), lambda s: (0, cf(s), 0, 0)),
                  pl.BlockSpec((1, 1, SSD_HEADS, q), lambda s: (1, cbk(s), 0, 0)),
                  full((2, 1, SSD_HEADS)), full((2, SSD_HEADS, 1)), full((2, 1, SSD_HEADS)),
                  full((2, SSD_HEADS, 1)), full((2, q, q))],
        out_specs=[pl.BlockSpec((q, SSD_INNER), lambda s: (jnp.where(cf(s) >= nc, 0, cf(s)), 0)),
                   pl.BlockSpec((q, SSD_INNER), lambda s: (jnp.where(cbk(s) >= nc, nc - 1, cbk(s)), 0))],
        out_shape=[jax.ShapeDtypeStruct((nc * q, SSD_INNER), F32), jax.ShapeDtypeStruct((nc * q, SSD_INNER), F32)],
        scratch_shapes=[pltpu.VMEM((2 * SSD_GROUPS, SSD_STATE, SSD_HPG * SSD_HEAD_DIM), F32)],
        compiler_params=_cparams("arbitrary"),
        name="ssd_scan",
    )(xbc_lat, xbc_ctx, xbc_lat, xbc_ctx, dt2, dt2, dt2t, dt2t, bias2, bias2t, a2, a2t, tri2)


def _merge_kernel(o_ref, yf_ref, yb_ref, xs_ref, z_ref, dsk_ref, nw_ref, w_ref, x_ref, g_ref, out_ref, a_ref):
    @pl.when(pl.program_id(1) == 0)
    def _():
        a_ref[:, :MLA_HEADS * MLA_V] = o_ref[...]
        y = yf_ref[...] + yb_ref[...] + dsk_ref[...] * xs_ref[...]
        gy = y * _silu(z_ref[...])
        gw = SSD_INNER // SSD_GROUPS
        for g in range(SSD_GROUPS):
            part = gy[:, g * gw:(g + 1) * gw]
            ms = jnp.mean(part * part, axis=-1, keepdims=True)
            a_ref[:, MLA_HEADS * MLA_V + g * gw:MLA_HEADS * MLA_V + (g + 1) * gw] = (
                part * lax.rsqrt(ms + EPS) * nw_ref[:, g * gw:(g + 1) * gw]).astype(BF16)

    out_ref[...] = x_ref[...] + g_ref[...] * _dot(a_ref[...], w_ref[...])


def merge_out(o_att, yf, yb, xbc, p, dsk, nw, w_o, x, gate, *, tm, tn):
    m, n = x.shape
    kw = w_o.shape[0]
    return pl.pallas_call(
        _merge_kernel,
        grid=(m // tm, n // tn),
        in_specs=[pl.BlockSpec((tm, MLA_HEADS * MLA_V), lambda i, j: (i, 0)),
                  pl.BlockSpec((tm, SSD_INNER), lambda i, j: (i, 0)),
                  pl.BlockSpec((tm, SSD_INNER), lambda i, j: (i, 0)),
                  pl.BlockSpec((tm, SSD_INNER), lambda i, j: (i, 0)),
                  pl.BlockSpec((tm, SSD_INNER), lambda i, j: (i, 0)),
                  pl.BlockSpec((1, SSD_INNER), lambda i, j: (0, 0)),
                  pl.BlockSpec((1, SSD_INNER), lambda i, j: (0, 0)),
                  pl.BlockSpec((kw, tn), lambda i, j: (0, j)),
                  pl.BlockSpec((tm, tn), lambda i, j: (i, j)),
                  pl.BlockSpec((1, tn), lambda i, j: (0, j))],
        out_specs=pl.BlockSpec((tm, tn), lambda i, j: (i, j)),
        out_shape=jax.ShapeDtypeStruct((m, n), F32),
        scratch_shapes=[pltpu.VMEM((tm, kw), BF16)],
        compiler_params=_cparams("parallel", "arbitrary"),
        name="merge_out",
    )(o_att, yf, yb, xbc, p, dsk, nw, w_o, x, gate)


def _mmres_kernel(a_ref, w_ref, b_ref, x_ref, g_ref, o_ref):
    o_ref[...] = x_ref[...] + g_ref[...] * (_dot(a_ref[...].astype(BF16), w_ref[...]) + b_ref[...])


def mm_res(a, w, b, x, gate, *, tm, tn):
    m, k = a.shape
    n = w.shape[1]
    return pl.pallas_call(
        _mmres_kernel,
        grid=(m // tm, n // tn),
        in_specs=[pl.BlockSpec((tm, k), lambda i, j: (i, 0)),
                  pl.BlockSpec((k, tn), lambda i, j: (0, j)),
                  pl.BlockSpec((1, tn), lambda i, j: (0, j)),
                  pl.BlockSpec((tm, tn), lambda i, j: (i, j)),
                  pl.BlockSpec((1, tn), lambda i, j: (0, j))],
        out_specs=pl.BlockSpec((tm, tn), lambda i, j: (i, j)),
        out_shape=jax.ShapeDtypeStruct((m, n), F32),
        compiler_params=_cparams("parallel", "arbitrary"),
        name="mm_res",
    )(a, w, b.reshape(1, n), x, gate)


def _ffn_kernel(x_ref, nw_ref, sh_ref, sc_ref, g_ref, w1_ref, w3_ref, w2_ref, fw_ref, o_ref, xn_ref, acc_ref,
                *, final_norm):
    f = pl.program_id(1)

    @pl.when(f == 0)
    def _():
        xn_ref[...] = _modnorm(x_ref[...], nw_ref[...], sh_ref[...], sc_ref[...]).astype(BF16)
        acc_ref[...] = jnp.zeros(acc_ref.shape, F32)

    xn = xn_ref[...]
    a = _dot(xn, w1_ref[...])
    b = _dot(xn, w3_ref[...])
    acc_ref[...] += _dot((_silu(a) * b).astype(BF16), w2_ref[...])

    @pl.when(f == pl.num_programs(1) - 1)
    def _():
        y = x_ref[...] + g_ref[...] * acc_ref[...]
        if final_norm:
            ms = jnp.mean(y * y, axis=-1, keepdims=True)
            y = y * lax.rsqrt(ms + EPS) * fw_ref[...]
        o_ref[...] = y


def ffn(x, nw, sh, sc, gate, w1, w3, w2, fw, *, layer, tm, tf, final_norm):
    m, dm = x.shape
    dff = w1.shape[2]
    vec = pl.BlockSpec((1, dm), lambda i, f: (0, 0))
    return pl.pallas_call(
        functools.partial(_ffn_kernel, final_norm=final_norm),
        grid=(m // tm, dff // tf),
        in_specs=[pl.BlockSpec((tm, dm), lambda i, f: (i, 0)), vec, vec, vec, vec,
                  pl.BlockSpec((None, dm, tf), lambda i, f: (layer, 0, f)),
                  pl.BlockSpec((None, dm, tf), lambda i, f: (layer, 0, f)),
                  pl.BlockSpec((None, tf, dm), lambda i, f: (layer, f, 0)),
                  vec],
        out_specs=pl.BlockSpec((tm, dm), lambda i, f: (i, 0)),
        out_shape=jax.ShapeDtypeStruct((m, dm), F32),
        scratch_shapes=[pltpu.VMEM((tm, dm), BF16), pltpu.VMEM((tm, dm), F32)],
        compiler_params=_cparams("parallel", "arbitrary"),
        name="ffn",
    )(x, nw.reshape(1, dm), sh, sc, gate, w1, w3, w2, fw.reshape(1, dm))


def _filter_hidden_kernel(fvec_ref, w1_ref, b1_ref, wm_ref, bm_ref, freq_ref, h_ref, *, seq_len, tr):
    i = pl.program_id(0)
    n1_total = 2 * seq_len // FFT_N2
    shift = n1_total.bit_length() - 1
    assert 1 << shift == n1_total
    p = i * tr + lax.broadcasted_iota(jnp.int32, (tr, 1), 0)
    n = ((p & (n1_total - 1)) << (FFT_N2.bit_length() - 1)) + (p >> shift)
    lag = jnp.where(n < seq_len, n, 2 * seq_len - n).astype(F32)
    t = lag / float(seq_len - 1)
    ang = lag * (2.0 * math.pi / seq_len)
    lane = lax.broadcasted_iota(jnp.int32, (tr, 128), 1)
    feat = jnp.sin(ang * fvec_ref[0:1, :] + fvec_ref[1:2, :])
    emb = jnp.where(lane == 0, t, jnp.where(lane < HY_EMB, feat, 0.0))
    fr = freq_ref[...]
    h2 = tr // 2
    e = emb.astype(BF16)
    z = jnp.concatenate([_dot(e[:h2], w1_ref[...]), _dot(e[h2:], w1_ref[...])], axis=1)
    hid = jnp.sin(fr * (z + b1_ref[...]))
    for j in range(wm_ref.shape[0]):
        hid = jnp.sin(fr * (_dot(hid.astype(BF16), wm_ref[j]) + bm_ref[j]))
    h_ref[0:h2, :] = hid[:, :HY_HID]
    h_ref[h2:tr, :] = hid[:, HY_HID:]


def filter_hidden(fvec, w1, b1, wm, bm, freq, *, seq_len, tr):
    full = lambda shp: pl.BlockSpec(shp, lambda i: (0,) * len(shp))
    return pl.pallas_call(
        functools.partial(_filter_hidden_kernel, seq_len=seq_len, tr=tr),
        grid=(2 * seq_len // tr,),
        in_specs=[full((2, 128)), full(w1.shape), full(b1.shape), full(wm.shape), full(bm.shape), full(freq.shape)],
        out_specs=pl.BlockSpec((tr, HY_HID), lambda i: (i, 0)),
        out_shape=jax.ShapeDtypeStruct((2 * seq_len, HY_HID), F32),
        compiler_params=_cparams("parallel"),
        name="hyena_filter_hidden",
    )(fvec, w1, b1, wm, bm, freq)


ROW_GROUP = 16


def _filter_stage1_kernel(f_ref, hid_ref, wo_ref, delta_ref, o_ref, abs_ref, os_ref, *, seq_len):
    i = pl.program_id(1)
    n1 = hid_ref.shape[1]
    half = n1 // 2
    n1_idx = lax.broadcasted_iota(jnp.int32, (n1, 1), 0)
    delta = delta_ref[...]
    f = f_ref[...]
    wo_past, wo_future = wo_ref[0], wo_ref[1]

    @pl.when(i == 0)
    def _():
        abs_ref[...] = jnp.zeros(abs_ref.shape, F32)

    asum = jnp.zeros(abs_ref.shape, F32)
    for r in range(ROW_GROUP):
        hb = hid_ref[r].astype(BF16)
        k = jnp.concatenate([_dot(hb[:half], wo_past), _dot(hb[half:], wo_future)], axis=0)
        n = n1_idx * FFT_N2 + (i * ROW_GROUP + r)
        lag = jnp.where(n < seq_len, n, 2 * seq_len - n).astype(F32)
        k = k * jnp.exp(-(lag / float(seq_len - 1)) * delta)
        k = jnp.where(n == seq_len, 0.0, k)
        asum = asum + jnp.sum(jnp.abs(k), axis=0, keepdims=True)
        os_ref[r] = _dot(f, k.astype(BF16))
    abs_ref[...] += asum
    o_ref[...] = jnp.swapaxes(os_ref[...], 0, 1).astype(o_ref.dtype)


def filter_stage1(f1, hid3, wo, delta, *, seq_len, tn):
    mrows, n1 = f1.shape
    dd = delta.shape[1]
    return pl.pallas_call(
        functools.partial(_filter_stage1_kernel, seq_len=seq_len),
        grid=(dd // tn, FFT_N2 // ROW_GROUP),
        in_specs=[pl.BlockSpec((mrows, n1), lambda j, i: (0, 0)),
                  pl.BlockSpec((ROW_GROUP, n1, HY_HID), lambda j, i: (i, 0, 0)),
                  pl.BlockSpec((2, HY_HID, tn), lambda j, i: (0, 0, j)),
                  pl.BlockSpec((1, tn), lambda j, i: (0, j))],
        out_specs=[pl.BlockSpec((mrows, ROW_GROUP, tn), lambda j, i: (0, i, j)),
                   pl.BlockSpec((1, tn), lambda j, i: (0, j))],
        out_shape=[jax.ShapeDtypeStruct((mrows, FFT_N2, dd), BF16), jax.ShapeDtypeStruct((1, dd), F32)],
        scratch_shapes=[pltpu.VMEM((ROW_GROUP, mrows, tn), F32)],
        compiler_params=_cparams("parallel", "arbitrary"),
        name="filter_stage1",
    )(f1, hid3, wo, delta)


def _fft1_kernel(f_ref, x_ref, o_ref, os_ref):
    xt = jnp.swapaxes(x_ref[...], 0, 1)
    for r in range(ROW_GROUP):
        os_ref[r] = _dot(f_ref[...], xt[r].astype(BF16))
    o_ref[...] = jnp.swapaxes(os_ref[...], 0, 1).astype(o_ref.dtype)


def _fft1_inv_kernel(g_ref, b_ref, v_ref, gate_ref, fb_ref, o_ref, os_ref):
    bt = jnp.swapaxes(b_ref[...].astype(F32), 0, 1)
    for r in range(ROW_GROUP):
        os_ref[r] = _dot(g_ref[...], bt[r].astype(BF16))
    conv = jnp.swapaxes(os_ref[...], 0, 1)
    o_ref[...] = gate_ref[...] * (conv + fb_ref[...] * v_ref[...])


def fft_stage1(f1, x3, *, d, tn, col_ofs=0):
    mrows, k = f1.shape
    n2 = x3.shape[1]
    nd = d // tn
    return pl.pallas_call(
        _fft1_kernel,
        grid=(n2 // ROW_GROUP, nd),
        in_specs=[pl.BlockSpec((mrows, k), lambda i, j: (0, 0)),
                  pl.BlockSpec((k, ROW_GROUP, tn), lambda i, j: (0, i, col_ofs * nd + j))],
        out_specs=pl.BlockSpec((mrows, ROW_GROUP, tn), lambda i, j: (0, i, j)),
        out_shape=jax.ShapeDtypeStruct((mrows, n2, d), BF16),
        scratch_shapes=[pltpu.VMEM((ROW_GROUP, mrows, tn), F32)],
        compiler_params=_cparams("parallel", "arbitrary"),
        name="fft_stage1",
    )(f1, x3)


def fft_stage1_inv(g1, b3, v3, v_ofs, gate3, gate_ofs, fbias, *, d, tn):
    mrows, k = g1.shape
    n2 = b3.shape[1]
    nd = d // tn
    return pl.pallas_call(
        _fft1_inv_kernel,
        grid=(n2 // ROW_GROUP, nd),
        in_specs=[pl.BlockSpec((mrows, k), lambda i, j: (0, 0)),
                  pl.BlockSpec((k, ROW_GROUP, tn), lambda i, j: (0, i, j)),
                  pl.BlockSpec((mrows, ROW_GROUP, tn), lambda i, j: (0, i, v_ofs * nd + j)),
                  pl.BlockSpec((mrows, ROW_GROUP, tn), lambda i, j: (0, i, gate_ofs * nd + j)),
                  pl.BlockSpec((1, tn), lambda i, j: (0, j))],
        out_specs=pl.BlockSpec((mrows, ROW_GROUP, tn), lambda i, j: (0, i, j)),
        out_shape=jax.ShapeDtypeStruct((mrows, n2, d), F32),
        scratch_shapes=[pltpu.VMEM((ROW_GROUP, mrows, tn), F32)],
        compiler_params=_cparams("parallel", "arbitrary"),
        name="fft_stage1_inv",
    )(g1, b3, v3, gate3, fbias)


def _fftmid_kernel(a_ref, ak_ref, mf_ref, mi_ref, scale_ref, o_ref):
    half = FFT_N2
    scale = scale_ref[...]
    for s in range(SLAB_TILE):
        mf = mf_ref[s]
        y = _dot(mf, a_ref[s])
        kf = _dot(mf, ak_ref[s])
        yr, yi = y[:half], y[half:]
        kr, ki = kf[:half], kf[half:]
        pr = (yr * kr - yi * ki) * scale
        pi = (yr * ki + yi * kr) * scale
        o_ref[s] = _dot(mi_ref[s], jnp.concatenate([pr, pi], axis=0).astype(BF16)).astype(o_ref.dtype)


def fft_mid(a3, ak3, ak_ofs, mf, mi, scale, *, dc):
    nsp, rows, d = a3.shape
    nd = d // dc
    blk = pl.BlockSpec((SLAB_TILE, rows, dc), lambda s, j: (s, 0, j))
    mat = pl.BlockSpec((SLAB_TILE, rows, rows), lambda s, j: (s, 0, 0))
    return pl.pallas_call(
        _fftmid_kernel,
        grid=(nsp // SLAB_TILE, nd),
        in_specs=[blk, pl.BlockSpec((SLAB_TILE, rows, dc), lambda s, j: (s, 0, ak_ofs * nd + j)), mat, mat,
                  pl.BlockSpec((1, dc), lambda s, j: (0, j))],
        out_specs=blk,
        out_shape=jax.ShapeDtypeStruct((nsp, rows, d), BF16),
        compiler_params=_cparams("parallel", "arbitrary"),
        name="fft_mid",
    )(a3, ak3, mf, mi, scale)


def _fft_tables(seq_len):
    n = 2 * seq_len
    n2 = FFT_N2
    n1 = n // n2
    ns = n1 // 2 + 1
    nsp = -(-ns // SLAB_TILE) * SLAB_TILE
    k1 = jnp.arange(nsp, dtype=jnp.int32)
    valid = (k1 < ns)
    m1 = jnp.arange(n1, dtype=jnp.int32)
    ang1 = (2.0 * math.pi / n1) * lax.rem(k1[:, None] * m1[None, :], n1).astype(F32)
    vf = valid[:, None].astype(F32)
    f1 = jnp.stack([jnp.cos(ang1) * vf, -jnp.sin(ang1) * vf], axis=1).reshape(2 * nsp, n1)
    wgt = jnp.where((k1 == 0) | (k1 == n1 // 2), 1.0, 2.0) * valid.astype(F32) / n
    g1 = jnp.stack([jnp.cos(ang1) * wgt[:, None], -jnp.sin(ang1) * wgt[:, None]], axis=1)
    g1 = g1.reshape(2 * nsp, n1).T[: n1 // 2]
    k2 = jnp.arange(n2, dtype=jnp.int32)
    m2 = jnp.arange(n2, dtype=jnp.int32)
    f = k1[:, None, None] + n1 * k2[None, :, None]
    ang2 = (2.0 * math.pi / n) * lax.rem(f * m2[None, None, :], n).astype(F32)
    vm = valid[:, None, None].astype(F32)
    c2, s2 = jnp.cos(ang2) * vm, jnp.sin(ang2) * vm
    mf = jnp.concatenate([jnp.concatenate([c2, s2], axis=2), jnp.concatenate([-s2, c2], axis=2)], axis=1)
    c2t, s2t = jnp.swapaxes(c2, 1, 2), jnp.swapaxes(s2, 1, 2)
    mi = jnp.concatenate([jnp.concatenate([c2t, -s2t], axis=2), jnp.concatenate([s2t, c2t], axis=2)], axis=1)
    return dict(n1=n1, nsp=nsp, f1=f1.astype(BF16), g1=g1.astype(BF16), mf=mf.astype(BF16), mi=mi.astype(BF16))


def long_conv_gate(v, v_cols, gate, gate_cols, fbias, ak3, ak_cols, scale, tabs, *, seq_len, d):
    n1, nsp = tabs["n1"], tabs["nsp"]
    half = n1 // 2
    v3 = v.reshape(half, FFT_N2, v.shape[1])
    g3 = gate.reshape(half, FFT_N2, gate.shape[1])
    a = fft_stage1(tabs["f1"][:, :half], v3, d=d, tn=256, col_ofs=v_cols)
    b3 = fft_mid(a.reshape(nsp, 2 * FFT_N2, d), ak3, ak_cols, tabs["mf"], tabs["mi"], scale, dc=512)
    out = fft_stage1_inv(tabs["g1"], b3.reshape(2 * nsp, FFT_N2, d), v3, v_cols, g3, gate_cols, fbias, d=d, tn=256)
    return out.reshape(seq_len, d)


def _rope_tables(seq_len, scale):
    n_freq = MLA_ROPE // 4
    rows = seq_len // GRID_W
    row = jnp.repeat(jnp.arange(rows, dtype=F32), GRID_W)
    col = jnp.tile(jnp.arange(GRID_W, dtype=F32), rows)
    inv = ROPE_THETA ** (-jnp.arange(n_freq, dtype=F32) / n_freq)
    ang = jnp.stack([row[:, None] * inv, col[:, None] * inv], axis=1)
    cos = jnp.broadcast_to(jnp.cos(ang)[:, :, None, :], (seq_len, 2, 2, n_freq)).reshape(seq_len, MLA_ROPE)
    sin = jnp.broadcast_to(jnp.sin(ang)[:, :, None, :], (seq_len, 2, 2, n_freq)).reshape(seq_len, MLA_ROPE)
    one = jnp.ones((seq_len, MLA_NOPE), F32)
    zero = jnp.zeros((seq_len, MLA_NOPE), F32)
    z64 = jnp.zeros((seq_len, MLA_HEAD_PAD - MLA_NOPE - MLA_ROPE), F32)
    ta = jnp.concatenate([one, cos, z64], axis=1) * scale
    tb = jnp.concatenate([zero, sin, z64], axis=1) * scale
    tc = jnp.concatenate([cos, z64], axis=1)
    ts = jnp.concatenate([sin, z64], axis=1)
    return ta, tb, tc, ts


def _rope_swap_cols(w):
    n_freq = MLA_ROPE // 4
    w4 = w.reshape(w.shape[0], 2, 2, n_freq)
    return jnp.stack([-w4[:, :, 1], w4[:, :, 0]], axis=2).reshape(w.shape[0], MLA_ROPE)


def _even_layer(x, ctx, mods, norm_mix_w, w_in, conv_w, conv_b, dt_bias, a_log, d_skip, ssd_norm_w,
                q_norm_w, w_uq, kv_norm_w, w_ukv, w_o):
    seq_len, d = x.shape
    sh1, sc1, g1 = (mods[0:1, i * d:(i + 1) * d] for i in range(3))
    csh1, csc1 = mods[1:2, 0:d], mods[1:2, d:2 * d]

    o1 = Q_SIDE + SSD_XBC
    o2 = o1 + SSD_DT
    o3 = o2 + MLA_KV_RANK
    w_kr = w_in[:, o3:]
    w_ext = jnp.concatenate([w_in[:, :Q_SIDE], w_in[:, Q_SIDE:o1], w_in[:, o2:o3], w_kr, _rope_swap_cols(w_kr),
                             w_in[:, o1:o2], jnp.zeros((d, P_COLS - P_DT - SSD_DT), F32)], axis=1).astype(BF16)
    zb = jnp.zeros((P_COLS,), F32)
    p_lat = normmm(x, norm_mix_w, sh1, sc1, w_ext, zb, tm=_pick(seq_len, (1024, 512)), tn=768, out_dtype=F32)
    p_ctx = normmm(ctx, norm_mix_w, csh1, csc1, w_ext, zb, tm=CTX_LEN, tn=768, out_dtype=F32)

    scale = float(MLA_NOPE + MLA_ROPE) ** -0.5 * math.log2(math.e)
    ta, tb, tc, ts = _rope_tables(seq_len, scale)
    wq = w_uq.reshape(MLA_Q_RANK, MLA_HEADS, MLA_NOPE + MLA_ROPE)
    zpad = jnp.zeros((MLA_Q_RANK, MLA_HEADS, MLA_HEAD_PAD - MLA_NOPE - MLA_ROPE), F32)
    wa = jnp.concatenate([wq, zpad], axis=2).reshape(MLA_Q_RANK, -1).astype(BF16)
    wr = wq[:, :, MLA_NOPE:]
    n_freq = MLA_ROPE // 4
    wr4 = wr.reshape(MLA_Q_RANK, MLA_HEADS, 2, 2, n_freq)
    wsw = jnp.stack([-wr4[:, :, :, 1], wr4[:, :, :, 0]], axis=3).reshape(MLA_Q_RANK, MLA_HEADS, MLA_ROPE)
    wb = jnp.concatenate([jnp.zeros((MLA_Q_RANK, MLA_HEADS, MLA_NOPE), F32), wsw, zpad], axis=2)
    wb = wb.reshape(MLA_Q_RANK, -1).astype(BF16)
    q = qproj(p_lat, q_norm_w, wa, wb, ta, tb, tm=512)

    k_all, v_all = kv_project(p_lat, p_ctx, tc, ts, kv_norm_w, w_ukv.astype(BF16))
    s_tot = seq_len + CTX_LEN
    o_att = attention(q, k_all, v_all, tq=_pick(seq_len, (2048, 1024, 512, 256)),
                      tk=_pick(s_tot, (1280, 1024, 768, 640, 512, 384, 256, 128)))

    xbc_lat = dwconv3(p_lat, conv_w, conv_b, tm=512, tc=SSD_XBC, col0=P_XBC, width=SSD_XBC, act=True)
    xbc_ctx = dwconv3(p_ctx, conv_w, conv_b, tm=CTX_LEN, tc=SSD_XBC, col0=P_XBC, width=SSD_XBC, act=True)
    dt_all = jnp.concatenate([p_lat[:, P_DT:P_DT + SSD_DT], p_ctx[:, P_DT:P_DT + SSD_DT]], axis=0)
    nch = s_tot // SSD_CHUNK
    dt2 = dt_all.reshape(s_tot, 2, SSD_HEADS).transpose(1, 0, 2)
    dt2t = dt2.reshape(2, nch, SSD_CHUNK, SSD_HEADS).transpose(0, 1, 3, 2)
    bias2 = dt_bias.reshape(2, 1, SSD_HEADS)
    bias2t = dt_bias.reshape(2, SSD_HEADS, 1)
    a_neg = -jnp.exp(a_log.astype(F32))
    a2 = a_neg.reshape(2, 1, SSD_HEADS)
    a2t = a_neg.reshape(2, SSD_HEADS, 1)
    lower = jnp.tril(jnp.ones((SSD_CHUNK, SSD_CHUNK), F32))
    tri2 = jnp.stack([lower, lower.T]).astype(BF16)
    yf, yb = ssd_scan(xbc_lat, xbc_ctx, dt2, dt2t, bias2, bias2t, a2, a2t, tri2, n_lat_chunks=seq_len // SSD_CHUNK)

    dsk = jnp.repeat(d_skip[0] + d_skip[1], SSD_HEAD_DIM).reshape(1, SSD_INNER)
    return merge_out(o_att, yf, yb, xbc_lat, p_lat, dsk, ssd_norm_w.reshape(1, SSD_INNER), w_o.astype(BF16), x, g1,
                     tm=256, tn=d)


def _odd_layer(x, mods, norm_mix_w, w_in, b_in, short_w, short_b, fw1, fb1, fw_mid, fb_mid, freq, fw_out,
               fbias, w_out, b_out):
    seq_len, d = x.shape
    sh1, sc1, g1 = (mods[0:1, i * d:(i + 1) * d] for i in range(3))
    pc = normmm_conv(x, norm_mix_w, sh1, sc1, w_in.astype(BF16), b_in, short_w, short_b,
                     tm=_pick(seq_len, (1024, 512)), tn=768)

    tabs = _fft_tables(seq_len)
    bands = (HY_EMB - 1) // 2
    fband = jnp.linspace(1e-4, bands - 1, bands, dtype=F32)
    zpad = jnp.zeros((128 - HY_EMB,), F32)
    fvec = jnp.stack([jnp.concatenate([jnp.zeros((1,), F32), fband, fband, zpad]),
                      jnp.concatenate([jnp.zeros((1,), F32), jnp.full((bands,), 0.5 * math.pi, F32),
                                       jnp.full((bands,), math.pi, F32), zpad])])
    w1p = jnp.concatenate([fw1.astype(F32), jnp.zeros((128 - HY_EMB, HY_HID), F32)], axis=0).astype(BF16)
    lo = math.log(HY_SLOW_DECAY) / HY_TARGET
    hi = math.log(HY_FAST_DECAY) / HY_TARGET
    delta = jnp.abs(jnp.linspace(lo, hi, d, dtype=F32)).reshape(1, d)
    n_ord = fw_out.shape[1]
    wo = jnp.transpose(fw_out, (2, 0, 1, 3)).reshape(2, HY_HID, n_ord * d).astype(BF16)
    two = lambda a: jnp.concatenate([a, a], axis=-1)
    zmid = jnp.zeros_like(fw_mid)
    wm_bd = jnp.concatenate([jnp.concatenate([fw_mid, zmid], axis=2), jnp.concatenate([zmid, fw_mid], axis=2)],
                            axis=1).astype(BF16)
    hid = filter_hidden(fvec, w1p, two(fb1.reshape(1, HY_HID)), wm_bd, two(fb_mid.reshape(-1, 1, HY_HID)),
                        two(freq.reshape(1, HY_HID)), seq_len=seq_len, tr=1024)
    ak, kabs = filter_stage1(tabs["f1"], hid.reshape(FFT_N2, tabs["n1"], HY_HID), wo, jnp.tile(delta, (1, n_ord)),
                             seq_len=seq_len, tn=256)
    ak3 = ak.reshape(tabs["nsp"], 2 * FFT_N2, n_ord * d)
    y_cols = 2
    y = None
    for i in range(n_ord):
        scale = 1.0 / kabs[:, i * d:(i + 1) * d]
        fb = fbias[i].reshape(1, d)
        if y is None:
            y = long_conv_gate(pc, y_cols, pc, i, fb, ak3, i, scale, tabs, seq_len=seq_len, d=d)
        else:
            y = long_conv_gate(y, 0, pc, i, fb, ak3, i, scale, tabs, seq_len=seq_len, d=d)
    y = y.reshape(seq_len, d)
    return mm_res(y, w_out.astype(BF16), b_out, x, g1, tm=512, tn=d)


def kernel(x, c, ctx, c_ctx, mod_w, mod_b, norm_mix_w, norm_ffn_w, ffn_w1, ffn_w3, ffn_w2, ev_w_in, ev_conv_w, ev_conv_b, ev_dt_bias, ev_a_log, ev_d_skip, ev_ssd_norm_w, ev_q_norm_w, ev_w_uq, ev_kv_norm_w, ev_w_ukv, ev_w_o, hy_w_in, hy_b_in, hy_short_w, hy_short_b, hy_fw1, hy_fb1, hy_fw_mid, hy_fb_mid, hy_freq, hy_fw_out, hy_fbias, hy_w_out, hy_b_out, final_norm_w):
    assert x.shape[0] == 1 and mod_w.shape[0] == 2
    xs = x[0]
    xc = ctx[0]
    d = xs.shape[1]
    vecs = jnp.concatenate([c.reshape(1, d), c_ctx.reshape(1, d), jnp.zeros((6, d), F32)], axis=0)
    depth = mod_w.shape[0]
    w1_b, w3_b, w2_b = ffn_w1.astype(BF16), ffn_w3.astype(BF16), ffn_w2.astype(BF16)
    for i in range(depth):
        mods = adaln_vectors(vecs, mod_w, mod_b, i)
        sh2, sc2, g2 = (mods[0:1, j * d:(j + 1) * d] for j in range(3, 6))
        if i % 2 == 0:
            e = i // 2
            xs = _even_layer(xs, xc, mods, norm_mix_w[i], ev_w_in[e], ev_conv_w[e], ev_conv_b[e], ev_dt_bias[e],
                             ev_a_log[e], ev_d_skip[e], ev_ssd_norm_w[e], ev_q_norm_w[e], ev_w_uq[e],
                             ev_kv_norm_w[e], ev_w_ukv[e], ev_w_o[e])
        else:
            o = i // 2
            xs = _odd_layer(xs, mods, norm_mix_w[i], hy_w_in[o], hy_b_in[o], hy_short_w[o], hy_short_b[o],
                            hy_fw1[o], hy_fb1[o], hy_fw_mid[o], hy_fb_mid[o], hy_freq[o], hy_fw_out[o],
                            hy_fbias[o], hy_w_out[o], hy_b_out[o])
        xs = ffn(xs, norm_ffn_w[i], sh2, sc2, g2, w1_b, w3_b, w2_b, final_norm_w, layer=i, tm=512, tf=512,
                 final_norm=(i == depth - 1))
    return xs[None]
```

```python
import functools
import math

import jax
import jax.numpy as jnp
from jax import lax
from jax.experimental import pallas as pl
from jax.experimental.pallas import tpu as pltpu

F32 = jnp.float32
BF16 = jnp.bfloat16

EPS = 1e-6
GRID_W = 64
CTX_LEN = 256
SSD_HEADS = 16
SSD_HEAD_DIM = 64
SSD_INNER = 1024
SSD_GROUPS = 2
SSD_HPG = 8
SSD_STATE = 128
SSD_CHUNK = 256
SSD_XBC = 1536
SSD_DT = 32
MLA_HEADS = 8
MLA_NOPE = 128
MLA_ROPE = 64
MLA_V = 128
MLA_Q_RANK = 512
MLA_KV_RANK = 512
ROPE_THETA = 10000.0
MLA_HEAD_PAD = 256
Q_SIDE = SSD_INNER + MLA_Q_RANK
HY_EMB = 33
HY_HID = 64
HY_FAST_DECAY = 0.3
HY_SLOW_DECAY = 1.5
HY_TARGET = 1e-2
FFT_N2 = 128
SLAB_TILE = 8

P_Z, P_CQ, P_XBC, P_CKV, P_KR, P_DT, P_COLS = 0, 1024, 1536, 3072, 3584, 3712, 3840

VMEM_LIMIT = 56 * 1024 * 1024

ROW_TILE = 512
WIDE_ROW_TILES = (1024, 512)
PROJ_COL_TILE = 768
FFN_COL_TILE = 512
MERGE_ROW_TILE = 256
ATTN_Q_TILES = (2048, 1024, 512, 256)
ATTN_K_TILES = (1280, 1024, 768, 640, 512, 384, 256, 128)
FFT_COL_TILE = 256
FFT_MID_COL_TILE = 512
FILTER_ROW_TILE = 1024


def _cparams(*sem):
    return pltpu.CompilerParams(dimension_semantics=sem, vmem_limit_bytes=VMEM_LIMIT)


def _pick(n, cands):
    for c in cands:
        if n % c == 0:
            return c
    raise ValueError(f"no tile for {n}")


def _split(a):
    hi = a.astype(BF16)
    lo = (a - hi.astype(F32)).astype(BF16)
    return hi, lo


def _dot(a, b):
    return jnp.dot(a, b, preferred_element_type=F32)


def _silu(x):
    return x * (1.0 / (1.0 + jnp.exp(-x)))


def _modnorm(x, nw, sh, sc):
    ms = jnp.mean(x * x, axis=-1, keepdims=True)
    return (x * lax.rsqrt(ms + EPS) * nw) * (1.0 + sc) + sh


def _matvec_kernel(x_ref, w_ref, b_ref, o_ref):
    x = x_ref[...]
    o_ref[...] = _dot(_silu(x).astype(BF16), w_ref[...].astype(BF16)) + b_ref[...]


def adaln_vectors(vecs, w_all, b_all, layer):
    nl, k, n = w_all.shape
    tn = 1024
    return pl.pallas_call(
        _matvec_kernel,
        grid=(n // tn,),
        in_specs=[pl.BlockSpec((8, k), lambda j: (0, 0)),
                  pl.BlockSpec((None, k, tn), lambda j: (layer, 0, j)),
                  pl.BlockSpec((None, 1, tn), lambda j: (layer, 0, j))],
        out_specs=pl.BlockSpec((8, tn), lambda j: (0, j)),
        out_shape=jax.ShapeDtypeStruct((8, n), F32),
        compiler_params=_cparams("arbitrary"),
        name="adaln_vectors",
    )(vecs, w_all, b_all.reshape(nl, 1, n))


def _normmm_kernel(x_ref, nw_ref, sh_ref, sc_ref, w_ref, b_ref, o_ref, xn_ref):
    @pl.when(pl.program_id(1) == 0)
    def _():
        xn_ref[...] = _modnorm(x_ref[...].astype(F32), nw_ref[...], sh_ref[...], sc_ref[...]).astype(BF16)

    o_ref[...] = (_dot(xn_ref[...], w_ref[...]) + b_ref[...]).astype(o_ref.dtype)


def normmm(x, nw, sh, sc, w, b, *, tm, tn, out_dtype, x_col=0):
    m = x.shape[0]
    k, n = w.shape
    return pl.pallas_call(
        _normmm_kernel,
        grid=(m // tm, n // tn),
        in_specs=[pl.BlockSpec((tm, k), lambda i, j: (i, x_col)),
                  pl.BlockSpec((1, k), lambda i, j: (0, 0)),
                  pl.BlockSpec((1, k), lambda i, j: (0, 0)),
                  pl.BlockSpec((1, k), lambda i, j: (0, 0)),
                  pl.BlockSpec((k, tn), lambda i, j: (0, j)),
                  pl.BlockSpec((1, tn), lambda i, j: (0, j))],
        out_specs=pl.BlockSpec((tm, tn), lambda i, j: (i, j)),
        out_shape=jax.ShapeDtypeStruct((m, n), out_dtype),
        scratch_shapes=[pltpu.VMEM((tm, k), BF16)],
        compiler_params=_cparams("parallel", "arbitrary"),
        name="normmm",
    )(x, nw.reshape(1, k), sh.reshape(1, k), sc.reshape(1, k), w, b.reshape(1, n))


def _normmm_conv_kernel(x_ref, xprev_ref, xnext_ref, nw_ref, sh_ref, sc_ref, w_ref, b_ref, cw_ref, cb_ref,
                        o_ref, xn_ref, xh_ref):
    i = pl.program_id(0)

    @pl.when(pl.program_id(1) == 0)
    def _():
        nw, sh, sc = nw_ref[...], sh_ref[...], sc_ref[...]
        xn_ref[...] = _modnorm(x_ref[...], nw, sh, sc).astype(BF16)
        xh_ref[0:8, :] = _modnorm(xprev_ref[...], nw, sh, sc).astype(BF16)
        xh_ref[8:16, :] = _modnorm(xnext_ref[...], nw, sh, sc).astype(BF16)

    w = w_ref[...]
    b = b_ref[...]
    y = _dot(xn_ref[...], w) + b
    yh = _dot(xh_ref[...], w) + b
    prev_row = jnp.where(i == 0, 0.0, yh[7:8, :])
    next_row = jnp.where(i == pl.num_programs(0) - 1, 0.0, yh[8:9, :])
    yp, yn = _shift_rows(y, prev_row, next_row)
    o_ref[...] = yp * cw_ref[0:1, :] + y * cw_ref[1:2, :] + yn * cw_ref[2:3, :] + cb_ref[...]


def normmm_conv(x, nw, sh, sc, w, b, cw, cb, *, tm, tn):
    m, k = x.shape
    n = w.shape[1]
    nrb = m // 8
    tb = tm // 8
    vec = pl.BlockSpec((1, k), lambda i, j: (0, 0))
    return pl.pallas_call(
        _normmm_conv_kernel,
        grid=(m // tm, n // tn),
        in_specs=[pl.BlockSpec((tm, k), lambda i, j: (i, 0)),
                  pl.BlockSpec((8, k), lambda i, j: (jnp.maximum(i * tb - 1, 0), 0)),
                  pl.BlockSpec((8, k), lambda i, j: (jnp.minimum((i + 1) * tb, nrb - 1), 0)),
                  vec, vec, vec,
                  pl.BlockSpec((k, tn), lambda i, j: (0, j)),
                  pl.BlockSpec((1, tn), lambda i, j: (0, j)),
                  pl.BlockSpec((3, tn), lambda i, j: (0, j)),
                  pl.BlockSpec((1, tn), lambda i, j: (0, j))],
        out_specs=pl.BlockSpec((tm, tn), lambda i, j: (i, j)),
        out_shape=jax.ShapeDtypeStruct((m, n), F32),
        scratch_shapes=[pltpu.VMEM((tm, k), BF16), pltpu.VMEM((16, k), BF16)],
        compiler_params=_cparams("parallel", "arbitrary"),
        name="normmm_conv",
    )(x, x, x, nw.reshape(1, k), sh.reshape(1, k), sc.reshape(1, k), w, b.reshape(1, n), cw, cb.reshape(1, n))


def _qproj_kernel(x_ref, nw_ref, wa_ref, wb_ref, ta_ref, tb_ref, o_ref):
    x = x_ref[...]
    ms = jnp.mean(x * x, axis=-1, keepdims=True)
    xn = (x * lax.rsqrt(ms + EPS) * nw_ref[...]).astype(BF16)
    ta = ta_ref[...]
    tb = tb_ref[...]
    hp = MLA_HEAD_PAD
    for h in range(MLA_HEADS):
        cols = slice(h * hp, (h + 1) * hp)
        o_ref[:, cols] = (_dot(xn, wa_ref[:, cols]) * ta + _dot(xn, wb_ref[:, cols]) * tb).astype(o_ref.dtype)


def qproj(p, nw, wa, wb, ta, tb, *, tm):
    m = ta.shape[0]
    k = MLA_Q_RANK
    hp = MLA_HEAD_PAD
    n = MLA_HEADS * hp
    return pl.pallas_call(
        _qproj_kernel,
        grid=(m // tm,),
        in_specs=[pl.BlockSpec((tm, k), lambda i: (i, P_CQ // MLA_Q_RANK)),
                  pl.BlockSpec((1, k), lambda i: (0, 0)),
                  pl.BlockSpec((k, n), lambda i: (0, 0)),
                  pl.BlockSpec((k, n), lambda i: (0, 0)),
                  pl.BlockSpec((tm, hp), lambda i: (i, 0)),
                  pl.BlockSpec((tm, hp), lambda i: (i, 0))],
        out_specs=pl.BlockSpec((tm, n), lambda i: (i, 0)),
        out_shape=jax.ShapeDtypeStruct((m, n), BF16),
        compiler_params=_cparams("parallel"),
        name="mla_qproj",
    )(p, nw.reshape(1, k), wa, wb, ta, tb)


def _kv_kernel(cl_ref, cc_ref, rl_ref, rc_ref, tc_ref, ts_ref, nw_ref, w_ref, k_ref, v_ref, *, n_lat_tiles):
    is_ctx = pl.program_id(0) >= n_lat_tiles
    x = jnp.where(is_ctx, cc_ref[...], cl_ref[...])
    ms = jnp.mean(x * x, axis=-1, keepdims=True)
    kv = _dot((x * lax.rsqrt(ms + EPS) * nw_ref[...]).astype(BF16), w_ref[...]).astype(BF16)
    krr = jnp.where(is_ctx, rc_ref[...], rl_ref[...])
    kr = krr[:, :MLA_ROPE]
    kr_rot = kr * tc_ref[...][:, :MLA_ROPE] + krr[:, MLA_ROPE:] * ts_ref[...][:, :MLA_ROPE]
    kr = jnp.where(is_ctx, kr, kr_rot)
    tail = jnp.concatenate([kr, jnp.zeros_like(kr)], axis=-1).astype(BF16)
    for h in range(MLA_HEADS):
        base = h * (MLA_NOPE + MLA_V)
        k_ref[:, h * MLA_HEAD_PAD:h * MLA_HEAD_PAD + MLA_NOPE] = kv[:, base:base + MLA_NOPE]
        k_ref[:, h * MLA_HEAD_PAD + MLA_NOPE:(h + 1) * MLA_HEAD_PAD] = tail
        v_ref[:, h * MLA_V:(h + 1) * MLA_V] = kv[:, base + MLA_NOPE:base + MLA_NOPE + MLA_V]


def kv_project(p_lat, p_ctx, tc, ts, nw, w):
    tm = CTX_LEN
    nl = p_lat.shape[0] // tm
    rows = p_lat.shape[0] + p_ctx.shape[0]
    k = MLA_KV_RANK
    lat = lambda i: jnp.minimum(i, nl - 1)
    return pl.pallas_call(
        functools.partial(_kv_kernel, n_lat_tiles=nl),
        grid=(rows // tm,),
        in_specs=[pl.BlockSpec((tm, k), lambda i: (lat(i), P_CKV // MLA_KV_RANK)),
                  pl.BlockSpec((tm, k), lambda i: (0, P_CKV // MLA_KV_RANK)),
                  pl.BlockSpec((tm, 128), lambda i: (lat(i), P_KR // 128)),
                  pl.BlockSpec((tm, 128), lambda i: (0, P_KR // 128)),
                  pl.BlockSpec((tm, 128), lambda i: (lat(i), 0)),
                  pl.BlockSpec((tm, 128), lambda i: (lat(i), 0)),
                  pl.BlockSpec((1, k), lambda i: (0, 0)),
                  pl.BlockSpec(w.shape, lambda i: (0, 0))],
        out_specs=[pl.BlockSpec((tm, MLA_HEADS * MLA_HEAD_PAD), lambda i: (i, 0)),
                   pl.BlockSpec((tm, MLA_HEADS * MLA_V), lambda i: (i, 0))],
        out_shape=[jax.ShapeDtypeStruct((rows, MLA_HEADS * MLA_HEAD_PAD), BF16),
                   jax.ShapeDtypeStruct((rows, MLA_HEADS * MLA_V), BF16)],
        compiler_params=_cparams("parallel"),
        name="mla_kv_project",
    )(p_lat, p_ctx, p_lat, p_ctx, tc, ts, nw.reshape(1, k), w)


def _attn_kernel(q_ref, k_ref, v_ref, o_ref, s0_ref, s1_ref, m_ref, acc_ref, *, nk):
    j = pl.program_id(2)

    def qk(s_ref):
        s_ref[...] = lax.dot_general(q_ref[...], k_ref[...], (((1,), (1,)), ((), ())), preferred_element_type=F32)

    def softmax_pv(s_ref):
        s = s_ref[...]
        m_prev = m_ref[...]
        m_new = jnp.maximum(m_prev, jnp.max(s, axis=-1, keepdims=True))
        alpha = jnp.exp2(m_prev - m_new)
        p = jnp.exp2(s - m_new).astype(BF16)
        v = v_ref[...]
        lane = lax.broadcasted_iota(jnp.int32, v.shape, 1)
        ones_col = jnp.where(lane == 0, 1.0, 0.0).astype(BF16)
        v_ext = jnp.concatenate([v, ones_col], axis=1)
        acc_ref[...] = alpha * acc_ref[...] + _dot(p, v_ext)
        m_ref[...] = m_new

    @pl.when(j == 0)
    def _():
        m_ref[...] = jnp.full(m_ref.shape, -jnp.inf, F32)
        acc_ref[...] = jnp.zeros(acc_ref.shape, F32)
        qk(s0_ref)

    mid = jnp.logical_and(j > 0, j < nk)

    @pl.when(jnp.logical_and(mid, j % 2 == 1))
    def _():
        softmax_pv(s0_ref)
        qk(s1_ref)

    @pl.when(jnp.logical_and(mid, j % 2 == 0))
    def _():
        softmax_pv(s1_ref)
        qk(s0_ref)

    @pl.when(j == nk)
    def _():
        softmax_pv(s1_ref if (nk - 1) % 2 else s0_ref)
        acc = acc_ref[...]
        o_ref[...] = (acc[:, :MLA_V] / acc[:, MLA_V:MLA_V + 1]).astype(o_ref.dtype)


def attention(q, k, v, *, tq, tk):
    lq = q.shape[0]
    s = k.shape[0]
    nk = s // tk
    return pl.pallas_call(
        functools.partial(_attn_kernel, nk=nk),
        grid=(MLA_HEADS, lq // tq, nk + 1),
        in_specs=[pl.BlockSpec((tq, MLA_HEAD_PAD), lambda h, i, j: (i, h)),
                  pl.BlockSpec((tk, MLA_HEAD_PAD), lambda h, i, j: (jnp.minimum(j, nk - 1), h)),
                  pl.BlockSpec((tk, MLA_V), lambda h, i, j: (jnp.maximum(j - 1, 0), h))],
        out_specs=pl.BlockSpec((tq, MLA_V), lambda h, i, j: (i, h)),
        out_shape=jax.ShapeDtypeStruct((lq, MLA_HEADS * MLA_V), BF16),
        scratch_shapes=[pltpu.VMEM((tq, tk), F32), pltpu.VMEM((tq, tk), F32), pltpu.VMEM((tq, 1), F32),
                        pltpu.VMEM((tq, 2 * MLA_V), F32)],
        compiler_params=_cparams("parallel", "parallel", "arbitrary"),
        name="mla_attention",
    )(q, k, v)


def _shift_rows(x, prev_row, next_row):
    tm = x.shape[0]
    rows = lax.broadcasted_iota(jnp.int32, x.shape, 0)
    xp = jnp.where(rows == 0, prev_row, pltpu.roll(x, 1, 0))
    xn = jnp.where(rows == tm - 1, next_row, pltpu.roll(x, tm - 1, 0))
    return xp, xn


def _dwconv_kernel(x_ref, xprev_ref, xnext_ref, w_ref, b_ref, o_ref, *, act):
    i = pl.program_id(0)
    x = x_ref[...]
    prev_row = jnp.where(i == 0, 0.0, xprev_ref[7:8, :])
    next_row = jnp.where(i == pl.num_programs(0) - 1, 0.0, xnext_ref[0:1, :])
    xp, xn = _shift_rows(x, prev_row, next_row)
    y = xp * w_ref[0:1, :] + x * w_ref[1:2, :] + xn * w_ref[2:3, :] + b_ref[...]
    if act:
        y = _silu(y)
    o_ref[...] = y


def dwconv3(x, w, b, *, tm, tc, col0, width, act):
    m = x.shape[0]
    cb = col0 // tc
    nrb = m // 8
    tb = tm // 8
    return pl.pallas_call(
        functools.partial(_dwconv_kernel, act=act),
        grid=(m // tm, width // tc),
        in_specs=[pl.BlockSpec((tm, tc), lambda i, j: (i, cb + j)),
                  pl.BlockSpec((8, tc), lambda i, j: (jnp.maximum(i * tb - 1, 0), cb + j)),
                  pl.BlockSpec((8, tc), lambda i, j: (jnp.minimum((i + 1) * tb, nrb - 1), cb + j)),
                  pl.BlockSpec((3, tc), lambda i, j: (0, j)),
                  pl.BlockSpec((1, tc), lambda i, j: (0, j))],
        out_specs=pl.BlockSpec((tm, tc), lambda i, j: (i, j)),
        out_shape=jax.ShapeDtypeStruct((m, width), F32),
        compiler_params=_cparams("arbitrary", "arbitrary"),
        name="dwconv3",
    )(x, x, x, w, b.reshape(1, width))


def _softplus(x):
    return jnp.maximum(x, 0.0) + jnp.log(1.0 + jnp.exp(-jnp.abs(x)))


def _ssd_prepare(d, xbc, dt_ref, dtt_ref, bias_ref, biast_ref, a_ref, at_ref, tri_ref):
    q = SSD_CHUNK
    dt = _softplus(dt_ref[0] + bias_ref[d])
    dtt = _softplus(dtt_ref[0, 0] + biast_ref[d])
    dta_hi, dta_lo = _split(dt * a_ref[d])
    dtat_hi, dtat_lo = _split(dtt * at_ref[d])
    tri = tri_ref[d]
    trit = tri_ref[1 - d]
    acum = _dot(tri, dta_hi) + _dot(tri, dta_lo)
    acumt = _dot(dtat_hi, trit) + _dot(dtat_lo, trit)
    total = acum[q - 1:q, :] if d == 0 else acum[0:1, :]
    return dict(xbc=xbc, dt=dt, acum=acum, acumt=acumt, to_end=jnp.exp(total - acum),
                from_start=jnp.exp(acum), chunk_decay=jnp.exp(total), mask=tri > 0.5)


def _per_head_lanes(a, g):
    return jnp.concatenate([jnp.broadcast_to(a[:, h:h + 1], (a.shape[0], SSD_HEAD_DIM))
                            for h in range(g * SSD_HPG, (g + 1) * SSD_HPG)], axis=1)


def _ssd_group(d, p, g, h_ref):
    xbc = p["xbc"]
    gw = SSD_HPG * SSD_HEAD_DIM
    bm = xbc[:, SSD_INNER + g * SSD_STATE:SSD_INNER + (g + 1) * SSD_STATE]
    cm = xbc[:, SSD_INNER + (SSD_GROUPS + g) * SSD_STATE:SSD_INNER + (SSD_GROUPS + g + 1) * SSD_STATE]
    bm_b = bm.astype(BF16)
    cm_b = cm.astype(BF16)
    cb = lax.dot_general(cm_b, bm_b, (((1,), (1,)), ((), ())), preferred_element_type=F32)
    xdt = xbc[:, g * gw:(g + 1) * gw] * _per_head_lanes(p["dt"], g)
    h_prev = h_ref[d * SSD_GROUPS + g]
    y_off = _dot(cm_b, h_prev.astype(BF16)) * _per_head_lanes(p["from_start"], g)
    states = _dot(bm.T.astype(BF16), (xdt * _per_head_lanes(p["to_end"], g)).astype(BF16))
    h_ref[d * SSD_GROUPS + g] = h_prev * _per_head_lanes(p["chunk_decay"], g) + states
    return dict(cb=cb, xdt_b=xdt.astype(BF16), y_off=y_off)


def _ssd_head_pair(p, grp, g, rp, o_ref):
    ys = []
    for r in (2 * rp, 2 * rp + 1):
        h = g * SSD_HPG + r
        seg = p["acum"][:, h:h + 1] - p["acumt"][h:h + 1, :]
        decay = jnp.exp(jnp.where(p["mask"], seg, -jnp.inf))
        ys.append(_dot((grp["cb"] * decay).astype(BF16), grp["xdt_b"][:, r * SSD_HEAD_DIM:(r + 1) * SSD_HEAD_DIM]))
    w = 2 * SSD_HEAD_DIM
    col = (g * SSD_HPG + 2 * rp) * SSD_HEAD_DIM
    o_ref[:, col:col + w] = jnp.concatenate(ys, axis=1) + grp["y_off"][:, rp * w:(rp + 1) * w]


def _ssd_kernel(xf_ref, xfc_ref, xb_ref, xbc_ref, dtf_ref, dtb_ref, dttf_ref, dttb_ref, bias_ref, biast_ref, a_ref, at_ref, tri_ref,
                of_ref, ob_ref, h_ref):
    @pl.when(pl.program_id(0) == 0)
    def _():
        h_ref[...] = jnp.zeros(h_ref.shape, F32)

    s = pl.program_id(0)
    ncx = CTX_LEN // SSD_CHUNK
    x_fwd = jnp.where(s < ncx, xfc_ref[...], xf_ref[...])
    x_bwd = jnp.where(s < ncx, xbc_ref[...], xb_ref[...])
    prm = (_ssd_prepare(0, x_fwd, dtf_ref, dttf_ref, bias_ref, biast_ref, a_ref, at_ref, tri_ref),
           _ssd_prepare(1, x_bwd, dtb_ref, dttb_ref, bias_ref, biast_ref, a_ref, at_ref, tri_ref))
    outs = (of_ref, ob_ref)
    for g in range(SSD_GROUPS):
        grp = (_ssd_group(0, prm[0], g, h_ref), _ssd_group(1, prm[1], g, h_ref))
        for rp in range(SSD_HPG // 2):
            for d in range(2):
                _ssd_head_pair(prm[d], grp[d], g, rp, outs[d])


def ssd_scan(xbc_lat, xbc_ctx, dt2, dt2t, bias2, bias2t, a2, a2t, tri2, *, n_lat_chunks):
    nc = n_lat_chunks
    ncx = CTX_LEN // SSD_CHUNK
    tot = nc + ncx
    q = SSD_CHUNK

    def cf(s):
        return lax.rem(s + nc, tot)

    def cbk(s):
        return tot - 1 - s

    def full(shape):
        return pl.BlockSpec(shape, lambda s: (0,) * len(shape))

    return pl.pallas_call(
        _ssd_kernel,
        grid=(tot,),
        in_specs=[pl.BlockSpec((q, SSD_XBC), lambda s: (jnp.minimum(cf(s), nc - 1), 0)),
                  pl.BlockSpec((q, SSD_XBC), lambda s: (jnp.maximum(cf(s) - nc, 0), 0)),
                  pl.BlockSpec((q, SSD_XBC), lambda s: (jnp.minimum(cbk(s), nc - 1), 0)),
                  pl.BlockSpec((q, SSD_XBC), lambda s: (jnp.maximum(cbk(s) - nc, 0), 0)),
                  pl.BlockSpec((1, q, SSD_HEADS), lambda s: (0, cf(s), 0)),
                  pl.BlockSpec((1, q, SSD_HEADS), lambda s: (1, cbk(s), 0)),
                  pl.BlockSpec((1, 1, SSD_HEADS, q), lambda s: (0, cf(s), 0, 0)),
                  pl.BlockSpec((1, 1, SSD_HEADS, q), lambda s: (1, cbk(s), 0, 0)),
                  full((2, 1, SSD_HEADS)), full((2, SSD_HEADS, 1)), full((2, 1, SSD_HEADS)),
                  full((2, SSD_HEADS, 1)), full((2, q, q))],
        out_specs=[pl.BlockSpec((q, SSD_INNER), lambda s: (jnp.where(cf(s) >= nc, 0, cf(s)), 0)),
                   pl.BlockSpec((q, SSD_INNER), lambda s: (jnp.where(cbk(s) >= nc, nc - 1, cbk(s)), 0))],
        out_shape=[jax.ShapeDtypeStruct((nc * q, SSD_INNER), F32), jax.ShapeDtypeStruct((nc * q, SSD_INNER), F32)],
        scratch_shapes=[pltpu.VMEM((2 * SSD_GROUPS, SSD_STATE, SSD_HPG * SSD_HEAD_DIM), F32)],
        compiler_params=_cparams("arbitrary"),
        name="ssd_scan",
    )(xbc_lat, xbc_ctx, xbc_lat, xbc_ctx, dt2, dt2, dt2t, dt2t, bias2, bias2t, a2, a2t, tri2)


def _merge_kernel(o_ref, yf_ref, yb_ref, xs_ref, z_ref, dsk_ref, nw_ref, w_ref, x_ref, g_ref, out_ref, a_ref):
    @pl.when(pl.program_id(1) == 0)
    def _():
        a_ref[:, :MLA_HEADS * MLA_V] = o_ref[...]
        y = yf_ref[...] + yb_ref[...] + dsk_ref[...] * xs_ref[...]
        gy = y * _silu(z_ref[...])
        gw = SSD_INNER // SSD_GROUPS
        for g in range(SSD_GROUPS):
            part = gy[:, g * gw:(g + 1) * gw]
            ms = jnp.mean(part * part, axis=-1, keepdims=True)
            a_ref[:, MLA_HEADS * MLA_V + g * gw:MLA_HEADS * MLA_V + (g + 1) * gw] = (
                part * lax.rsqrt(ms + EPS) * nw_ref[:, g * gw:(g + 1) * gw]).astype(BF16)

    out_ref[...] = x_ref[...] + g_ref[...] * _dot(a_ref[...], w_ref[...])


def merge_out(o_att, yf, yb, xbc, p, dsk, nw, w_o, x, gate, *, tm, tn):
    m, n = x.shape
    kw = w_o.shape[0]
    return pl.pallas_call(
        _merge_kernel,
        grid=(m // tm, n // tn),
        in_specs=[pl.BlockSpec((tm, MLA_HEADS * MLA_V), lambda i, j: (i, 0)),
                  pl.BlockSpec((tm, SSD_INNER), lambda i, j: (i, 0)),
                  pl.BlockSpec((tm, SSD_INNER), lambda i, j: (i, 0)),
                  pl.BlockSpec((tm, SSD_INNER), lambda i, j: (i, 0)),
                  pl.BlockSpec((tm, SSD_INNER), lambda i, j: (i, 0)),
                  pl.BlockSpec((1, SSD_INNER), lambda i, j: (0, 0)),
                  pl.BlockSpec((1, SSD_INNER), lambda i, j: (0, 0)),
                  pl.BlockSpec((kw, tn), lambda i, j: (0, j)),
                  pl.BlockSpec((tm, tn), lambda i, j: (i, j)),
                  pl.BlockSpec((1, tn), lambda i, j: (0, j))],
        out_specs=pl.BlockSpec((tm, tn), lambda i, j: (i, j)),
        out_shape=jax.ShapeDtypeStruct((m, n), F32),
        scratch_shapes=[pltpu.VMEM((tm, kw), BF16)],
        compiler_params=_cparams("parallel", "arbitrary"),
        name="merge_out",
    )(o_att, yf, yb, xbc, p, dsk, nw, w_o, x, gate)


def _mmres_kernel(a_ref, w_ref, b_ref, x_ref, g_ref, o_ref):
    o_ref[...] = x_ref[...] + g_ref[...] * (_dot(a_ref[...].astype(BF16), w_ref[...]) + b_ref[...])


def mm_res(a, w, b, x, gate, *, tm, tn):
    m, k = a.shape
    n = w.shape[1]
    return pl.pallas_call(
        _mmres_kernel,
        grid=(m // tm, n // tn),
        in_specs=[pl.BlockSpec((tm, k), lambda i, j: (i, 0)),
                  pl.BlockSpec((k, tn), lambda i, j: (0, j)),
                  pl.BlockSpec((1, tn), lambda i, j: (0, j)),
                  pl.BlockSpec((tm, tn), lambda i, j: (i, j)),
                  pl.BlockSpec((1, tn), lambda i, j: (0, j))],
        out_specs=pl.BlockSpec((tm, tn), lambda i, j: (i, j)),
        out_shape=jax.ShapeDtypeStruct((m, n), F32),
        compiler_params=_cparams("parallel", "arbitrary"),
        name="mm_res",
    )(a, w, b.reshape(1, n), x, gate)


def _ffn_kernel(x_ref, nw_ref, sh_ref, sc_ref, g_ref, w1_ref, w3_ref, w2_ref, fw_ref, o_ref, xn_ref, acc_ref,
                *, final_norm):
    f = pl.program_id(1)

    @pl.when(f == 0)
    def _():
        xn_ref[...] = _modnorm(x_ref[...], nw_ref[...], sh_ref[...], sc_ref[...]).astype(BF16)
        acc_ref[...] = jnp.zeros(acc_ref.shape, F32)

    xn = xn_ref[...]
    a = _dot(xn, w1_ref[...])
    b = _dot(xn, w3_ref[...])
    acc_ref[...] += _dot((_silu(a) * b).astype(BF16), w2_ref[...])

    @pl.when(f == pl.num_programs(1) - 1)
    def _():
        y = x_ref[...] + g_ref[...] * acc_ref[...]
        if final_norm:
            ms = jnp.mean(y * y, axis=-1, keepdims=True)
            y = y * lax.rsqrt(ms + EPS) * fw_ref[...]
        o_ref[...] = y


def ffn(x, nw, sh, sc, gate, w1, w3, w2, fw, *, layer, tm, tf, final_norm):
    m, dm = x.shape
    dff = w1.shape[2]
    vec = pl.BlockSpec((1, dm), lambda i, f: (0, 0))
    return pl.pallas_call(
        functools.partial(_ffn_kernel, final_norm=final_norm),
        grid=(m // tm, dff // tf),
        in_specs=[pl.BlockSpec((tm, dm), lambda i, f: (i, 0)), vec, vec, vec, vec,
                  pl.BlockSpec((None, dm, tf), lambda i, f: (layer, 0, f)),
                  pl.BlockSpec((None, dm, tf), lambda i, f: (layer, 0, f)),
                  pl.BlockSpec((None, tf, dm), lambda i, f: (layer, f, 0)),
                  vec],
        out_specs=pl.BlockSpec((tm, dm), lambda i, f: (i, 0)),
        out_shape=jax.ShapeDtypeStruct((m, dm), F32),
        scratch_shapes=[pltpu.VMEM((tm, dm), BF16), pltpu.VMEM((tm, dm), F32)],
        compiler_params=_cparams("parallel", "arbitrary"),
        name="ffn",
    )(x, nw.reshape(1, dm), sh, sc, gate, w1, w3, w2, fw.reshape(1, dm))


def _filter_hidden_kernel(fvec_ref, w1_ref, b1_ref, wm_ref, bm_ref, freq_ref, h_ref, *, seq_len, tr):
    i = pl.program_id(0)
    n1_total = 2 * seq_len // FFT_N2
    shift = n1_total.bit_length() - 1
    assert 1 << shift == n1_total
    p = i * tr + lax.broadcasted_iota(jnp.int32, (tr, 1), 0)
    n = ((p & (n1_total - 1)) << (FFT_N2.bit_length() - 1)) + (p >> shift)
    lag = jnp.where(n < seq_len, n, 2 * seq_len - n).astype(F32)
    t = lag / float(seq_len - 1)
    ang = lag * (2.0 * math.pi / seq_len)
    lane = lax.broadcasted_iota(jnp.int32, (tr, 128), 1)
    feat = jnp.sin(ang * fvec_ref[0:1, :] + fvec_ref[1:2, :])
    emb = jnp.where(lane == 0, t, jnp.where(lane < HY_EMB, feat, 0.0))
    fr = freq_ref[...]
    h2 = tr // 2
    e = emb.astype(BF16)
    z = jnp.concatenate([_dot(e[:h2], w1_ref[...]), _dot(e[h2:], w1_ref[...])], axis=1)
    hid = jnp.sin(fr * (z + b1_ref[...]))
    for j in range(wm_ref.shape[0]):
        hid = jnp.sin(fr * (_dot(hid.astype(BF16), wm_ref[j]) + bm_ref[j]))
    h_ref[0:h2, :] = hid[:, :HY_HID]
    h_ref[h2:tr, :] = hid[:, HY_HID:]


def filter_hidden(fvec, w1, b1, wm, bm, freq, *, seq_len, tr):
    full = lambda shp: pl.BlockSpec(shp, lambda i: (0,) * len(shp))
    return pl.pallas_call(
        functools.partial(_filter_hidden_kernel, seq_len=seq_len, tr=tr),
        grid=(2 * seq_len // tr,),
        in_specs=[full((2, 128)), full(w1.shape), full(b1.shape), full(wm.shape), full(bm.shape), full(freq.shape)],
        out_specs=pl.BlockSpec((tr, HY_HID), lambda i: (i, 0)),
        out_shape=jax.ShapeDtypeStruct((2 * seq_len, HY_HID), F32),
        compiler_params=_cparams("parallel"),
        name="hyena_filter_hidden",
    )(fvec, w1, b1, wm, bm, freq)


ROW_GROUP = 16


def _filter_stage1_kernel(f_ref, hid_ref, wo_ref, delta_ref, o_ref, abs_ref, os_ref, *, seq_len):
    i = pl.program_id(1)
    n1 = hid_ref.shape[1]
    half = n1 // 2
    n1_idx = lax.broadcasted_iota(jnp.int32, (n1, 1), 0)
    delta = delta_ref[...]
    f = f_ref[...]
    wo_past, wo_future = wo_ref[0], wo_ref[1]

    @pl.when(i == 0)
    def _():
        abs_ref[...] = jnp.zeros(abs_ref.shape, F32)

    asum = jnp.zeros(abs_ref.shape, F32)
    for r in range(ROW_GROUP):
        hb = hid_ref[r].astype(BF16)
        k = jnp.concatenate([_dot(hb[:half], wo_past), _dot(hb[half:], wo_future)], axis=0)
        n = n1_idx * FFT_N2 + (i * ROW_GROUP + r)
        lag = jnp.where(n < seq_len, n, 2 * seq_len - n).astype(F32)
        k = k * jnp.exp(-(lag / float(seq_len - 1)) * delta)
        k = jnp.where(n == seq_len, 0.0, k)
        asum = asum + jnp.sum(jnp.abs(k), axis=0, keepdims=True)
        os_ref[r] = _dot(f, k.astype(BF16))
    abs_ref[...] += asum
    o_ref[...] = jnp.swapaxes(os_ref[...], 0, 1).astype(o_ref.dtype)


def filter_stage1(f1, hid3, wo, delta, *, seq_len, tn):
    mrows, n1 = f1.shape
    dd = delta.shape[1]
    return pl.pallas_call(
        functools.partial(_filter_stage1_kernel, seq_len=seq_len),
        grid=(dd // tn, FFT_N2 // ROW_GROUP),
        in_specs=[pl.BlockSpec((mrows, n1), lambda j, i: (0, 0)),
                  pl.BlockSpec((ROW_GROUP, n1, HY_HID), lambda j, i: (i, 0, 0)),
                  pl.BlockSpec((2, HY_HID, tn), lambda j, i: (0, 0, j)),
                  pl.BlockSpec((1, tn), lambda j, i: (0, j))],
        out_specs=[pl.BlockSpec((mrows, ROW_GROUP, tn), lambda j, i: (0, i, j)),
                   pl.BlockSpec((1, tn), lambda j, i: (0, j))],
        out_shape=[jax.ShapeDtypeStruct((mrows, FFT_N2, dd), BF16), jax.ShapeDtypeStruct((1, dd), F32)],
        scratch_shapes=[pltpu.VMEM((ROW_GROUP, mrows, tn), F32)],
        compiler_params=_cparams("parallel", "arbitrary"),
        name="filter_stage1",
    )(f1, hid3, wo, delta)


def _fft1_kernel(f_ref, x_ref, o_ref, os_ref):
    xt = jnp.swapaxes(x_ref[...], 0, 1)
    for r in range(ROW_GROUP):
        os_ref[r] = _dot(f_ref[...], xt[r].astype(BF16))
    o_ref[...] = jnp.swapaxes(os_ref[...], 0, 1).astype(o_ref.dtype)


def _fft1_inv_kernel(g_ref, b_ref, v_ref, gate_ref, fb_ref, o_ref, os_ref):
    bt = jnp.swapaxes(b_ref[...].astype(F32), 0, 1)
    for r in range(ROW_GROUP):
        os_ref[r] = _dot(g_ref[...], bt[r].astype(BF16))
    conv = jnp.swapaxes(os_ref[...], 0, 1)
    o_ref[...] = gate_ref[...] * (conv + fb_ref[...] * v_ref[...])


def fft_stage1(f1, x3, *, d, tn, col_ofs=0):
    mrows, k = f1.shape
    n2 = x3.shape[1]
    nd = d // tn
    return pl.pallas_call(
        _fft1_kernel,
        grid=(n2 // ROW_GROUP, nd),
        in_specs=[pl.BlockSpec((mrows, k), lambda i, j: (0, 0)),
                  pl.BlockSpec((k, ROW_GROUP, tn), lambda i, j: (0, i, col_ofs * nd + j))],
        out_specs=pl.BlockSpec((mrows, ROW_GROUP, tn), lambda i, j: (0, i, j)),
        out_shape=jax.ShapeDtypeStruct((mrows, n2, d), BF16),
        scratch_shapes=[pltpu.VMEM((ROW_GROUP, mrows, tn), F32)],
        compiler_params=_cparams("parallel", "arbitrary"),
        name="fft_stage1",
    )(f1, x3)


def fft_stage1_inv(g1, b3, v3, v_ofs, gate3, gate_ofs, fbias, *, d, tn):
    mrows, k = g1.shape
    n2 = b3.shape[1]
    nd = d // tn
    return pl.pallas_call(
        _fft1_inv_kernel,
        grid=(n2 // ROW_GROUP, nd),
        in_specs=[pl.BlockSpec((mrows, k), lambda i, j: (0, 0)),
                  pl.BlockSpec((k, ROW_GROUP, tn), lambda i, j: (0, i, j)),
                  pl.BlockSpec((mrows, ROW_GROUP, tn), lambda i, j: (0, i, v_ofs * nd + j)),
                  pl.BlockSpec((mrows, ROW_GROUP, tn), lambda i, j: (0, i, gate_ofs * nd + j)),
                  pl.BlockSpec((1, tn), lambda i, j: (0, j))],
        out_specs=pl.BlockSpec((mrows, ROW_GROUP, tn), lambda i, j: (0, i, j)),
        out_shape=jax.ShapeDtypeStruct((mrows, n2, d), F32),
        scratch_shapes=[pltpu.VMEM((ROW_GROUP, mrows, tn), F32)],
        compiler_params=_cparams("parallel", "arbitrary"),
        name="fft_stage1_inv",
    )(g1, b3, v3, gate3, fbias)


def _fftmid_kernel(a_ref, ak_ref, mf_ref, mi_ref, scale_ref, o_ref):
    half = FFT_N2
    scale = scale_ref[...]
    for s in range(SLAB_TILE):
        mf = mf_ref[s]
        y = _dot(mf, a_ref[s])
        kf = _dot(mf, ak_ref[s])
        yr, yi = y[:half], y[half:]
        kr, ki = kf[:half], kf[half:]
        pr = (yr * kr - yi * ki) * scale
        pi = (yr * ki + yi * kr) * scale
        o_ref[s] = _dot(mi_ref[s], jnp.concatenate([pr, pi], axis=0).astype(BF16)).astype(o_ref.dtype)


def fft_mid(a3, ak3, ak_ofs, mf, mi, scale, *, dc):
    nsp, rows, d = a3.shape
    nd = d // dc
    blk = pl.BlockSpec((SLAB_TILE, rows, dc), lambda s, j: (s, 0, j))
    mat = pl.BlockSpec((SLAB_TILE, rows, rows), lambda s, j: (s, 0, 0))
    return pl.pallas_call(
        _fftmid_kernel,
        grid=(nsp // SLAB_TILE, nd),
        in_specs=[blk, pl.BlockSpec((SLAB_TILE, rows, dc), lambda s, j: (s, 0, ak_ofs * nd + j)), mat, mat,
                  pl.BlockSpec((1, dc), lambda s, j: (0, j))],
        out_specs=blk,
        out_shape=jax.ShapeDtypeStruct((nsp, rows, d), BF16),
        compiler_params=_cparams("parallel", "arbitrary"),
        name="fft_mid",
    )(a3, ak3, mf, mi, scale)


def _fft_tables(seq_len):
    n = 2 * seq_len
    n2 = FFT_N2
    n1 = n // n2
    ns = n1 // 2 + 1
    nsp = -(-ns // SLAB_TILE) * SLAB_TILE
    k1 = jnp.arange(nsp, dtype=jnp.int32)
    valid = (k1 < ns)
    m1 = jnp.arange(n1, dtype=jnp.int32)
    ang1 = (2.0 * math.pi / n1) * lax.rem(k1[:, None] * m1[None, :], n1).astype(F32)
    vf = valid[:, None].astype(F32)
    f1 = jnp.stack([jnp.cos(ang1) * vf, -jnp.sin(ang1) * vf], axis=1).reshape(2 * nsp, n1)
    wgt = jnp.where((k1 == 0) | (k1 == n1 // 2), 1.0, 2.0) * valid.astype(F32) / n
    g1 = jnp.stack([jnp.cos(ang1) * wgt[:, None], -jnp.sin(ang1) * wgt[:, None]], axis=1)
    g1 = g1.reshape(2 * nsp, n1).T[: n1 // 2]
    k2 = jnp.arange(n2, dtype=jnp.int32)
    m2 = jnp.arange(n2, dtype=jnp.int32)
    f = k1[:, None, None] + n1 * k2[None, :, None]
    ang2 = (2.0 * math.pi / n) * lax.rem(f * m2[None, None, :], n).astype(F32)
    vm = valid[:, None, None].astype(F32)
    c2, s2 = jnp.cos(ang2) * vm, jnp.sin(ang2) * vm
    mf = jnp.concatenate([jnp.concatenate([c2, s2], axis=2), jnp.concatenate([-s2, c2], axis=2)], axis=1)
    c2t, s2t = jnp.swapaxes(c2, 1, 2), jnp.swapaxes(s2, 1, 2)
    mi = jnp.concatenate([jnp.concatenate([c2t, -s2t], axis=2), jnp.concatenate([s2t, c2t], axis=2)], axis=1)
    return dict(n1=n1, nsp=nsp, f1=f1.astype(BF16), g1=g1.astype(BF16), mf=mf.astype(BF16), mi=mi.astype(BF16))


def long_conv_gate(v, v_cols, gate, gate_cols, fbias, ak3, ak_cols, scale, tabs, *, seq_len, d):
    n1, nsp = tabs["n1"], tabs["nsp"]
    half = n1 // 2
    v3 = v.reshape(half, FFT_N2, v.shape[1])
    g3 = gate.reshape(half, FFT_N2, gate.shape[1])
    a = fft_stage1(tabs["f1"][:, :half], v3, d=d, tn=FFT_COL_TILE, col_ofs=v_cols)
    b3 = fft_mid(a.reshape(nsp, 2 * FFT_N2, d), ak3, ak_cols, tabs["mf"], tabs["mi"], scale, dc=FFT_MID_COL_TILE)
    out = fft_stage1_inv(tabs["g1"], b3.reshape(2 * nsp, FFT_N2, d), v3, v_cols, g3, gate_cols, fbias, d=d,
                         tn=FFT_COL_TILE)
    return out.reshape(seq_len, d)


def _rope_tables(seq_len, scale):
    n_freq = MLA_ROPE // 4
    rows = seq_len // GRID_W
    row = jnp.repeat(jnp.arange(rows, dtype=F32), GRID_W)
    col = jnp.tile(jnp.arange(GRID_W, dtype=F32), rows)
    inv = ROPE_THETA ** (-jnp.arange(n_freq, dtype=F32) / n_freq)
    ang = jnp.stack([row[:, None] * inv, col[:, None] * inv], axis=1)
    cos = jnp.broadcast_to(jnp.cos(ang)[:, :, None, :], (seq_len, 2, 2, n_freq)).reshape(seq_len, MLA_ROPE)
    sin = jnp.broadcast_to(jnp.sin(ang)[:, :, None, :], (seq_len, 2, 2, n_freq)).reshape(seq_len, MLA_ROPE)
    one = jnp.ones((seq_len, MLA_NOPE), F32)
    zero = jnp.zeros((seq_len, MLA_NOPE), F32)
    z64 = jnp.zeros((seq_len, MLA_HEAD_PAD - MLA_NOPE - MLA_ROPE), F32)
    ta = jnp.concatenate([one, cos, z64], axis=1) * scale
    tb = jnp.concatenate([zero, sin, z64], axis=1) * scale
    tc = jnp.concatenate([cos, z64], axis=1)
    ts = jnp.concatenate([sin, z64], axis=1)
    return ta, tb, tc, ts


def _rope_swap_cols(w):
    n_freq = MLA_ROPE // 4
    w4 = w.reshape(w.shape[0], 2, 2, n_freq)
    return jnp.stack([-w4[:, :, 1], w4[:, :, 0]], axis=2).reshape(w.shape[0], MLA_ROPE)


def _even_layer(x, ctx, mods, norm_mix_w, w_in, conv_w, conv_b, dt_bias, a_log, d_skip, ssd_norm_w,
                q_norm_w, w_uq, kv_norm_w, w_ukv, w_o):
    seq_len, d = x.shape
    sh1, sc1, g1 = (mods[0:1, i * d:(i + 1) * d] for i in range(3))
    csh1, csc1 = mods[1:2, 0:d], mods[1:2, d:2 * d]

    o1 = Q_SIDE + SSD_XBC
    o2 = o1 + SSD_DT
    o3 = o2 + MLA_KV_RANK
    w_kr = w_in[:, o3:]
    w_ext = jnp.concatenate([w_in[:, :Q_SIDE], w_in[:, Q_SIDE:o1], w_in[:, o2:o3], w_kr, _rope_swap_cols(w_kr),
                             w_in[:, o1:o2], jnp.zeros((d, P_COLS - P_DT - SSD_DT), F32)], axis=1).astype(BF16)
    zb = jnp.zeros((P_COLS,), F32)
    p_lat = normmm(x, norm_mix_w, sh1, sc1, w_ext, zb, tm=_pick(seq_len, WIDE_ROW_TILES), tn=PROJ_COL_TILE,
                   out_dtype=F32)
    p_ctx = normmm(ctx, norm_mix_w, csh1, csc1, w_ext, zb, tm=CTX_LEN, tn=PROJ_COL_TILE, out_dtype=F32)

    scale = float(MLA_NOPE + MLA_ROPE) ** -0.5 * math.log2(math.e)
    ta, tb, tc, ts = _rope_tables(seq_len, scale)
    wq = w_uq.reshape(MLA_Q_RANK, MLA_HEADS, MLA_NOPE + MLA_ROPE)
    zpad = jnp.zeros((MLA_Q_RANK, MLA_HEADS, MLA_HEAD_PAD - MLA_NOPE - MLA_ROPE), F32)
    wa = jnp.concatenate([wq, zpad], axis=2).reshape(MLA_Q_RANK, -1).astype(BF16)
    wr = wq[:, :, MLA_NOPE:]
    n_freq = MLA_ROPE // 4
    wr4 = wr.reshape(MLA_Q_RANK, MLA_HEADS, 2, 2, n_freq)
    wsw = jnp.stack([-wr4[:, :, :, 1], wr4[:, :, :, 0]], axis=3).reshape(MLA_Q_RANK, MLA_HEADS, MLA_ROPE)
    wb = jnp.concatenate([jnp.zeros((MLA_Q_RANK, MLA_HEADS, MLA_NOPE), F32), wsw, zpad], axis=2)
    wb = wb.reshape(MLA_Q_RANK, -1).astype(BF16)
    q = qproj(p_lat, q_norm_w, wa, wb, ta, tb, tm=ROW_TILE)

    k_all, v_all = kv_project(p_lat, p_ctx, tc, ts, kv_norm_w, w_ukv.astype(BF16))
    s_tot = seq_len + CTX_LEN
    o_att = attention(q, k_all, v_all, tq=_pick(seq_len, ATTN_Q_TILES), tk=_pick(s_tot, ATTN_K_TILES))

    xbc_lat = dwconv3(p_lat, conv_w, conv_b, tm=ROW_TILE, tc=SSD_XBC, col0=P_XBC, width=SSD_XBC, act=True)
    xbc_ctx = dwconv3(p_ctx, conv_w, conv_b, tm=CTX_LEN, tc=SSD_XBC, col0=P_XBC, width=SSD_XBC, act=True)
    dt_all = jnp.concatenate([p_lat[:, P_DT:P_DT + SSD_DT], p_ctx[:, P_DT:P_DT + SSD_DT]], axis=0)
    nch = s_tot // SSD_CHUNK
    dt2 = dt_all.reshape(s_tot, 2, SSD_HEADS).transpose(1, 0, 2)
    dt2t = dt2.reshape(2, nch, SSD_CHUNK, SSD_HEADS).transpose(0, 1, 3, 2)
    bias2 = dt_bias.reshape(2, 1, SSD_HEADS)
    bias2t = dt_bias.reshape(2, SSD_HEADS, 1)
    a_neg = -jnp.exp(a_log.astype(F32))
    a2 = a_neg.reshape(2, 1, SSD_HEADS)
    a2t = a_neg.reshape(2, SSD_HEADS, 1)
    lower = jnp.tril(jnp.ones((SSD_CHUNK, SSD_CHUNK), F32))
    tri2 = jnp.stack([lower, lower.T]).astype(BF16)
    yf, yb = ssd_scan(xbc_lat, xbc_ctx, dt2, dt2t, bias2, bias2t, a2, a2t, tri2, n_lat_chunks=seq_len // SSD_CHUNK)

    dsk = jnp.repeat(d_skip[0] + d_skip[1], SSD_HEAD_DIM).reshape(1, SSD_INNER)
    return merge_out(o_att, yf, yb, xbc_lat, p_lat, dsk, ssd_norm_w.reshape(1, SSD_INNER), w_o.astype(BF16), x, g1,
                     tm=MERGE_ROW_TILE, tn=d)


def _odd_layer(x, mods, norm_mix_w, w_in, b_in, short_w, short_b, fw1, fb1, fw_mid, fb_mid, freq, fw_out,
               fbias, w_out, b_out):
    seq_len, d = x.shape
    sh1, sc1, g1 = (mods[0:1, i * d:(i + 1) * d] for i in range(3))
    pc = normmm_conv(x, norm_mix_w, sh1, sc1, w_in.astype(BF16), b_in, short_w, short_b,
                     tm=_pick(seq_len, WIDE_ROW_TILES), tn=PROJ_COL_TILE)

    tabs = _fft_tables(seq_len)
    bands = (HY_EMB - 1) // 2
    fband = jnp.linspace(1e-4, bands - 1, bands, dtype=F32)
    zpad = jnp.zeros((128 - HY_EMB,), F32)
    fvec = jnp.stack([jnp.concatenate([jnp.zeros((1,), F32), fband, fband, zpad]),
                      jnp.concatenate([jnp.zeros((1,), F32), jnp.full((bands,), 0.5 * math.pi, F32),
                                       jnp.full((bands,), math.pi, F32), zpad])])
    w1p = jnp.concatenate([fw1.astype(F32), jnp.zeros((128 - HY_EMB, HY_HID), F32)], axis=0).astype(BF16)
    lo = math.log(HY_SLOW_DECAY) / HY_TARGET
    hi = math.log(HY_FAST_DECAY) / HY_TARGET
    delta = jnp.abs(jnp.linspace(lo, hi, d, dtype=F32)).reshape(1, d)
    n_ord = fw_out.shape[1]
    wo = jnp.transpose(fw_out, (2, 0, 1, 3)).reshape(2, HY_HID, n_ord * d).astype(BF16)
    two = lambda a: jnp.concatenate([a, a], axis=-1)
    zmid = jnp.zeros_like(fw_mid)
    wm_bd = jnp.concatenate([jnp.concatenate([fw_mid, zmid], axis=2), jnp.concatenate([zmid, fw_mid], axis=2)],
                            axis=1).astype(BF16)
    hid = filter_hidden(fvec, w1p, two(fb1.reshape(1, HY_HID)), wm_bd, two(fb_mid.reshape(-1, 1, HY_HID)),
                        two(freq.reshape(1, HY_HID)), seq_len=seq_len, tr=FILTER_ROW_TILE)
    ak, kabs = filter_stage1(tabs["f1"], hid.reshape(FFT_N2, tabs["n1"], HY_HID), wo, jnp.tile(delta, (1, n_ord)),
                             seq_len=seq_len, tn=FFT_COL_TILE)
    ak3 = ak.reshape(tabs["nsp"], 2 * FFT_N2, n_ord * d)
    y_cols = 2
    y = None
    for i in range(n_ord):
        scale = 1.0 / kabs[:, i * d:(i + 1) * d]
        fb = fbias[i].reshape(1, d)
        if y is None:
            y = long_conv_gate(pc, y_cols, pc, i, fb, ak3, i, scale, tabs, seq_len=seq_len, d=d)
        else:
            y = long_conv_gate(y, 0, pc, i, fb, ak3, i, scale, tabs, seq_len=seq_len, d=d)
    y = y.reshape(seq_len, d)
    return mm_res(y, w_out.astype(BF16), b_out, x, g1, tm=ROW_TILE, tn=d)


def kernel(x, c, ctx, c_ctx, mod_w, mod_b, norm_mix_w, norm_ffn_w, ffn_w1, ffn_w3, ffn_w2, ev_w_in, ev_conv_w, ev_conv_b, ev_dt_bias, ev_a_log, ev_d_skip, ev_ssd_norm_w, ev_q_norm_w, ev_w_uq, ev_kv_norm_w, ev_w_ukv, ev_w_o, hy_w_in, hy_b_in, hy_short_w, hy_short_b, hy_fw1, hy_fb1, hy_fw_mid, hy_fb_mid, hy_freq, hy_fw_out, hy_fbias, hy_w_out, hy_b_out, final_norm_w):
    assert x.shape[0] == 1 and mod_w.shape[0] == 2
    xs = x[0]
    xc = ctx[0]
    d = xs.shape[1]
    vecs = jnp.concatenate([c.reshape(1, d), c_ctx.reshape(1, d), jnp.zeros((6, d), F32)], axis=0)
    depth = mod_w.shape[0]
    w1_b, w3_b, w2_b = ffn_w1.astype(BF16), ffn_w3.astype(BF16), ffn_w2.astype(BF16)
    for i in range(depth):
        mods = adaln_vectors(vecs, mod_w, mod_b, i)
        sh2, sc2, g2 = (mods[0:1, j * d:(j + 1) * d] for j in range(3, 6))
        if i % 2 == 0:
            e = i // 2
            xs = _even_layer(xs, xc, mods, norm_mix_w[i], ev_w_in[e], ev_conv_w[e], ev_conv_b[e], ev_dt_bias[e],
                             ev_a_log[e], ev_d_skip[e], ev_ssd_norm_w[e], ev_q_norm_w[e], ev_w_uq[e],
                             ev_kv_norm_w[e], ev_w_ukv[e], ev_w_o[e])
        else:
            o = i // 2
            xs = _odd_layer(xs, mods, norm_mix_w[i], hy_w_in[o], hy_b_in[o], hy_short_w[o], hy_short_b[o],
                            hy_fw1[o], hy_fb1[o], hy_fw_mid[o], hy_fb_mid[o], hy_freq[o], hy_fw_out[o],
                            hy_fbias[o], hy_w_out[o], hy_b_out[o])
        xs = ffn(xs, norm_ffn_w[i], sh2, sc2, g2, w1_b, w3_b, w2_b, final_norm_w, layer=i, tm=ROW_TILE, tf=FFN_COL_TILE,
                 final_norm=(i == depth - 1))
    return xs[None]
```

```python
import functools
import math

import jax
import jax.numpy as jnp
from jax import lax
from jax.experimental import pallas as pl
from jax.experimental.pallas import tpu as pltpu

F32 = jnp.float32
BF16 = jnp.bfloat16

EPS = 1e-6
GRID_W = 64
CTX_LEN = 256
SSD_HEADS = 16
SSD_HEAD_DIM = 64
SSD_INNER = 1024
SSD_GROUPS = 2
SSD_HPG = 8
SSD_STATE = 128
SSD_CHUNK = 256
SSD_XBC = 1536
SSD_DT = 32
MLA_HEADS = 8
MLA_NOPE = 128
MLA_ROPE = 64
MLA_V = 128
MLA_Q_RANK = 512
MLA_KV_RANK = 512
ROPE_THETA = 10000.0
MLA_HEAD_PAD = 256
Q_SIDE = SSD_INNER + MLA_Q_RANK
HY_EMB = 33
HY_HID = 64
HY_FAST_DECAY = 0.3
HY_SLOW_DECAY = 1.5
HY_TARGET = 1e-2
FFT_N2 = 128
SLAB_TILE = 8

P_Z, P_CQ, P_XBC, P_CKV, P_KR, P_DT, P_COLS = 0, 1024, 1536, 3072, 3584, 3712, 3840

VMEM_LIMIT = 56 * 1024 * 1024

ROW_TILE = 512
WIDE_ROW_TILES = (1024, 512)
PROJ_COL_TILE = 768
FFN_COL_TILE = 512
MERGE_ROW_TILE = 256
ATTN_Q_TILES = (2048, 1024, 512, 256)
ATTN_K_TILES = (1280, 1024, 768, 640, 512, 384, 256, 128)
FFT_COL_TILE = 256
FFT_MID_COL_TILE = 512
FILTER_ROW_TILE = 1024


def _cparams(*sem):
    return pltpu.CompilerParams(dimension_semantics=sem, vmem_limit_bytes=VMEM_LIMIT)


def _pick(n, cands):
    for c in cands:
        if n % c == 0:
            return c
    raise ValueError(f"no tile for {n}")


def _split(a):
    hi = a.astype(BF16)
    lo = (a - hi.astype(F32)).astype(BF16)
    return hi, lo


def _dot(a, b):
    return jnp.dot(a, b, preferred_element_type=F32)


def _silu(x):
    return x * (1.0 / (1.0 + jnp.exp(-x)))


def _modnorm(x, nw, sh, sc):
    ms = jnp.mean(x * x, axis=-1, keepdims=True)
    return (x * lax.rsqrt(ms + EPS) * nw) * (1.0 + sc) + sh


def _matvec_kernel(x_ref, w_ref, b_ref, o_ref):
    x = x_ref[...]
    o_ref[...] = _dot(_silu(x).astype(BF16), w_ref[...].astype(BF16)) + b_ref[...]


def adaln_vectors(vecs, w_all, b_all, layer):
    nl, k, n = w_all.shape
    tn = 1024
    return pl.pallas_call(
        _matvec_kernel,
        grid=(n // tn,),
        in_specs=[pl.BlockSpec((8, k), lambda j: (0, 0)),
                  pl.BlockSpec((None, k, tn), lambda j: (layer, 0, j)),
                  pl.BlockSpec((None, 1, tn), lambda j: (layer, 0, j))],
        out_specs=pl.BlockSpec((8, tn), lambda j: (0, j)),
        out_shape=jax.ShapeDtypeStruct((8, n), F32),
        compiler_params=_cparams("arbitrary"),
        name="adaln_vectors",
    )(vecs, w_all, b_all.reshape(nl, 1, n))


def _normmm_kernel(x_ref, nw_ref, sh_ref, sc_ref, w_ref, b_ref, o_ref, xn_ref):
    @pl.when(pl.program_id(1) == 0)
    def _():
        xn_ref[...] = _modnorm(x_ref[...].astype(F32), nw_ref[...], sh_ref[...], sc_ref[...]).astype(BF16)

    o_ref[...] = (_dot(xn_ref[...], w_ref[...]) + b_ref[...]).astype(o_ref.dtype)


def normmm(x, nw, sh, sc, w, b, *, tm, tn, out_dtype, x_col=0):
    m = x.shape[0]
    k, n = w.shape
    return pl.pallas_call(
        _normmm_kernel,
        grid=(m // tm, n // tn),
        in_specs=[pl.BlockSpec((tm, k), lambda i, j: (i, x_col)),
                  pl.BlockSpec((1, k), lambda i, j: (0, 0)),
                  pl.BlockSpec((1, k), lambda i, j: (0, 0)),
                  pl.BlockSpec((1, k), lambda i, j: (0, 0)),
                  pl.BlockSpec((k, tn), lambda i, j: (0, j)),
                  pl.BlockSpec((1, tn), lambda i, j: (0, j))],
        out_specs=pl.BlockSpec((tm, tn), lambda i, j: (i, j)),
        out_shape=jax.ShapeDtypeStruct((m, n), out_dtype),
        scratch_shapes=[pltpu.VMEM((tm, k), BF16)],
        compiler_params=_cparams("parallel", "arbitrary"),
        name="normmm",
    )(x, nw.reshape(1, k), sh.reshape(1, k), sc.reshape(1, k), w, b.reshape(1, n))


def _normmm_conv_kernel(x_ref, xprev_ref, xnext_ref, nw_ref, sh_ref, sc_ref, w_ref, b_ref, cw_ref, cb_ref,
                        o_ref, xn_ref, xh_ref):
    i = pl.program_id(0)

    @pl.when(pl.program_id(1) == 0)
    def _():
        nw, sh, sc = nw_ref[...], sh_ref[...], sc_ref[...]
        xn_ref[...] = _modnorm(x_ref[...], nw, sh, sc).astype(BF16)
        xh_ref[0:8, :] = _modnorm(xprev_ref[...], nw, sh, sc).astype(BF16)
        xh_ref[8:16, :] = _modnorm(xnext_ref[...], nw, sh, sc).astype(BF16)

    w = w_ref[...]
    b = b_ref[...]
    y = _dot(xn_ref[...], w) + b
    yh = _dot(xh_ref[...], w) + b
    prev_row = jnp.where(i == 0, 0.0, yh[7:8, :])
    next_row = jnp.where(i == pl.num_programs(0) - 1, 0.0, yh[8:9, :])
    yp, yn = _shift_rows(y, prev_row, next_row)
    o_ref[...] = yp * cw_ref[0:1, :] + y * cw_ref[1:2, :] + yn * cw_ref[2:3, :] + cb_ref[...]


def normmm_conv(x, nw, sh, sc, w, b, cw, cb, *, tm, tn):
    m, k = x.shape
    n = w.shape[1]
    nrb = m // 8
    tb = tm // 8
    vec = pl.BlockSpec((1, k), lambda i, j: (0, 0))
    return pl.pallas_call(
        _normmm_conv_kernel,
        grid=(m // tm, n // tn),
        in_specs=[pl.BlockSpec((tm, k), lambda i, j: (i, 0)),
                  pl.BlockSpec((8, k), lambda i, j: (jnp.maximum(i * tb - 1, 0), 0)),
                  pl.BlockSpec((8, k), lambda i, j: (jnp.minimum((i + 1) * tb, nrb - 1), 0)),
                  vec, vec, vec,
                  pl.BlockSpec((k, tn), lambda i, j: (0, j)),
                  pl.BlockSpec((1, tn), lambda i, j: (0, j)),
                  pl.BlockSpec((3, tn), lambda i, j: (0, j)),
                  pl.BlockSpec((1, tn), lambda i, j: (0, j))],
        out_specs=pl.BlockSpec((tm, tn), lambda i, j: (i, j)),
        out_shape=jax.ShapeDtypeStruct((m, n), F32),
        scratch_shapes=[pltpu.VMEM((tm, k), BF16), pltpu.VMEM((16, k), BF16)],
        compiler_params=_cparams("parallel", "arbitrary"),
        name="normmm_conv",
    )(x, x, x, nw.reshape(1, k), sh.reshape(1, k), sc.reshape(1, k), w, b.reshape(1, n), cw, cb.reshape(1, n))


def _qproj_kernel(x_ref, nw_ref, wa_ref, wb_ref, ta_ref, tb_ref, o_ref):
    x = x_ref[...]
    ms = jnp.mean(x * x, axis=-1, keepdims=True)
    xn = (x * lax.rsqrt(ms + EPS) * nw_ref[...]).astype(BF16)
    ta = ta_ref[...]
    tb = tb_ref[...]
    hp = MLA_HEAD_PAD
    for h in range(MLA_HEADS):
        cols = slice(h * hp, (h + 1) * hp)
        o_ref[:, cols] = (_dot(xn, wa_ref[:, cols]) * ta + _dot(xn, wb_ref[:, cols]) * tb).astype(o_ref.dtype)


def qproj(p, nw, wa, wb, ta, tb, *, tm):
    m = ta.shape[0]
    k = MLA_Q_RANK
    hp = MLA_HEAD_PAD
    n = MLA_HEADS * hp
    return pl.pallas_call(
        _qproj_kernel,
        grid=(m // tm,),
        in_specs=[pl.BlockSpec((tm, k), lambda i: (i, P_CQ // MLA_Q_RANK)),
                  pl.BlockSpec((1, k), lambda i: (0, 0)),
                  pl.BlockSpec((k, n), lambda i: (0, 0)),
                  pl.BlockSpec((k, n), lambda i: (0, 0)),
                  pl.BlockSpec((tm, hp), lambda i: (i, 0)),
                  pl.BlockSpec((tm, hp), lambda i: (i, 0))],
        out_specs=pl.BlockSpec((tm, n), lambda i: (i, 0)),
        out_shape=jax.ShapeDtypeStruct((m, n), BF16),
        compiler_params=_cparams("parallel"),
        name="mla_qproj",
    )(p, nw.reshape(1, k), wa, wb, ta, tb)


def _kv_kernel(cl_ref, cc_ref, rl_ref, rc_ref, tc_ref, ts_ref, nw_ref, w_ref, k_ref, v_ref, *, n_lat_tiles):
    is_ctx = pl.program_id(0) >= n_lat_tiles
    x = jnp.where(is_ctx, cc_ref[...], cl_ref[...])
    ms = jnp.mean(x * x, axis=-1, keepdims=True)
    kv = _dot((x * lax.rsqrt(ms + EPS) * nw_ref[...]).astype(BF16), w_ref[...]).astype(BF16)
    krr = jnp.where(is_ctx, rc_ref[...], rl_ref[...])
    kr = krr[:, :MLA_ROPE]
    kr_rot = kr * tc_ref[...][:, :MLA_ROPE] + krr[:, MLA_ROPE:] * ts_ref[...][:, :MLA_ROPE]
    kr = jnp.where(is_ctx, kr, kr_rot)
    tail = jnp.concatenate([kr, jnp.zeros_like(kr)], axis=-1).astype(BF16)
    for h in range(MLA_HEADS):
        base = h * (MLA_NOPE + MLA_V)
        k_ref[:, h * MLA_HEAD_PAD:h * MLA_HEAD_PAD + MLA_NOPE] = kv[:, base:base + MLA_NOPE]
        k_ref[:, h * MLA_HEAD_PAD + MLA_NOPE:(h + 1) * MLA_HEAD_PAD] = tail
        v_ref[:, h * MLA_V:(h + 1) * MLA_V] = kv[:, base + MLA_NOPE:base + MLA_NOPE + MLA_V]


def kv_project(p_lat, p_ctx, tc, ts, nw, w):
    tm = CTX_LEN
    nl = p_lat.shape[0] // tm
    rows = p_lat.shape[0] + p_ctx.shape[0]
    k = MLA_KV_RANK
    lat = lambda i: jnp.minimum(i, nl - 1)
    return pl.pallas_call(
        functools.partial(_kv_kernel, n_lat_tiles=nl),
        grid=(rows // tm,),
        in_specs=[pl.BlockSpec((tm, k), lambda i: (lat(i), P_CKV // MLA_KV_RANK)),
                  pl.BlockSpec((tm, k), lambda i: (0, P_CKV // MLA_KV_RANK)),
                  pl.BlockSpec((tm, 128), lambda i: (lat(i), P_KR // 128)),
                  pl.BlockSpec((tm, 128), lambda i: (0, P_KR // 128)),
                  pl.BlockSpec((tm, 128), lambda i: (lat(i), 0)),
                  pl.BlockSpec((tm, 128), lambda i: (lat(i), 0)),
                  pl.BlockSpec((1, k), lambda i: (0, 0)),
                  pl.BlockSpec(w.shape, lambda i: (0, 0))],
        out_specs=[pl.BlockSpec((tm, MLA_HEADS * MLA_HEAD_PAD), lambda i: (i, 0)),
                   pl.BlockSpec((tm, MLA_HEADS * MLA_V), lambda i: (i, 0))],
        out_shape=[jax.ShapeDtypeStruct((rows, MLA_HEADS * MLA_HEAD_PAD), BF16),
                   jax.ShapeDtypeStruct((rows, MLA_HEADS * MLA_V), BF16)],
        compiler_params=_cparams("parallel"),
        name="mla_kv_project",
    )(p_lat, p_ctx, p_lat, p_ctx, tc, ts, nw.reshape(1, k), w)


def _attn_kernel(q_ref, k_ref, v_ref, o_ref, s0_ref, s1_ref, m_ref, acc_ref, *, nk):
    j = pl.program_id(2)

    def qk(s_ref):
        s_ref[...] = lax.dot_general(q_ref[...], k_ref[...], (((1,), (1,)), ((), ())), preferred_element_type=F32)

    def softmax_pv(s_ref):
        s = s_ref[...]
        m_prev = m_ref[...]
        m_new = jnp.maximum(m_prev, jnp.max(s, axis=-1, keepdims=True))
        alpha = jnp.exp2(m_prev - m_new)
        p = jnp.exp2(s - m_new).astype(BF16)
        v = v_ref[...]
        lane = lax.broadcasted_iota(jnp.int32, v.shape, 1)
        ones_col = jnp.where(lane == 0, 1.0, 0.0).astype(BF16)
        v_ext = jnp.concatenate([v, ones_col], axis=1)
        acc_ref[...] = alpha * acc_ref[...] + _dot(p, v_ext)
        m_ref[...] = m_new

    @pl.when(j == 0)
    def _():
        m_ref[...] = jnp.full(m_ref.shape, -jnp.inf, F32)
        acc_ref[...] = jnp.zeros(acc_ref.shape, F32)
        qk(s0_ref)

    mid = jnp.logical_and(j > 0, j < nk)

    @pl.when(jnp.logical_and(mid, j % 2 == 1))
    def _():
        softmax_pv(s0_ref)
        qk(s1_ref)

    @pl.when(jnp.logical_and(mid, j % 2 == 0))
    def _():
        softmax_pv(s1_ref)
        qk(s0_ref)

    @pl.when(j == nk)
    def _():
        softmax_pv(s1_ref if (nk - 1) % 2 else s0_ref)
        acc = acc_ref[...]
        o_ref[...] = (acc[:, :MLA_V] / acc[:, MLA_V:MLA_V + 1]).astype(o_ref.dtype)


def attention(q, k, v, *, tq, tk):
    lq = q.shape[0]
    s = k.shape[0]
    nk = s // tk
    return pl.pallas_call(
        functools.partial(_attn_kernel, nk=nk),
        grid=(MLA_HEADS, lq // tq, nk + 1),
        in_specs=[pl.BlockSpec((tq, MLA_HEAD_PAD), lambda h, i, j: (i, h)),
                  pl.BlockSpec((tk, MLA_HEAD_PAD), lambda h, i, j: (jnp.minimum(j, nk - 1), h)),
                  pl.BlockSpec((tk, MLA_V), lambda h, i, j: (jnp.maximum(j - 1, 0), h))],
        out_specs=pl.BlockSpec((tq, MLA_V), lambda h, i, j: (i, h)),
        out_shape=jax.ShapeDtypeStruct((lq, MLA_HEADS * MLA_V), BF16),
        scratch_shapes=[pltpu.VMEM((tq, tk), F32), pltpu.VMEM((tq, tk), F32), pltpu.VMEM((tq, 1), F32),
                        pltpu.VMEM((tq, 2 * MLA_V), F32)],
        compiler_params=_cparams("parallel", "parallel", "arbitrary"),
        name="mla_attention",
    )(q, k, v)


def _shift_rows(x, prev_row, next_row):
    tm = x.shape[0]
    rows = lax.broadcasted_iota(jnp.int32, x.shape, 0)
    xp = jnp.where(rows == 0, prev_row, pltpu.roll(x, 1, 0))
    xn = jnp.where(rows == tm - 1, next_row, pltpu.roll(x, tm - 1, 0))
    return xp, xn


def _dwconv_kernel(x_ref, xprev_ref, xnext_ref, w_ref, b_ref, o_ref, *, act):
    i = pl.program_id(0)
    x = x_ref[...]
    prev_row = jnp.where(i == 0, 0.0, xprev_ref[7:8, :])
    next_row = jnp.where(i == pl.num_programs(0) - 1, 0.0, xnext_ref[0:1, :])
    xp, xn = _shift_rows(x, prev_row, next_row)
    y = xp * w_ref[0:1, :] + x * w_ref[1:2, :] + xn * w_ref[2:3, :] + b_ref[...]
    if act:
        y = _silu(y)
    o_ref[...] = y


def dwconv3(x, w, b, *, tm, tc, col0, width, act):
    m = x.shape[0]
    cb = col0 // tc
    nrb = m // 8
    tb = tm // 8
    return pl.pallas_call(
        functools.partial(_dwconv_kernel, act=act),
        grid=(m // tm, width // tc),
        in_specs=[pl.BlockSpec((tm, tc), lambda i, j: (i, cb + j)),
                  pl.BlockSpec((8, tc), lambda i, j: (jnp.maximum(i * tb - 1, 0), cb + j)),
                  pl.BlockSpec((8, tc), lambda i, j: (jnp.minimum((i + 1) * tb, nrb - 1), cb + j)),
                  pl.BlockSpec((3, tc), lambda i, j: (0, j)),
                  pl.BlockSpec((1, tc), lambda i, j: (0, j))],
        out_specs=pl.BlockSpec((tm, tc), lambda i, j: (i, j)),
        out_shape=jax.ShapeDtypeStruct((m, width), F32),
        compiler_params=_cparams("arbitrary", "arbitrary"),
        name="dwconv3",
    )(x, x, x, w, b.reshape(1, width))


def _softplus(x):
    return jnp.maximum(x, 0.0) + jnp.log(1.0 + jnp.exp(-jnp.abs(x)))


def _ssd_prepare(d, xbc, dt_ref, dtt_ref, bias_ref, biast_ref, a_ref, at_ref, tri_ref):
    q = SSD_CHUNK
    dt = _softplus(dt_ref[0] + bias_ref[d])
    dtt = _softplus(dtt_ref[0, 0] + biast_ref[d])
    dta_hi, dta_lo = _split(dt * a_ref[d])
    dtat_hi, dtat_lo = _split(dtt * at_ref[d])
    tri = tri_ref[d]
    trit = tri_ref[1 - d]
    acum = _dot(tri, dta_hi) + _dot(tri, dta_lo)
    acumt = _dot(dtat_hi, trit) + _dot(dtat_lo, trit)
    total = acum[q - 1:q, :] if d == 0 else acum[0:1, :]
    return dict(xbc=xbc, dt=dt, acum=acum, acumt=acumt, to_end=jnp.exp(total - acum),
                from_start=jnp.exp(acum), chunk_decay=jnp.exp(total), mask=tri > 0.5)


def _per_head_lanes(a, g):
    return jnp.concatenate([jnp.broadcast_to(a[:, h:h + 1], (a.shape[0], SSD_HEAD_DIM))
                            for h in range(g * SSD_HPG, (g + 1) * SSD_HPG)], axis=1)


def _ssd_group(d, p, g, h_ref):
    xbc = p["xbc"]
    gw = SSD_HPG * SSD_HEAD_DIM
    bm = xbc[:, SSD_INNER + g * SSD_STATE:SSD_INNER + (g + 1) * SSD_STATE]
    cm = xbc[:, SSD_INNER + (SSD_GROUPS + g) * SSD_STATE:SSD_INNER + (SSD_GROUPS + g + 1) * SSD_STATE]
    bm_b = bm.astype(BF16)
    cm_b = cm.astype(BF16)
    cb = lax.dot_general(cm_b, bm_b, (((1,), (1,)), ((), ())), preferred_element_type=F32)
    xdt = xbc[:, g * gw:(g + 1) * gw] * _per_head_lanes(p["dt"], g)
    h_prev = h_ref[d * SSD_GROUPS + g]
    y_off = _dot(cm_b, h_prev.astype(BF16)) * _per_head_lanes(p["from_start"], g)
    states = _dot(bm.T.astype(BF16), (xdt * _per_head_lanes(p["to_end"], g)).astype(BF16))
    h_ref[d * SSD_GROUPS + g] = h_prev * _per_head_lanes(p["chunk_decay"], g) + states
    return dict(cb=cb, xdt_b=xdt.astype(BF16), y_off=y_off)


def _ssd_head_pair(p, grp, g, rp, o_ref):
    ys = []
    for r in (2 * rp, 2 * rp + 1):
        h = g * SSD_HPG + r
        seg = p["acum"][:, h:h + 1] - p["acumt"][h:h + 1, :]
        decay = jnp.exp(jnp.where(p["mask"], seg, -jnp.inf))
        ys.append(_dot((grp["cb"] * decay).astype(BF16), grp["xdt_b"][:, r * SSD_HEAD_DIM:(r + 1) * SSD_HEAD_DIM]))
    w = 2 * SSD_HEAD_DIM
    col = (g * SSD_HPG + 2 * rp) * SSD_HEAD_DIM
    o_ref[:, col:col + w] = jnp.concatenate(ys, axis=1) + grp["y_off"][:, rp * w:(rp + 1) * w]


def _ssd_kernel(xf_ref, xfc_ref, xb_ref, xbc_ref, dtf_ref, dtb_ref, dttf_ref, dttb_ref, bias_ref, biast_ref, a_ref, at_ref, tri_ref,
                of_ref, ob_ref, h_ref):
    @pl.when(pl.program_id(0) == 0)
    def _():
        h_ref[...] = jnp.zeros(h_ref.shape, F32)

    s = pl.program_id(0)
    ncx = CTX_LEN // SSD_CHUNK
    x_fwd = jnp.where(s < ncx, xfc_ref[...], xf_ref[...])
    x_bwd = jnp.where(s < ncx, xbc_ref[...], xb_ref[...])
    prm = (_ssd_prepare(0, x_fwd, dtf_ref, dttf_ref, bias_ref, biast_ref, a_ref, at_ref, tri_ref),
           _ssd_prepare(1, x_bwd, dtb_ref, dttb_ref, bias_ref, biast_ref, a_ref, at_ref, tri_ref))
    outs = (of_ref, ob_ref)
    for g in range(SSD_GROUPS):
        grp = (_ssd_group(0, prm[0], g, h_ref), _ssd_group(1, prm[1], g, h_ref))
        for rp in range(SSD_HPG // 2):
            for d in range(2):
                _ssd_head_pair(prm[d], grp[d], g, rp, outs[d])


def ssd_scan(xbc_lat, xbc_ctx, dt2, dt2t, bias2, bias2t, a2, a2t, tri2, *, n_lat_chunks):
    nc = n_lat_chunks
    ncx = CTX_LEN // SSD_CHUNK
    tot = nc + ncx
    q = SSD_CHUNK

    def cf(s):
        return lax.rem(s + nc, tot)

    def cbk(s):
        return tot - 1 - s

    def full(shape):
        return pl.BlockSpec(shape, lambda s: (0,) * len(shape))

    return pl.pallas_call(
        _ssd_kernel,
        grid=(tot,),
        in_specs=[pl.BlockSpec((q, SSD_XBC), lambda s: (jnp.minimum(cf(s), nc - 1), 0)),
                  pl.BlockSpec((q, SSD_XBC), lambda s: (jnp.maximum(cf(s) - nc, 0), 0)),
                  pl.BlockSpec((q, SSD_XBC), lambda s: (jnp.minimum(cbk(s), nc - 1), 0)),
                  pl.BlockSpec((q, SSD_XBC), lambda s: (jnp.maximum(cbk(s) - nc, 0), 0)),
                  pl.BlockSpec((1, q, SSD_HEADS), lambda s: (0, cf(s), 0)),
                  pl.BlockSpec((1, q, SSD_HEADS), lambda s: (1, cbk(s), 0)),
                  pl.BlockSpec((1, 1, SSD_HEADS, q), lambda s: (0, cf(s), 0, 0)),
                  pl.BlockSpec((1, 1, SSD_HEADS, q), lambda s: (1, cbk(s), 0, 0)),
                  full((2, 1, SSD_HEADS)), full((2, SSD_HEADS, 1)), full((2, 1, SSD_HEADS)),
                  full((2, SSD_HEADS, 1)), full((2, q, q))],
        out_specs=[pl.BlockSpec((q, SSD_INNER), lambda s: (jnp.where(cf(s) >= nc, 0, cf(s)), 0)),
                   pl.BlockSpec((q, SSD_INNER), lambda s: (jnp.where(cbk(s) >= nc, nc - 1, cbk(s)), 0))],
        out_shape=[jax.ShapeDtypeStruct((nc * q, SSD_INNER), F32), jax.ShapeDtypeStruct((nc * q, SSD_INNER), F32)],
        scratch_shapes=[pltpu.VMEM((2 * SSD_GROUPS, SSD_STATE, SSD_HPG * SSD_HEAD_DIM), F32)],
        compiler_params=_cparams("arbitrary"),
        name="ssd_scan",
    )(xbc_lat, xbc_ctx, xbc_lat, xbc_ctx, dt2, dt2, dt2t, dt2t, bias2, bias2t, a2, a2t, tri2)


def _merge_kernel(o_ref, yf_ref, yb_ref, xs_ref, z_ref, dsk_ref, nw_ref, w_ref, x_ref, g_ref, out_ref, a_ref):
    @pl.when(pl.program_id(1) == 0)
    def _():
        a_ref[:, :MLA_HEADS * MLA_V] = o_ref[...]
        y = yf_ref[...] + yb_ref[...] + dsk_ref[...] * xs_ref[...]
        gy = y * _silu(z_ref[...])
        gw = SSD_INNER // SSD_GROUPS
        for g in range(SSD_GROUPS):
            part = gy[:, g * gw:(g + 1) * gw]
            ms = jnp.mean(part * part, axis=-1, keepdims=True)
            a_ref[:, MLA_HEADS * MLA_V + g * gw:MLA_HEADS * MLA_V + (g + 1) * gw] = (
                part * lax.rsqrt(ms + EPS) * nw_ref[:, g * gw:(g + 1) * gw]).astype(BF16)

    out_ref[...] = x_ref[...] + g_ref[...] * _dot(a_ref[...], w_ref[...])


def merge_out(o_att, yf, yb, xbc, p, dsk, nw, w_o, x, gate, *, tm, tn):
    m, n = x.shape
    kw = w_o.shape[0]
    return pl.pallas_call(
        _merge_kernel,
        grid=(m // tm, n // tn),
        in_specs=[pl.BlockSpec((tm, MLA_HEADS * MLA_V), lambda i, j: (i, 0)),
                  pl.BlockSpec((tm, SSD_INNER), lambda i, j: (i, 0)),
                  pl.BlockSpec((tm, SSD_INNER), lambda i, j: (i, 0)),
                  pl.BlockSpec((tm, SSD_INNER), lambda i, j: (i, 0)),
                  pl.BlockSpec((tm, SSD_INNER), lambda i, j: (i, 0)),
                  pl.BlockSpec((1, SSD_INNER), lambda i, j: (0, 0)),
                  pl.BlockSpec((1, SSD_INNER), lambda i, j: (0, 0)),
                  pl.BlockSpec((kw, tn), lambda i, j: (0, j)),
                  pl.BlockSpec((tm, tn), lambda i, j: (i, j)),
                  pl.BlockSpec((1, tn), lambda i, j: (0, j))],
        out_specs=pl.BlockSpec((tm, tn), lambda i, j: (i, j)),
        out_shape=jax.ShapeDtypeStruct((m, n), F32),
        scratch_shapes=[pltpu.VMEM((tm, kw), BF16)],
        compiler_params=_cparams("parallel", "arbitrary"),
        name="merge_out",
    )(o_att, yf, yb, xbc, p, dsk, nw, w_o, x, gate)


def _mmres_kernel(a_ref, w_ref, b_ref, x_ref, g_ref, o_ref):
    o_ref[...] = x_ref[...] + g_ref[...] * (_dot(a_ref[...].astype(BF16), w_ref[...]) + b_ref[...])


def mm_res(a, w, b, x, gate, *, tm, tn):
    m, k = a.shape
    n = w.shape[1]
    return pl.pallas_call(
        _mmres_kernel,
        grid=(m // tm, n // tn),
        in_specs=[pl.BlockSpec((tm, k), lambda i, j: (i, 0)),
                  pl.BlockSpec((k, tn), lambda i, j: (0, j)),
                  pl.BlockSpec((1, tn), lambda i, j: (0, j)),
                  pl.BlockSpec((tm, tn), lambda i, j: (i, j)),
                  pl.BlockSpec((1, tn), lambda i, j: (0, j))],
        out_specs=pl.BlockSpec((tm, tn), lambda i, j: (i, j)),
        out_shape=jax.ShapeDtypeStruct((m, n), F32),
        compiler_params=_cparams("parallel", "arbitrary"),
        name="mm_res",
    )(a, w, b.reshape(1, n), x, gate)


def _ffn_kernel(x_ref, nw_ref, sh_ref, sc_ref, g_ref, w1_ref, w3_ref, w2_ref, fw_ref, o_ref, xn_ref, acc_ref,
                *, final_norm):
    f = pl.program_id(1)

    @pl.when(f == 0)
    def _():
        xn_ref[...] = _modnorm(x_ref[...], nw_ref[...], sh_ref[...], sc_ref[...]).astype(BF16)
        acc_ref[...] = jnp.zeros(acc_ref.shape, F32)

    xn = xn_ref[...]
    a = _dot(xn, w1_ref[...])
    b = _dot(xn, w3_ref[...])
    acc_ref[...] += _dot((_silu(a) * b).astype(BF16), w2_ref[...])

    @pl.when(f == pl.num_programs(1) - 1)
    def _():
        y = x_ref[...] + g_ref[...] * acc_ref[...]
        if final_norm:
            ms = jnp.mean(y * y, axis=-1, keepdims=True)
            y = y * lax.rsqrt(ms + EPS) * fw_ref[...]
        o_ref[...] = y


def ffn(x, nw, sh, sc, gate, w1, w3, w2, fw, *, layer, tm, tf, final_norm):
    m, dm = x.shape
    dff = w1.shape[2]
    vec = pl.BlockSpec((1, dm), lambda i, f: (0, 0))
    return pl.pallas_call(
        functools.partial(_ffn_kernel, final_norm=final_norm),
        grid=(m // tm, dff // tf),
        in_specs=[pl.BlockSpec((tm, dm), lambda i, f: (i, 0)), vec, vec, vec, vec,
                  pl.BlockSpec((None, dm, tf), lambda i, f: (layer, 0, f)),
                  pl.BlockSpec((None, dm, tf), lambda i, f: (layer, 0, f)),
                  pl.BlockSpec((None, tf, dm), lambda i, f: (layer, f, 0)),
                  vec],
        out_specs=pl.BlockSpec((tm, dm), lambda i, f: (i, 0)),
        out_shape=jax.ShapeDtypeStruct((m, dm), F32),
        scratch_shapes=[pltpu.VMEM((tm, dm), BF16), pltpu.VMEM((tm, dm), F32)],
        compiler_params=_cparams("parallel", "arbitrary"),
        name="ffn",
    )(x, nw.reshape(1, dm), sh, sc, gate, w1, w3, w2, fw.reshape(1, dm))


def _filter_hidden_kernel(fvec_ref, w1_ref, b1_ref, wm_ref, bm_ref, freq_ref, h_ref, *, seq_len, tr):
    i = pl.program_id(0)
    n1_total = 2 * seq_len // FFT_N2
    shift = n1_total.bit_length() - 1
    assert 1 << shift == n1_total
    p = i * tr + lax.broadcasted_iota(jnp.int32, (tr, 1), 0)
    n = ((p & (n1_total - 1)) << (FFT_N2.bit_length() - 1)) + (p >> shift)
    lag = jnp.where(n < seq_len, n, 2 * seq_len - n).astype(F32)
    t = lag / float(seq_len - 1)
    ang = lag * (2.0 * math.pi / seq_len)
    lane = lax.broadcasted_iota(jnp.int32, (tr, 128), 1)
    feat = jnp.sin(ang * fvec_ref[0:1, :] + fvec_ref[1:2, :])
    emb = jnp.where(lane == 0, t, jnp.where(lane < HY_EMB, feat, 0.0))
    fr = freq_ref[...]
    h2 = tr // 2
    e = emb.astype(BF16)
    z = jnp.concatenate([_dot(e[:h2], w1_ref[...]), _dot(e[h2:], w1_ref[...])], axis=1)
    hid = jnp.sin(fr * (z + b1_ref[...]))
    for j in range(wm_ref.shape[0]):
        hid = jnp.sin(fr * (_dot(hid.astype(BF16), wm_ref[j]) + bm_ref[j]))
    h_ref[0:h2, :] = hid[:, :HY_HID]
    h_ref[h2:tr, :] = hid[:, HY_HID:]


def filter_hidden(fvec, w1, b1, wm, bm, freq, *, seq_len, tr):
    full = lambda shp: pl.BlockSpec(shp, lambda i: (0,) * len(shp))
    return pl.pallas_call(
        functools.partial(_filter_hidden_kernel, seq_len=seq_len, tr=tr),
        grid=(2 * seq_len // tr,),
        in_specs=[full((2, 128)), full(w1.shape), full(b1.shape), full(wm.shape), full(bm.shape), full(freq.shape)],
        out_specs=pl.BlockSpec((tr, HY_HID), lambda i: (i, 0)),
        out_shape=jax.ShapeDtypeStruct((2 * seq_len, HY_HID), F32),
        compiler_params=_cparams("parallel"),
        name="hyena_filter_hidden",
    )(fvec, w1, b1, wm, bm, freq)


ROW_GROUP = 16


def _filter_stage1_kernel(f_ref, hid_ref, wo_ref, delta_ref, o_ref, abs_ref, os_ref, *, seq_len):
    i = pl.program_id(1)
    n1 = hid_ref.shape[1]
    half = n1 // 2
    n1_idx = lax.broadcasted_iota(jnp.int32, (n1, 1), 0)
    rate = delta_ref[...] * (1.0 / float(seq_len - 1))
    f = f_ref[...]
    wo_past, wo_future = wo_ref[0], wo_ref[1]

    @pl.when(i == 0)
    def _():
        abs_ref[...] = jnp.zeros(abs_ref.shape, F32)

    lag_block = jnp.where(n1_idx < half, n1_idx * FFT_N2, 2 * seq_len - n1_idx * FFT_N2).astype(F32)
    win_block = jnp.exp(-lag_block * rate)
    asum = jnp.zeros(abs_ref.shape, F32)
    for r in range(ROW_GROUP):
        hb = hid_ref[r].astype(BF16)
        step = (i * ROW_GROUP + r).astype(F32) * rate
        k = jnp.concatenate([_dot(hb[:half], wo_past) * jnp.exp(-step),
                             _dot(hb[half:], wo_future) * jnp.exp(step)], axis=0) * win_block
        if r == 0:
            k = jnp.where(jnp.logical_and(n1_idx == half, i == 0), 0.0, k)
        asum = asum + jnp.sum(jnp.abs(k), axis=0, keepdims=True)
        os_ref[r] = _dot(f, k.astype(BF16))
    abs_ref[...] += asum
    o_ref[...] = jnp.swapaxes(os_ref[...], 0, 1).astype(o_ref.dtype)


def filter_stage1(f1, hid3, wo, delta, *, seq_len, tn):
    mrows, n1 = f1.shape
    dd = delta.shape[1]
    return pl.pallas_call(
        functools.partial(_filter_stage1_kernel, seq_len=seq_len),
        grid=(dd // tn, FFT_N2 // ROW_GROUP),
        in_specs=[pl.BlockSpec((mrows, n1), lambda j, i: (0, 0)),
                  pl.BlockSpec((ROW_GROUP, n1, HY_HID), lambda j, i: (i, 0, 0)),
                  pl.BlockSpec((2, HY_HID, tn), lambda j, i: (0, 0, j)),
                  pl.BlockSpec((1, tn), lambda j, i: (0, j))],
        out_specs=[pl.BlockSpec((mrows, ROW_GROUP, tn), lambda j, i: (0, i, j)),
                   pl.BlockSpec((1, tn), lambda j, i: (0, j))],
        out_shape=[jax.ShapeDtypeStruct((mrows, FFT_N2, dd), BF16), jax.ShapeDtypeStruct((1, dd), F32)],
        scratch_shapes=[pltpu.VMEM((ROW_GROUP, mrows, tn), F32)],
        compiler_params=_cparams("parallel", "arbitrary"),
        name="filter_stage1",
    )(f1, hid3, wo, delta)


def _fft1_kernel(f_ref, x_ref, o_ref, os_ref):
    xt = jnp.swapaxes(x_ref[...], 0, 1)
    for r in range(ROW_GROUP):
        os_ref[r] = _dot(f_ref[...], xt[r].astype(BF16))
    o_ref[...] = jnp.swapaxes(os_ref[...], 0, 1).astype(o_ref.dtype)


def _fft1_inv_kernel(g_ref, b_ref, v_ref, gate_ref, fb_ref, o_ref, os_ref):
    bt = jnp.swapaxes(b_ref[...].astype(F32), 0, 1)
    for r in range(ROW_GROUP):
        os_ref[r] = _dot(g_ref[...], bt[r].astype(BF16))
    conv = jnp.swapaxes(os_ref[...], 0, 1)
    o_ref[...] = gate_ref[...] * (conv + fb_ref[...] * v_ref[...])


def fft_stage1(f1, x3, *, d, tn, col_ofs=0):
    mrows, k = f1.shape
    n2 = x3.shape[1]
    nd = d // tn
    return pl.pallas_call(
        _fft1_kernel,
        grid=(n2 // ROW_GROUP, nd),
        in_specs=[pl.BlockSpec((mrows, k), lambda i, j: (0, 0)),
                  pl.BlockSpec((k, ROW_GROUP, tn), lambda i, j: (0, i, col_ofs * nd + j))],
        out_specs=pl.BlockSpec((mrows, ROW_GROUP, tn), lambda i, j: (0, i, j)),
        out_shape=jax.ShapeDtypeStruct((mrows, n2, d), BF16),
        scratch_shapes=[pltpu.VMEM((ROW_GROUP, mrows, tn), F32)],
        compiler_params=_cparams("parallel", "arbitrary"),
        name="fft_stage1",
    )(f1, x3)


def fft_stage1_inv(g1, b3, v3, v_ofs, gate3, gate_ofs, fbias, *, d, tn):
    mrows, k = g1.shape
    n2 = b3.shape[1]
    nd = d // tn
    return pl.pallas_call(
        _fft1_inv_kernel,
        grid=(n2 // ROW_GROUP, nd),
        in_specs=[pl.BlockSpec((mrows, k), lambda i, j: (0, 0)),
                  pl.BlockSpec((k, ROW_GROUP, tn), lambda i, j: (0, i, j)),
                  pl.BlockSpec((mrows, ROW_GROUP, tn), lambda i, j: (0, i, v_ofs * nd + j)),
                  pl.BlockSpec((mrows, ROW_GROUP, tn), lambda i, j: (0, i, gate_ofs * nd + j)),
                  pl.BlockSpec((1, tn), lambda i, j: (0, j))],
        out_specs=pl.BlockSpec((mrows, ROW_GROUP, tn), lambda i, j: (0, i, j)),
        out_shape=jax.ShapeDtypeStruct((mrows, n2, d), F32),
        scratch_shapes=[pltpu.VMEM((ROW_GROUP, mrows, tn), F32)],
        compiler_params=_cparams("parallel", "arbitrary"),
        name="fft_stage1_inv",
    )(g1, b3, v3, gate3, fbias)


def _fftmid_kernel(a_ref, ak_ref, mf_ref, mi_ref, scale_ref, o_ref):
    half = FFT_N2
    scale = scale_ref[...]
    for s in range(SLAB_TILE):
        mf = mf_ref[s]
        y = _dot(mf, a_ref[s])
        kf = _dot(mf, ak_ref[s])
        yr, yi = y[:half], y[half:]
        kr, ki = kf[:half], kf[half:]
        pr = (yr * kr - yi * ki) * scale
        pi = (yr * ki + yi * kr) * scale
        o_ref[s] = _dot(mi_ref[s], jnp.concatenate([pr, pi], axis=0).astype(BF16)).astype(o_ref.dtype)


def fft_mid(a3, ak3, ak_ofs, mf, mi, scale, *, dc):
    nsp, rows, d = a3.shape
    nd = d // dc
    blk = pl.BlockSpec((SLAB_TILE, rows, dc), lambda s, j: (s, 0, j))
    mat = pl.BlockSpec((SLAB_TILE, rows, rows), lambda s, j: (s, 0, 0))
    return pl.pallas_call(
        _fftmid_kernel,
        grid=(nsp // SLAB_TILE, nd),
        in_specs=[blk, pl.BlockSpec((SLAB_TILE, rows, dc), lambda s, j: (s, 0, ak_ofs * nd + j)), mat, mat,
                  pl.BlockSpec((1, dc), lambda s, j: (0, j))],
        out_specs=blk,
        out_shape=jax.ShapeDtypeStruct((nsp, rows, d), BF16),
        compiler_params=_cparams("parallel", "arbitrary"),
        name="fft_mid",
    )(a3, ak3, mf, mi, scale)


def _fft_tables(seq_len):
    n = 2 * seq_len
    n2 = FFT_N2
    n1 = n // n2
    ns = n1 // 2 + 1
    nsp = -(-ns // SLAB_TILE) * SLAB_TILE
    k1 = jnp.arange(nsp, dtype=jnp.int32)
    valid = (k1 < ns)
    m1 = jnp.arange(n1, dtype=jnp.int32)
    ang1 = (2.0 * math.pi / n1) * lax.rem(k1[:, None] * m1[None, :], n1).astype(F32)
    vf = valid[:, None].astype(F32)
    f1 = jnp.stack([jnp.cos(ang1) * vf, -jnp.sin(ang1) * vf], axis=1).reshape(2 * nsp, n1)
    wgt = jnp.where((k1 == 0) | (k1 == n1 // 2), 1.0, 2.0) * valid.astype(F32) / n
    g1 = jnp.stack([jnp.cos(ang1) * wgt[:, None], -jnp.sin(ang1) * wgt[:, None]], axis=1)
    g1 = g1.reshape(2 * nsp, n1).T[: n1 // 2]
    k2 = jnp.arange(n2, dtype=jnp.int32)
    m2 = jnp.arange(n2, dtype=jnp.int32)
    f = k1[:, None, None] + n1 * k2[None, :, None]
    ang2 = (2.0 * math.pi / n) * lax.rem(f * m2[None, None, :], n).astype(F32)
    vm = valid[:, None, None].astype(F32)
    c2, s2 = jnp.cos(ang2) * vm, jnp.sin(ang2) * vm
    mf = jnp.concatenate([jnp.concatenate([c2, s2], axis=2), jnp.concatenate([-s2, c2], axis=2)], axis=1)
    c2t, s2t = jnp.swapaxes(c2, 1, 2), jnp.swapaxes(s2, 1, 2)
    mi = jnp.concatenate([jnp.concatenate([c2t, -s2t], axis=2), jnp.concatenate([s2t, c2t], axis=2)], axis=1)
    return dict(n1=n1, nsp=nsp, f1=f1.astype(BF16), g1=g1.astype(BF16), mf=mf.astype(BF16), mi=mi.astype(BF16))


def long_conv_gate(v, v_cols, gate, gate_cols, fbias, ak3, ak_cols, scale, tabs, *, seq_len, d):
    n1, nsp = tabs["n1"], tabs["nsp"]
    half = n1 // 2
    v3 = v.reshape(half, FFT_N2, v.shape[1])
    g3 = gate.reshape(half, FFT_N2, gate.shape[1])
    a = fft_stage1(tabs["f1"][:, :half], v3, d=d, tn=FFT_COL_TILE, col_ofs=v_cols)
    b3 = fft_mid(a.reshape(nsp, 2 * FFT_N2, d), ak3, ak_cols, tabs["mf"], tabs["mi"], scale, dc=FFT_MID_COL_TILE)
    out = fft_stage1_inv(tabs["g1"], b3.reshape(2 * nsp, FFT_N2, d), v3, v_cols, g3, gate_cols, fbias, d=d,
                         tn=FFT_COL_TILE)
    return out.reshape(seq_len, d)


def _rope_tables(seq_len, scale):
    n_freq = MLA_ROPE // 4
    rows = seq_len // GRID_W
    row = jnp.repeat(jnp.arange(rows, dtype=F32), GRID_W)
    col = jnp.tile(jnp.arange(GRID_W, dtype=F32), rows)
    inv = ROPE_THETA ** (-jnp.arange(n_freq, dtype=F32) / n_freq)
    ang = jnp.stack([row[:, None] * inv, col[:, None] * inv], axis=1)
    cos = jnp.broadcast_to(jnp.cos(ang)[:, :, None, :], (seq_len, 2, 2, n_freq)).reshape(seq_len, MLA_ROPE)
    sin = jnp.broadcast_to(jnp.sin(ang)[:, :, None, :], (seq_len, 2, 2, n_freq)).reshape(seq_len, MLA_ROPE)
    one = jnp.ones((seq_len, MLA_NOPE), F32)
    zero = jnp.zeros((seq_len, MLA_NOPE), F32)
    z64 = jnp.zeros((seq_len, MLA_HEAD_PAD - MLA_NOPE - MLA_ROPE), F32)
    ta = jnp.concatenate([one, cos, z64], axis=1) * scale
    tb = jnp.concatenate([zero, sin, z64], axis=1) * scale
    tc = jnp.concatenate([cos, z64], axis=1)
    ts = jnp.concatenate([sin, z64], axis=1)
    return ta, tb, tc, ts


def _rope_swap_cols(w):
    n_freq = MLA_ROPE // 4
    w4 = w.reshape(w.shape[0], 2, 2, n_freq)
    return jnp.stack([-w4[:, :, 1], w4[:, :, 0]], axis=2).reshape(w.shape[0], MLA_ROPE)


def _even_layer(x, ctx, mods, norm_mix_w, w_in, conv_w, conv_b, dt_bias, a_log, d_skip, ssd_norm_w,
                q_norm_w, w_uq, kv_norm_w, w_ukv, w_o):
    seq_len, d = x.shape
    sh1, sc1, g1 = (mods[0:1, i * d:(i + 1) * d] for i in range(3))
    csh1, csc1 = mods[1:2, 0:d], mods[1:2, d:2 * d]

    o1 = Q_SIDE + SSD_XBC
    o2 = o1 + SSD_DT
    o3 = o2 + MLA_KV_RANK
    w_kr = w_in[:, o3:]
    w_ext = jnp.concatenate([w_in[:, :Q_SIDE], w_in[:, Q_SIDE:o1], w_in[:, o2:o3], w_kr, _rope_swap_cols(w_kr),
                             w_in[:, o1:o2], jnp.zeros((d, P_COLS - P_DT - SSD_DT), F32)], axis=1).astype(BF16)
    zb = jnp.zeros((P_COLS,), F32)
    p_lat = normmm(x, norm_mix_w, sh1, sc1, w_ext, zb, tm=_pick(seq_len, WIDE_ROW_TILES), tn=PROJ_COL_TILE,
                   out_dtype=F32)
    p_ctx = normmm(ctx, norm_mix_w, csh1, csc1, w_ext, zb, tm=CTX_LEN, tn=PROJ_COL_TILE, out_dtype=F32)

    scale = float(MLA_NOPE + MLA_ROPE) ** -0.5 * math.log2(math.e)
    ta, tb, tc, ts = _rope_tables(seq_len, scale)
    wq = w_uq.reshape(MLA_Q_RANK, MLA_HEADS, MLA_NOPE + MLA_ROPE)
    zpad = jnp.zeros((MLA_Q_RANK, MLA_HEADS, MLA_HEAD_PAD - MLA_NOPE - MLA_ROPE), F32)
    wa = jnp.concatenate([wq, zpad], axis=2).reshape(MLA_Q_RANK, -1).astype(BF16)
    wr = wq[:, :, MLA_NOPE:]
    n_freq = MLA_ROPE // 4
    wr4 = wr.reshape(MLA_Q_RANK, MLA_HEADS, 2, 2, n_freq)
    wsw = jnp.stack([-wr4[:, :, :, 1], wr4[:, :, :, 0]], axis=3).reshape(MLA_Q_RANK, MLA_HEADS, MLA_ROPE)
    wb = jnp.concatenate([jnp.zeros((MLA_Q_RANK, MLA_HEADS, MLA_NOPE), F32), wsw, zpad], axis=2)
    wb = wb.reshape(MLA_Q_RANK, -1).astype(BF16)
    q = qproj(p_lat, q_norm_w, wa, wb, ta, tb, tm=ROW_TILE)

    k_all, v_all = kv_project(p_lat, p_ctx, tc, ts, kv_norm_w, w_ukv.astype(BF16))
    s_tot = seq_len + CTX_LEN
    o_att = attention(q, k_all, v_all, tq=_pick(seq_len, ATTN_Q_TILES), tk=_pick(s_tot, ATTN_K_TILES))

    xbc_lat = dwconv3(p_lat, conv_w, conv_b, tm=ROW_TILE, tc=SSD_XBC, col0=P_XBC, width=SSD_XBC, act=True)
    xbc_ctx = dwconv3(p_ctx, conv_w, conv_b, tm=CTX_LEN, tc=SSD_XBC, col0=P_XBC, width=SSD_XBC, act=True)
    dt_all = jnp.concatenate([p_lat[:, P_DT:P_DT + SSD_DT], p_ctx[:, P_DT:P_DT + SSD_DT]], axis=0)
    nch = s_tot // SSD_CHUNK
    dt2 = dt_all.reshape(s_tot, 2, SSD_HEADS).transpose(1, 0, 2)
    dt2t = dt2.reshape(2, nch, SSD_CHUNK, SSD_HEADS).transpose(0, 1, 3, 2)
    bias2 = dt_bias.reshape(2, 1, SSD_HEADS)
    bias2t = dt_bias.reshape(2, SSD_HEADS, 1)
    a_neg = -jnp.exp(a_log.astype(F32))
    a2 = a_neg.reshape(2, 1, SSD_HEADS)
    a2t = a_neg.reshape(2, SSD_HEADS, 1)
    lower = jnp.tril(jnp.ones((SSD_CHUNK, SSD_CHUNK), F32))
    tri2 = jnp.stack([lower, lower.T]).astype(BF16)
    yf, yb = ssd_scan(xbc_lat, xbc_ctx, dt2, dt2t, bias2, bias2t, a2, a2t, tri2, n_lat_chunks=seq_len // SSD_CHUNK)

    dsk = jnp.repeat(d_skip[0] + d_skip[1], SSD_HEAD_DIM).reshape(1, SSD_INNER)
    return merge_out(o_att, yf, yb, xbc_lat, p_lat, dsk, ssd_norm_w.reshape(1, SSD_INNER), w_o.astype(BF16), x, g1,
                     tm=MERGE_ROW_TILE, tn=d)


def _odd_layer(x, mods, norm_mix_w, w_in, b_in, short_w, short_b, fw1, fb1, fw_mid, fb_mid, freq, fw_out,
               fbias, w_out, b_out):
    seq_len, d = x.shape
    sh1, sc1, g1 = (mods[0:1, i * d:(i + 1) * d] for i in range(3))
    pc = normmm_conv(x, norm_mix_w, sh1, sc1, w_in.astype(BF16), b_in, short_w, short_b,
                     tm=_pick(seq_len, WIDE_ROW_TILES), tn=PROJ_COL_TILE)

    tabs = _fft_tables(seq_len)
    bands = (HY_EMB - 1) // 2
    fband = jnp.linspace(1e-4, bands - 1, bands, dtype=F32)
    zpad = jnp.zeros((128 - HY_EMB,), F32)
    fvec = jnp.stack([jnp.concatenate([jnp.zeros((1,), F32), fband, fband, zpad]),
                      jnp.concatenate([jnp.zeros((1,), F32), jnp.full((bands,), 0.5 * math.pi, F32),
                                       jnp.full((bands,), math.pi, F32), zpad])])
    w1p = jnp.concatenate([fw1.astype(F32), jnp.zeros((128 - HY_EMB, HY_HID), F32)], axis=0).astype(BF16)
    lo = math.log(HY_SLOW_DECAY) / HY_TARGET
    hi = math.log(HY_FAST_DECAY) / HY_TARGET
    delta = jnp.abs(jnp.linspace(lo, hi, d, dtype=F32)).reshape(1, d)
    n_ord = fw_out.shape[1]
    wo = jnp.transpose(fw_out, (2, 0, 1, 3)).reshape(2, HY_HID, n_ord * d).astype(BF16)
    two = lambda a: jnp.concatenate([a, a], axis=-1)
    zmid = jnp.zeros_like(fw_mid)
    wm_bd = jnp.concatenate([jnp.concatenate([fw_mid, zmid], axis=2), jnp.concatenate([zmid, fw_mid], axis=2)],
                            axis=1).astype(BF16)
    hid = filter_hidden(fvec, w1p, two(fb1.reshape(1, HY_HID)), wm_bd, two(fb_mid.reshape(-1, 1, HY_HID)),
                        two(freq.reshape(1, HY_HID)), seq_len=seq_len, tr=FILTER_ROW_TILE)
    ak, kabs = filter_stage1(tabs["f1"], hid.reshape(FFT_N2, tabs["n1"], HY_HID), wo, jnp.tile(delta, (1, n_ord)),
                             seq_len=seq_len, tn=FFT_COL_TILE)
    ak3 = ak.reshape(tabs["nsp"], 2 * FFT_N2, n_ord * d)
    y_cols = 2
    y = None
    for i in range(n_ord):
        scale = 1.0 / kabs[:, i * d:(i + 1) * d]
        fb = fbias[i].reshape(1, d)
        if y is None:
            y = long_conv_gate(pc, y_cols, pc, i, fb, ak3, i, scale, tabs, seq_len=seq_len, d=d)
        else:
            y = long_conv_gate(y, 0, pc, i, fb, ak3, i, scale, tabs, seq_len=seq_len, d=d)
    y = y.reshape(seq_len, d)
    return mm_res(y, w_out.astype(BF16), b_out, x, g1, tm=ROW_TILE, tn=d)


def kernel(x, c, ctx, c_ctx, mod_w, mod_b, norm_mix_w, norm_ffn_w, ffn_w1, ffn_w3, ffn_w2, ev_w_in, ev_conv_w, ev_conv_b, ev_dt_bias, ev_a_log, ev_d_skip, ev_ssd_norm_w, ev_q_norm_w, ev_w_uq, ev_kv_norm_w, ev_w_ukv, ev_w_o, hy_w_in, hy_b_in, hy_short_w, hy_short_b, hy_fw1, hy_fb1, hy_fw_mid, hy_fb_mid, hy_freq, hy_fw_out, hy_fbias, hy_w_out, hy_b_out, final_norm_w):
    assert x.shape[0] == 1 and mod_w.shape[0] == 2
    xs = x[0]
    xc = ctx[0]
    d = xs.shape[1]
    vecs = jnp.concatenate([c.reshape(1, d), c_ctx.reshape(1, d), jnp.zeros((6, d), F32)], axis=0)
    depth = mod_w.shape[0]
    w1_b, w3_b, w2_b = ffn_w1.astype(BF16), ffn_w3.astype(BF16), ffn_w2.astype(BF16)
    for i in range(depth):
        mods = adaln_vectors(vecs, mod_w, mod_b, i)
        sh2, sc2, g2 = (mods[0:1, j * d:(j + 1) * d] for j in range(3, 6))
        if i % 2 == 0:
            e = i // 2
            xs = _even_layer(xs, xc, mods, norm_mix_w[i], ev_w_in[e], ev_conv_w[e], ev_conv_b[e], ev_dt_bias[e],
                             ev_a_log[e], ev_d_skip[e], ev_ssd_norm_w[e], ev_q_norm_w[e], ev_w_uq[e],
                             ev_kv_norm_w[e], ev_w_ukv[e], ev_w_o[e])
        else:
            o = i // 2
            xs = _odd_layer(xs, mods, norm_mix_w[i], hy_w_in[o], hy_b_in[o], hy_short_w[o], hy_short_b[o],
                            hy_fw1[o], hy_fb1[o], hy_fw_mid[o], hy_fb_mid[o], hy_freq[o], hy_fw_out[o],
                            hy_fbias[o], hy_w_out[o], hy_b_out[o])
        xs = ffn(xs, norm_ffn_w[i], sh2, sc2, g2, w1_b, w3_b, w2_b, final_norm_w, layer=i, tm=ROW_TILE, tf=FFN_COL_TILE,
                 final_norm=(i == depth - 1))
    return xs[None]
```
